```python
import math
import jax, jax.numpy as jnp
from jax import lax
import numpy as np

D_MODEL = 1024
BATCH = 4
SEQ = 8192
DEPTH = 1

HEAD_DIM = 64
N_Q_HEADS = 8
N_KV_HEADS = 2
Q_PER_KV = N_Q_HEADS // N_KV_HEADS
ATTN_WIDTH = N_Q_HEADS * HEAD_DIM
KV_WIDTH = N_KV_HEADS * HEAD_DIM
WINDOW = 128
ATTN_BLOCK = 128
ROPE_THETA = 10000.0
SSM_HEADS = 8
SSM_HEAD_DIM = 64
SSM_WIDTH = SSM_HEADS * SSM_HEAD_DIM
SSM_GROUPS = 2
HEADS_PER_GROUP = SSM_HEADS // SSM_GROUPS
D_STATE = 128
CONV_WIDTH = 4
CHUNK = 128
CONV_CH = SSM_WIDTH + 2 * SSM_GROUPS * D_STATE
MIX_WIDTH = ATTN_WIDTH + SSM_WIDTH
IN_PROJ = ATTN_WIDTH + 2 * KV_WIDTH + SSM_WIDTH + CONV_CH + SSM_HEADS
D_FF = -(-8 * D_MODEL // (3 * 256)) * 256
N_MOD = 6
EPS = 1e-6

kernel_name = "hymba_swa_sink_ssd_adaln_layer"


def rmsnorm(x, w):
    xf = x.astype(jnp.float32)
    y = xf * lax.rsqrt(jnp.mean(xf * xf, axis=-1, keepdims=True) + EPS)
    return (y * w.astype(jnp.float32)).astype(x.dtype)


def modulate(h, shift, scale):
    return h * (1.0 + scale[:, None, :]) + shift[:, None, :]


def rope(t, positions):
    half = HEAD_DIM // 2
    inv_freq = ROPE_THETA ** (-jnp.arange(half, dtype=jnp.float32) / half)
    ang = positions.astype(jnp.float32)[..., None] * inv_freq
    cos = jnp.cos(ang)[:, :, None, :]
    sin = jnp.sin(ang)[:, :, None, :]
    tf = t.astype(jnp.float32)
    t1, t2 = tf[..., :half], tf[..., half:]
    out = jnp.concatenate([t1 * cos - t2 * sin, t2 * cos + t1 * sin], axis=-1)
    return out.astype(t.dtype)


def sliding_window_attention(q, k, v, sinks):
    b, s = q.shape[0], q.shape[1]
    nb = s // ATTN_BLOCK
    qb = q.reshape(b, nb, ATTN_BLOCK, N_KV_HEADS, Q_PER_KV, HEAD_DIM)

    def band(t):
        tb = t.reshape(b, nb, ATTN_BLOCK, N_KV_HEADS, HEAD_DIM)
        prev = jnp.pad(tb, ((0, 0), (1, 0), (0, 0), (0, 0), (0, 0)))[:, :-1]
        return jnp.concatenate([prev, tb], axis=2)

    kb, vb = band(k), band(v)
    scores = jnp.einsum('bnqhgd,bnkhd->bnhgqk', qb, kb).astype(jnp.float32)
    scores = scores * (1.0 / math.sqrt(HEAD_DIM))
    blk = jnp.arange(nb)[:, None] * ATTN_BLOCK
    qpos = blk + jnp.arange(ATTN_BLOCK)[None, :]
    kpos = blk - ATTN_BLOCK + jnp.arange(2 * ATTN_BLOCK)[None, :]
    diff = qpos[:, :, None] - kpos[:, None, :]
    mask = (diff >= 0) & (diff < WINDOW) & (kpos[:, None, :] >= 0)
    scores = jnp.where(mask[None, :, None, None], scores, -jnp.inf)
    sink = sinks.astype(jnp.float32).reshape(N_KV_HEADS, Q_PER_KV)[None, None, :, :, None, None]
    m = jnp.maximum(jnp.max(scores, axis=-1, keepdims=True), sink)
    p = jnp.exp(scores - m)
    denom = jnp.sum(p, axis=-1, keepdims=True) + jnp.exp(sink - m)
    probs = (p / denom).astype(v.dtype)
    out = jnp.einsum('bnhgqk,bnkhd->bnqhgd', probs, vb)
    return out.reshape(b, s, ATTN_WIDTH)


def causal_depthwise_conv(u, w, bias):
    out = lax.conv_general_dilated(
        u, w[:, None, :].astype(u.dtype), window_strides=(1,),
        padding=[(CONV_WIDTH - 1, 0)], dimension_numbers=('NWC', 'WIO', 'NWC'),
        feature_group_count=u.shape[-1])
    return out + bias.astype(u.dtype)


def ssd_chunked_scan(xs, dt, A, Bm, Cm, d_skip):
    b, s = xs.shape[0], xs.shape[1]
    nc = s // CHUNK
    xf = xs.astype(jnp.float32)
    xdt = (xf * dt[..., None]).reshape(b, nc, CHUNK, SSM_GROUPS, HEADS_PER_GROUP, SSM_HEAD_DIM)
    a = (dt * A).reshape(b, nc, CHUNK, SSM_GROUPS, HEADS_PER_GROUP)
    a_cs = jnp.cumsum(a, axis=2)
    Bc = Bm.astype(jnp.float32).reshape(b, nc, CHUNK, SSM_GROUPS, D_STATE)
    Cc = Cm.astype(jnp.float32).reshape(b, nc, CHUNK, SSM_GROUPS, D_STATE)
    causal = jnp.tril(jnp.ones((CHUNK, CHUNK), dtype=bool))[None, None, :, :, None, None]
    seg = a_cs[:, :, :, None] - a_cs[:, :, None, :]
    decay = jnp.exp(jnp.where(causal, seg, -jnp.inf))
    cb = jnp.einsum('bclgn,bcsgn->bclsg', Cc, Bc)
    y_diag = jnp.einsum('bclsg,bclsgj,bcsgjp->bclgjp', cb, decay, xdt)
    decay_to_end = jnp.exp(a_cs[:, :, -1:] - a_cs)
    states = jnp.einsum('bclgn,bclgj,bclgjp->bcgjpn', Bc, decay_to_end, xdt)
    chunk_decay = jnp.exp(a_cs[:, :, -1])

    def step(h, inp):
        st, dec = inp
        return h * dec[..., None, None] + st, h

    init = jnp.zeros((b, SSM_GROUPS, HEADS_PER_GROUP, SSM_HEAD_DIM, D_STATE), jnp.float32)
    _, prev = lax.scan(step, init, (jnp.moveaxis(states, 1, 0), jnp.moveaxis(chunk_decay, 1, 0)))
    prev = jnp.moveaxis(prev, 0, 1)
    y_off = jnp.einsum('bclgn,bcgjpn,bclgj->bclgjp', Cc, prev, jnp.exp(a_cs))
    y = (y_diag + y_off).reshape(b, s, SSM_HEADS, SSM_HEAD_DIM)
    return y + xf * d_skip.astype(jnp.float32)[:, None]


def hybrid_mixer(h, positions, w_in, conv_w, conv_b, dt_bias, a_log, d_skip, sinks, ssm_norm_w, w_out):
    b, s = h.shape[0], h.shape[1]
    proj = h @ w_in
    o1 = ATTN_WIDTH
    o2 = o1 + KV_WIDTH
    o3 = o2 + KV_WIDTH
    o4 = o3 + SSM_WIDTH
    o5 = o4 + CONV_CH
    q, k, v, z, xbc, dt_raw = jnp.split(proj, [o1, o2, o3, o4, o5], axis=-1)
    q = rope(q.reshape(b, s, N_Q_HEADS, HEAD_DIM), positions)
    k = rope(k.reshape(b, s, N_KV_HEADS, HEAD_DIM), positions)
    v = v.reshape(b, s, N_KV_HEADS, HEAD_DIM)
    attn = sliding_window_attention(q, k, v, sinks)
    xbc = jax.nn.silu(causal_depthwise_conv(xbc, conv_w, conv_b))
    xs, Bm, Cm = jnp.split(xbc, [SSM_WIDTH, SSM_WIDTH + SSM_GROUPS * D_STATE], axis=-1)
    xs = xs.reshape(b, s, SSM_HEADS, SSM_HEAD_DIM)
    Bm = Bm.reshape(b, s, SSM_GROUPS, D_STATE)
    Cm = Cm.reshape(b, s, SSM_GROUPS, D_STATE)
    dt = jax.nn.softplus(dt_raw.astype(jnp.float32) + dt_bias.astype(jnp.float32))
    A = -jnp.exp(a_log.astype(jnp.float32))
    y = ssd_chunked_scan(xs, dt, A, Bm, Cm, d_skip).reshape(b, s, SSM_WIDTH)
    y = y * jax.nn.silu(z.astype(jnp.float32))
    yg = y.reshape(b, s, SSM_GROUPS, SSM_WIDTH // SSM_GROUPS)
    yg = yg * lax.rsqrt(jnp.mean(yg * yg, axis=-1, keepdims=True) + EPS)
    y = (yg.reshape(b, s, SSM_WIDTH) * ssm_norm_w.astype(jnp.float32)).astype(h.dtype)
    return jnp.concatenate([attn.astype(h.dtype), y], axis=-1) @ w_out


def swiglu(h, w_gate_up, w_down):
    gu = h @ w_gate_up
    g, u = jnp.split(gu, 2, axis=-1)
    return (jax.nn.silu(g) * u) @ w_down


def setup_inputs(seed: int = 0) -> dict:
    key = jax.random.key(seed)
    ks = jax.random.split(key, 20)
    f32 = jnp.float32
    L = DEPTH
    x = jax.random.normal(ks[0], (BATCH, SEQ, D_MODEL), f32)
    c = jax.random.normal(ks[1], (BATCH, D_MODEL), f32)
    offset = jax.random.randint(ks[2], (BATCH, 1), 0, 4096, dtype=jnp.int32)
    positions = (jnp.arange(SEQ, dtype=jnp.int32)[None, :] + offset).astype(jnp.int32)
    w_ada = jax.random.normal(ks[3], (L, D_MODEL, N_MOD * D_MODEL), f32) * (0.5 * D_MODEL ** -0.5)
    b_ada = 0.01 * jax.random.normal(ks[4], (L, N_MOD * D_MODEL), f32)
    norm1_w = 1.0 + 0.02 * jax.random.normal(ks[5], (L, D_MODEL), f32)
    w_in = jax.random.normal(ks[6], (L, D_MODEL, IN_PROJ), f32) * D_MODEL ** -0.5
    conv_w = jax.random.normal(ks[7], (L, CONV_WIDTH, CONV_CH), f32) * CONV_WIDTH ** -0.5
    conv_b = 0.01 * jax.random.normal(ks[8], (L, CONV_CH), f32)
    dt0 = jnp.exp(jax.random.uniform(ks[9], (L, SSM_HEADS), f32, math.log(1e-3), math.log(1e-1)))
    dt_bias = dt0 + jnp.log(-jnp.expm1(-dt0))
    a_log = jnp.log(jax.random.uniform(ks[10], (L, SSM_HEADS), f32, 1.0, 16.0))
    d_skip = 1.0 + 0.1 * jax.random.normal(ks[11], (L, SSM_HEADS), f32)
    attn_sinks = jax.random.normal(ks[12], (L, N_Q_HEADS), f32)
    ssm_norm_w = 1.0 + 0.02 * jax.random.normal(ks[13], (L, SSM_WIDTH), f32)
    w_out = jax.random.normal(ks[14], (L, MIX_WIDTH, D_MODEL), f32) * MIX_WIDTH ** -0.5
    norm2_w = 1.0 + 0.02 * jax.random.normal(ks[15], (L, D_MODEL), f32)
    w_gate_up = jax.random.normal(ks[16], (L, D_MODEL, 2 * D_FF), f32) * D_MODEL ** -0.5
    w_down = jax.random.normal(ks[17], (L, D_FF, D_MODEL), f32) * D_FF ** -0.5
    final_norm_w = 1.0 + 0.02 * jax.random.normal(ks[18], (D_MODEL,), f32)
    return {"x": x, "c": c, "positions": positions, "w_ada": w_ada, "b_ada": b_ada,
            "norm1_w": norm1_w, "w_in": w_in, "conv_w": conv_w, "conv_b": conv_b,
            "dt_bias": dt_bias, "a_log": a_log, "d_skip": d_skip, "attn_sinks": attn_sinks,
            "ssm_norm_w": ssm_norm_w, "w_out": w_out, "norm2_w": norm2_w,
            "w_gate_up": w_gate_up, "w_down": w_down, "final_norm_w": final_norm_w}


def reference(x, c, positions, w_ada, b_ada, norm1_w, w_in, conv_w, conv_b, dt_bias, a_log,
              d_skip, attn_sinks, ssm_norm_w, w_out, norm2_w, w_gate_up, w_down, final_norm_w):
    for layer in range(DEPTH):
        mod = jax.nn.silu(c) @ w_ada[layer] + b_ada[layer]
        shift1, scale1, gate1, shift2, scale2, gate2 = jnp.split(mod, N_MOD, axis=-1)
        h = modulate(rmsnorm(x, norm1_w[layer]), shift1, scale1)
        x = x + gate1[:, None, :] * hybrid_mixer(
            h, positions, w_in[layer], conv_w[layer], conv_b[layer], dt_bias[layer],
            a_log[layer], d_skip[layer], attn_sinks[layer], ssm_norm_w[layer], w_out[layer])
        h = modulate(rmsnorm(x, norm2_w[layer]), shift2, scale2)
        x = x + gate2[:, None, :] * swiglu(h, w_gate_up[layer], w_down[layer])
    return rmsnorm(x, final_norm_w)
```

```python
import functools
import math

import jax
import jax.numpy as jnp
import numpy as np
from jax import lax
from jax.experimental import pallas as pl
from jax.experimental.pallas import tpu as pltpu

D_MODEL = 1024
HEAD_DIM = 64
N_Q_HEADS = 8
N_KV_HEADS = 2
Q_PER_KV = N_Q_HEADS // N_KV_HEADS
ATTN_WIDTH = N_Q_HEADS * HEAD_DIM
KV_WIDTH = N_KV_HEADS * HEAD_DIM
WINDOW = 128
ROPE_THETA = 10000.0
SSM_HEADS = 8
SSM_HEAD_DIM = 64
SSM_WIDTH = SSM_HEADS * SSM_HEAD_DIM
SSM_GROUPS = 2
HEADS_PER_GROUP = SSM_HEADS // SSM_GROUPS
GROUP_WIDTH = HEADS_PER_GROUP * SSM_HEAD_DIM
D_STATE = 128
CONV_WIDTH = 4
CHUNK = 128
CONV_CH = SSM_WIDTH + 2 * SSM_GROUPS * D_STATE
MIX_WIDTH = ATTN_WIDTH + SSM_WIDTH
D_FF = 2816
N_MOD = 6
EPS = 1e-6

LANES = 128
SUBLANES = 8
HEAD_SLOT = 8
MASK_VALUE = -1e30
VMEM_LIMIT_BYTES = 56 * 1024 * 1024

MIXER_TILE = 512
FFN_TILE = 512
MOD_TILE = 1024

F32 = jnp.float32
BF16 = jnp.bfloat16

_EXPAND_ITEMS = (("acs", 3, LANES), ("ea", 2, SSM_HEAD_DIM), ("w", 2, SSM_HEAD_DIM),
                 ("dt", 2, SSM_HEAD_DIM))


def _expand_layout():
    slot = 0
    col = 0
    layout = {}
    for name, parts, rep in _EXPAND_ITEMS:
        layout[name] = (slot, parts, col, rep * SSM_HEADS)
        slot += parts
        col += rep * SSM_HEADS
    return layout, slot, col


def _expand_matrix():
    layout, nslots, ncols = _expand_layout()
    assert nslots * HEAD_SLOT <= LANES
    e = np.zeros((LANES, ncols), np.float32)
    for name, parts, rep in _EXPAND_ITEMS:
        slot0, _, col0, _ = layout[name]
        for p in range(parts):
            for hh in range(SSM_HEADS):
                e[(slot0 + p) * HEAD_SLOT + hh, col0 + hh * rep:col0 + (hh + 1) * rep] = 1.0
    return jnp.asarray(e, BF16)


def _silu(x):
    return x * (1.0 / (1.0 + jnp.exp(-x)))


def _softplus(x):
    return jnp.maximum(x, 0.0) + jnp.log1p(jnp.exp(-jnp.abs(x)))


def _dot(a, b):
    return jnp.dot(a, b, preferred_element_type=F32)


def _dot_nt(a, b):
    return lax.dot_general(a, b, (((1,), (1,)), ((), ())), preferred_element_type=F32)


def _dot_tn(a, b):
    return lax.dot_general(a, b, (((0,), (0,)), ((), ())), preferred_element_type=F32)


def _rmsnorm_mod(x, norm_w, shift, scale):
    ms = jnp.mean(x * x, axis=-1, keepdims=True)
    y = x * lax.rsqrt(ms + EPS) * norm_w
    return y * (1.0 + scale) + shift


def _mod_kernel(c_ref, w_ref, b_ref, o_ref):
    sc = _silu(c_ref[...]).astype(BF16)
    o_ref[...] = _dot(sc, w_ref[...].astype(BF16)) + b_ref[...]


def _adaln_mod(c, w_ada, b_ada):
    b, d = c.shape
    n = w_ada.shape[1]
    return pl.pallas_call(
        _mod_kernel,
        grid=(n // MOD_TILE,),
        in_specs=[
            pl.BlockSpec((b, d), lambda j: (0, 0)),
            pl.BlockSpec((d, MOD_TILE), lambda j: (0, j)),
            pl.BlockSpec((1, MOD_TILE), lambda j: (0, j)),
        ],
        out_specs=pl.BlockSpec((b, MOD_TILE), lambda j: (0, j)),
        out_shape=jax.ShapeDtypeStruct((b, n), F32),
        compiler_params=pltpu.CompilerParams(
            dimension_semantics=("arbitrary",), vmem_limit_bytes=VMEM_LIMIT_BYTES),
        name="adaln_mod",
    )(c, w_ada, b_ada.reshape(1, n))


def _pack_heads(items):
    lane = lax.broadcasted_iota(jnp.int32, items[0][0].shape, 1)
    acc = jnp.zeros(items[0][0].shape, F32)
    slot = 0
    for v, parts in items:
        r = v
        for _ in range(parts):
            p = r.astype(BF16).astype(F32)
            r = r - p
            shifted = pltpu.roll(p, slot * HEAD_SLOT, axis=1) if slot else p
            sel = (lane >= slot * HEAD_SLOT) & (lane < (slot + 1) * HEAD_SLOT)
            acc = jnp.where(sel, shifted, acc)
            slot += 1
    return acc.astype(BF16)


def _mixer_kernel(x_ref, pos_ref, mod_ref, n1w_ref, invf_ref, wqkv_ref, wz_ref, wxbc_ref,
                  wdt_ref, convw_ref, convb_ref, dtb_ref, alog_ref, dskip_ref, sinks_ref,
                  snw_ref, wout_ref, tril_ref, expand_ref,
                  o_ref,
                  q_scr, k_scr, v_scr, z_scr, xbc_scr, dt_scr, state_scr, mix_scr):
    ts = x_ref.shape[1]
    n_chunks = ts // CHUNK
    i = pl.program_id(1)
    layout, _, _ = _expand_layout()

    @pl.when(i == 0)
    def _():
        k_scr[0:CHUNK, :] = jnp.zeros((CHUNK, KV_WIDTH), BF16)
        v_scr[0:CHUNK, :] = jnp.zeros((CHUNK, KV_WIDTH), BF16)
        xbc_scr[0:SUBLANES, :] = jnp.zeros((SUBLANES, CONV_CH), F32)
        state_scr[...] = jnp.zeros(state_scr.shape, F32)

    mod = mod_ref[0]
    shift1, scale1, gate1 = mod[0:1], mod[1:2], mod[2:3]
    x = x_ref[0]
    h = _rmsnorm_mod(x, n1w_ref[...], shift1, scale1).astype(BF16)

    qkv = _dot(h, wqkv_ref[...])
    z_scr[...] = _dot(h, wz_ref[...])
    xbc_scr[SUBLANES:SUBLANES + ts, :] = _dot(h, wxbc_ref[...])
    dt_scr[...] = _dot(h, wdt_ref[...])

    ang = pos_ref[0].astype(F32) * invf_ref[...]
    cos = jnp.cos(ang)
    sin = jnp.sin(ang)
    lane = lax.broadcasted_iota(jnp.int32, (ts, LANES), 1)
    first_half = (lane % HEAD_DIM) < (HEAD_DIM // 2)
    sin_signed = jnp.where(first_half, -sin, sin)

    def rope(t):
        rot = jnp.where(first_half, pltpu.roll(t, LANES - HEAD_DIM // 2, axis=1),
                        pltpu.roll(t, HEAD_DIM // 2, axis=1))
        return t * cos + rot * sin_signed

    q_scale = 1.0 / math.sqrt(HEAD_DIM)
    for j in range(ATTN_WIDTH // LANES):
        q_scr[:, j * LANES:(j + 1) * LANES] = (
            rope(qkv[:, j * LANES:(j + 1) * LANES]) * q_scale).astype(BF16)
    k_scr[CHUNK:CHUNK + ts, :] = rope(qkv[:, ATTN_WIDTH:ATTN_WIDTH + KV_WIDTH]).astype(BF16)
    v_scr[CHUNK:CHUNK + ts, :] = qkv[:, ATTN_WIDTH + KV_WIDTH:].astype(BF16)

    a_row = -jnp.exp(alog_ref[...])
    tril = tril_ref[...]
    causal = (lax.broadcasted_iota(jnp.int32, (CHUNK, CHUNK), 0)
              >= lax.broadcasted_iota(jnp.int32, (CHUNK, CHUNK), 1))

    rows_q = lax.broadcasted_iota(jnp.int32, (Q_PER_KV * CHUNK, 2 * CHUNK), 0) % CHUNK
    cols_k = lax.broadcasted_iota(jnp.int32, (Q_PER_KV * CHUNK, 2 * CHUNK), 1)
    band = (cols_k > rows_q) & (cols_k <= rows_q + WINDOW)

    def chunk_body(c, carry):
        r0 = pl.multiple_of(c * CHUNK, CHUNK)
        k_min = jnp.where((i == 0) & (c == 0), CHUNK, 0)
        valid = band & (cols_k >= k_min)
        kband = k_scr[pl.ds(r0, 2 * CHUNK), :]
        vband = v_scr[pl.ds(r0, 2 * CHUNK), :]
        qblk = q_scr[pl.ds(r0, CHUNK), :]
        for g in range(N_KV_HEADS):
            qg = jnp.concatenate(
                [qblk[:, (g * Q_PER_KV + j) * HEAD_DIM:(g * Q_PER_KV + j + 1) * HEAD_DIM]
                 for j in range(Q_PER_KV)], axis=0)
            s = _dot_nt(qg, kband[:, g * HEAD_DIM:(g + 1) * HEAD_DIM])
            s = jnp.where(valid, s, MASK_VALUE)
            sink = jnp.concatenate(
                [jnp.full((CHUNK, 1), sinks_ref[g * Q_PER_KV + j], F32) for j in range(Q_PER_KV)],
                axis=0)
            m = jnp.maximum(jnp.max(s, axis=-1, keepdims=True), sink)
            p = jnp.exp(s - m)
            denom = jnp.sum(p, axis=-1, keepdims=True) + jnp.exp(sink - m)
            o = _dot(p.astype(BF16), vband[:, g * HEAD_DIM:(g + 1) * HEAD_DIM])
            o = o / denom
            og = jnp.concatenate([o[j * CHUNK:(j + 1) * CHUNK] for j in range(Q_PER_KV)], axis=1)
            mix_scr[pl.ds(r0, CHUNK), g * Q_PER_KV * HEAD_DIM:(g + 1) * Q_PER_KV * HEAD_DIM] = (
                og.astype(BF16))

        win = xbc_scr[pl.ds(r0, SUBLANES + CHUNK), :]
        conv = convb_ref[...]
        for t in range(CONV_WIDTH):
            lo = SUBLANES - (CONV_WIDTH - 1) + t
            conv = conv + convw_ref[t:t + 1, :] * win[lo:lo + CHUNK]
        xbc = _silu(conv)
        xs = xbc[:, :SSM_WIDTH]
        b_mat = xbc[:, SSM_WIDTH:SSM_WIDTH + SSM_GROUPS * D_STATE]
        c_mat = xbc[:, SSM_WIDTH + SSM_GROUPS * D_STATE:]

        dt = _softplus(dt_scr[pl.ds(r0, CHUNK), :] + dtb_ref[...])
        a = dt * a_row
        a_cs = jnp.zeros((CHUNK, LANES), F32)
        r = a
        for _ in range(3):
            part = r.astype(BF16)
            r = r - part.astype(F32)
            a_cs = a_cs + _dot(tril, part)
        a_last = a_cs[CHUNK - 1:CHUNK, :]
        ea = jnp.exp(a_cs)
        w_end = dt * jnp.exp(a_last - a_cs)
        a_cs_t = a_cs.T
        packed = _pack_heads([(a_cs, 3), (ea, 2), (w_end, 2), (dt, 2)])
        ex = _dot(packed, expand_ref[...])

        def ex_block(name):
            _, _, col0, width = layout[name]
            return ex[:, col0:col0 + width]

        acs_b = ex_block("acs")
        ea_e = ex_block("ea")
        xdt = (xs * ex_block("dt")).astype(BF16)
        xw = (xs * ex_block("w")).astype(BF16)
        ys = []
        for g in range(SSM_GROUPS):
            gs = slice(g * GROUP_WIDTH, (g + 1) * GROUP_WIDTH)
            bg = b_mat[:, g * D_STATE:(g + 1) * D_STATE].astype(BF16)
            cg = c_mat[:, g * D_STATE:(g + 1) * D_STATE].astype(BF16)
            cb = _dot_nt(cg, bg)
            s_old = state_scr[g]
            y_off = _dot(cg, s_old.astype(BF16)) * ea_e[:, gs]
            yd = []
            for j in range(HEADS_PER_GROUP):
                hh = g * HEADS_PER_GROUP + j
                seg = acs_b[:, hh * LANES:(hh + 1) * LANES] - a_cs_t[hh:hh + 1, :]
                dec = jnp.exp(jnp.where(causal, seg, MASK_VALUE))
                yd.append(_dot((cb * dec).astype(BF16),
                               xdt[:, hh * SSM_HEAD_DIM:(hh + 1) * SSM_HEAD_DIM]))
            y_g = jnp.concatenate(yd, axis=1) + y_off + xs[:, gs] * dskip_ref[:, gs]
            state_scr[g] = s_old * ea_e[CHUNK - 1:CHUNK, gs] + _dot_tn(bg, xw[:, gs])
            y_g = y_g * _silu(z_scr[pl.ds(r0, CHUNK), gs])
            y_g = y_g * lax.rsqrt(jnp.mean(y_g * y_g, axis=-1, keepdims=True) + EPS)
            ys.append(y_g * snw_ref[:, gs])
        mix_scr[pl.ds(r0, CHUNK), ATTN_WIDTH:] = jnp.concatenate(ys, axis=1).astype(BF16)
        return carry

    lax.fori_loop(0, n_chunks, chunk_body, 0)

    k_scr[0:CHUNK, :] = k_scr[ts:ts + CHUNK, :]
    v_scr[0:CHUNK, :] = v_scr[ts:ts + CHUNK, :]
    xbc_scr[0:SUBLANES, :] = xbc_scr[ts:ts + SUBLANES, :]

    o_ref[0] = x + gate1 * _dot(mix_scr[...], wout_ref[...])


def _const_spec(shape):
    nd = len(shape)
    return pl.BlockSpec(shape, lambda b, i: (0,) * nd, pipeline_mode=pl.Buffered(1))


def _mixer(x, pos3, mod3, n1w, invf, wqkv, wz, wxbc, wdt, convw, convb, dtb, alog, dskip, sinks,
           snw, wout, tril, expand):
    b, s, d = x.shape
    ts = MIXER_TILE
    in_specs = [
        pl.BlockSpec((1, ts, d), lambda bb, i: (bb, i, 0)),
        pl.BlockSpec((1, ts, 1), lambda bb, i: (bb, i, 0)),
        pl.BlockSpec((1, N_MOD, d), lambda bb, i: (bb, 0, 0)),
        _const_spec(n1w.shape), _const_spec(invf.shape), _const_spec(wqkv.shape),
        _const_spec(wz.shape), _const_spec(wxbc.shape), _const_spec(wdt.shape),
        _const_spec(convw.shape), _const_spec(convb.shape), _const_spec(dtb.shape),
        _const_spec(alog.shape), _const_spec(dskip.shape),
        pl.BlockSpec(memory_space=pltpu.SMEM),
        _const_spec(snw.shape), _const_spec(wout.shape), _const_spec(tril.shape),
        _const_spec(expand.shape),
    ]
    scratch = [
        pltpu.VMEM((ts, ATTN_WIDTH), BF16),
        pltpu.VMEM((CHUNK + ts, KV_WIDTH), BF16),
        pltpu.VMEM((CHUNK + ts, KV_WIDTH), BF16),
        pltpu.VMEM((ts, SSM_WIDTH), F32),
        pltpu.VMEM((SUBLANES + ts, CONV_CH), F32),
        pltpu.VMEM((ts, LANES), F32),
        pltpu.VMEM((SSM_GROUPS, D_STATE, GROUP_WIDTH), F32),
        pltpu.VMEM((ts, MIX_WIDTH), BF16),
    ]
    return pl.pallas_call(
        _mixer_kernel,
        grid=(b, s // ts),
        in_specs=in_specs,
        out_specs=pl.BlockSpec((1, ts, d), lambda bb, i: (bb, i, 0)),
        out_shape=jax.ShapeDtypeStruct((b, s, d), F32),
        scratch_shapes=scratch,
        compiler_params=pltpu.CompilerParams(
            dimension_semantics=("arbitrary", "arbitrary"), vmem_limit_bytes=VMEM_LIMIT_BYTES),
        name="mixer",
    )(x, pos3, mod3, n1w, invf, wqkv, wz, wxbc, wdt, convw, convb, dtb, alog, dskip, sinks, snw,
      wout, tril, expand)


def _ffn_kernel(x_ref, mod_ref, n2w_ref, wg_ref, wu_ref, wd_ref, nfw_ref, o_ref):
    mod = mod_ref[0]
    shift2, scale2, gate2 = mod[3:4], mod[4:5], mod[5:6]
    x = x_ref[0]
    h = _rmsnorm_mod(x, n2w_ref[...], shift2, scale2).astype(BF16)
    g = _dot(h, wg_ref[...])
    u = _dot(h, wu_ref[...])
    act = (_silu(g) * u).astype(BF16)
    x2 = x + gate2 * _dot(act, wd_ref[...])
    ms = jnp.mean(x2 * x2, axis=-1, keepdims=True)
    o_ref[0] = x2 * lax.rsqrt(ms + EPS) * nfw_ref[...]


def _ffn(x1, mod3, n2w, wg, wu, wd, nfw):
    b, s, d = x1.shape
    tm = FFN_TILE
    return pl.pallas_call(
        _ffn_kernel,
        grid=(b, s // tm),
        in_specs=[
            pl.BlockSpec((1, tm, d), lambda bb, i: (bb, i, 0)),
            pl.BlockSpec((1, N_MOD, d), lambda bb, i: (bb, 0, 0)),
            _const_spec(n2w.shape), _const_spec(wg.shape), _const_spec(wu.shape),
            _const_spec(wd.shape), _const_spec(nfw.shape),
        ],
        out_specs=pl.BlockSpec((1, tm, d), lambda bb, i: (bb, i, 0)),
        out_shape=jax.ShapeDtypeStruct((b, s, d), F32),
        compiler_params=pltpu.CompilerParams(
            dimension_semantics=("arbitrary", "arbitrary"), vmem_limit_bytes=VMEM_LIMIT_BYTES),
        name="ffn",
    )(x1, mod3, n2w, wg, wu, wd, nfw)


def _pad_lanes(v, fill=0.0):
    return jnp.pad(v.astype(F32), (0, LANES - v.shape[0]), constant_values=fill).reshape(1, LANES)


def kernel(x, c, positions, w_ada, b_ada, norm1_w, w_in, conv_w, conv_b, dt_bias, a_log, d_skip,
           attn_sinks, ssm_norm_w, w_out, norm2_w, w_gate_up, w_down, final_norm_w):
    b, s, d = x.shape
    depth = w_ada.shape[0]
    half = HEAD_DIM // 2
    inv_freq = ROPE_THETA ** (-jnp.arange(half, dtype=F32) / half)
    invf = jnp.tile(inv_freq, LANES // half).reshape(1, LANES)
    tril = jnp.asarray(np.tril(np.ones((CHUNK, CHUNK), np.float32)), BF16)
    expand = _expand_matrix()
    pos3 = positions.reshape(b, s, 1)
    o1 = ATTN_WIDTH + 2 * KV_WIDTH
    o2 = o1 + SSM_WIDTH
    o3 = o2 + CONV_CH
    assert depth == 1, "the final norm is fused into the (single) layer's ffn call"
    layer = 0
    mod3 = _adaln_mod(c, w_ada[layer], b_ada[layer]).reshape(b, N_MOD, d)
    w_in_l = w_in[layer].astype(BF16)
    wdt = jnp.pad(w_in_l[:, o3:], ((0, 0), (0, LANES - SSM_HEADS)))
    x1 = _mixer(
        x, pos3, mod3, norm1_w[layer].reshape(1, d), invf,
        w_in_l[:, :o1], w_in_l[:, o1:o2], w_in_l[:, o2:o3], wdt,
        conv_w[layer], conv_b[layer].reshape(1, CONV_CH),
        _pad_lanes(dt_bias[layer]), _pad_lanes(a_log[layer]),
        jnp.repeat(d_skip[layer].astype(F32), SSM_HEAD_DIM).reshape(1, SSM_WIDTH),
        attn_sinks[layer].astype(F32),
        ssm_norm_w[layer].reshape(1, SSM_WIDTH), w_out[layer].astype(BF16), tril, expand)
    wgu = w_gate_up[layer].astype(BF16)
    return _ffn(x1, mod3, norm2_w[layer].reshape(1, d), wgu[:, :D_FF], wgu[:, D_FF:],
                w_down[layer].astype(BF16), final_norm_w.reshape(1, d))
```

```python
import math

import jax
import jax.numpy as jnp
import numpy as np
from jax import lax
from jax.experimental import pallas as pl
from jax.experimental.pallas import tpu as pltpu

D_MODEL = 1024
HEAD_DIM = 64
N_Q_HEADS = 8
N_KV_HEADS = 2
Q_PER_KV = N_Q_HEADS // N_KV_HEADS
ATTN_WIDTH = N_Q_HEADS * HEAD_DIM
KV_WIDTH = N_KV_HEADS * HEAD_DIM
WINDOW = 128
ROPE_THETA = 10000.0
SSM_HEADS = 8
SSM_HEAD_DIM = 64
SSM_WIDTH = SSM_HEADS * SSM_HEAD_DIM
SSM_GROUPS = 2
HEADS_PER_GROUP = SSM_HEADS // SSM_GROUPS
GROUP_WIDTH = HEADS_PER_GROUP * SSM_HEAD_DIM
D_STATE = 128
CONV_WIDTH = 4
CHUNK = 128
CONV_CH = SSM_WIDTH + 2 * SSM_GROUPS * D_STATE
MIX_WIDTH = ATTN_WIDTH + SSM_WIDTH
D_FF = 2816
N_MOD = 6
EPS = 1e-6

LANES = 128
SUBLANES = 8
HEAD_SLOT = 8
MASK_VALUE = -1e30
VMEM_LIMIT_BYTES = 56 * 1024 * 1024

MIXER_TILE = 512
FFN_TILE = 512
MOD_TILE = 1024

F32 = jnp.float32
BF16 = jnp.bfloat16

STAGE_ORDER = ("s", "a0", "a1", "s", "a0", "a1", "s", "a0", "a1", "s", "a0", "s", "a1", "s", "s")

_EXPAND_ITEMS = (("acs", 3, LANES), ("ea", 2, SSM_HEAD_DIM), ("w", 2, SSM_HEAD_DIM),
                 ("dt", 2, SSM_HEAD_DIM))


def _expand_layout():
    slot = 0
    col = 0
    layout = {}
    for name, parts, rep in _EXPAND_ITEMS:
        layout[name] = (slot, parts, col, rep * SSM_HEADS)
        slot += parts
        col += rep * SSM_HEADS
    return layout, slot, col


def _expand_matrix():
    layout, nslots, ncols = _expand_layout()
    assert nslots * HEAD_SLOT <= LANES
    e = np.zeros((LANES, ncols), np.float32)
    for name, parts, rep in _EXPAND_ITEMS:
        slot0, _, col0, _ = layout[name]
        for p in range(parts):
            for hh in range(SSM_HEADS):
                e[(slot0 + p) * HEAD_SLOT + hh, col0 + hh * rep:col0 + (hh + 1) * rep] = 1.0
    return jnp.asarray(e, BF16)


def _scan_matrix():
    t = np.arange(CHUNK)
    upper = (t[:, None] <= t[None, :]).astype(np.float32)
    return jnp.asarray(np.concatenate([upper, np.ones((CHUNK, CHUNK), np.float32)], axis=1), BF16)


def _silu(x):
    return x * (1.0 / (1.0 + jnp.exp(-x)))


def _softplus(x):
    return jnp.maximum(x, 0.0) + jnp.log1p(jnp.exp(-jnp.abs(x)))


def _dot(a, b):
    return jnp.dot(a, b, preferred_element_type=F32)


def _dot_nt(a, b):
    return lax.dot_general(a, b, (((1,), (1,)), ((), ())), preferred_element_type=F32)


def _dot_tn(a, b):
    return lax.dot_general(a, b, (((0,), (0,)), ((), ())), preferred_element_type=F32)


def _rmsnorm_mod(x, norm_w, shift, scale):
    ms = jnp.mean(x * x, axis=-1, keepdims=True)
    y = x * lax.rsqrt(ms + EPS) * norm_w
    return y * (1.0 + scale) + shift


def _split_bf16(v, parts):
    out = []
    r = v
    for _ in range(parts):
        p = r.astype(BF16).astype(F32)
        out.append(p)
        r = r - p
    return out


def _mod_kernel(c_ref, w_ref, b_ref, o_ref):
    sc = _silu(c_ref[...]).astype(BF16)
    o_ref[...] = _dot(sc, w_ref[...].astype(BF16)) + b_ref[...]


def _adaln_mod(c, w_ada, b_ada):
    b, d = c.shape
    n = w_ada.shape[1]
    return pl.pallas_call(
        _mod_kernel,
        grid=(n // MOD_TILE,),
        in_specs=[
            pl.BlockSpec((b, d), lambda j: (0, 0)),
            pl.BlockSpec((d, MOD_TILE), lambda j: (0, j)),
            pl.BlockSpec((1, MOD_TILE), lambda j: (0, j)),
        ],
        out_specs=pl.BlockSpec((b, MOD_TILE), lambda j: (0, j)),
        out_shape=jax.ShapeDtypeStruct((b, n), F32),
        compiler_params=pltpu.CompilerParams(
            dimension_semantics=("arbitrary",), vmem_limit_bytes=VMEM_LIMIT_BYTES),
        name="adaln_mod",
    )(c, w_ada, b_ada.reshape(1, n))


def _mixer_kernel(x_ref, pos_ref, mod_ref, n1w_ref, invf_ref, wqkv_ref, wz_ref, wxbc_ref,
                  wdt_ref, convw_ref, convb_ref, dtb_ref, alog_ref, dskip_ref, sinks_ref,
                  snw_ref, wout_ref, scan_ref, expand_ref,
                  o_ref,
                  q_scr, k_scr, v_scr, z_scr, xbc_scr, dt_scr, state_scr, mix_scr):
    ts = x_ref.shape[1]
    n_chunks = ts // CHUNK
    i = pl.program_id(1)
    layout, n_slots, _ = _expand_layout()

    @pl.when(i == 0)
    def _():
        k_scr[0:CHUNK, :] = jnp.zeros((CHUNK, KV_WIDTH), BF16)
        v_scr[0:CHUNK, :] = jnp.zeros((CHUNK, KV_WIDTH), BF16)
        xbc_scr[0:SUBLANES, :] = jnp.zeros((SUBLANES, CONV_CH), F32)
        state_scr[...] = jnp.zeros(state_scr.shape, F32)

    mod = mod_ref[0]
    shift1, scale1, gate1 = mod[0:1], mod[1:2], mod[2:3]
    x = x_ref[0]
    h = _rmsnorm_mod(x, n1w_ref[...], shift1, scale1).astype(BF16)

    qkv = _dot(h, wqkv_ref[...])
    z_scr[...] = _dot(h, wz_ref[...])
    xbc_scr[SUBLANES:SUBLANES + ts, :] = _dot(h, wxbc_ref[...])
    dt_scr[...] = _dot_nt(wdt_ref[...], h)

    ang = pos_ref[0].astype(F32) * invf_ref[...]
    cos = jnp.cos(ang)
    sin = jnp.sin(ang)
    lane = lax.broadcasted_iota(jnp.int32, (ts, LANES), 1)
    first_half = (lane % HEAD_DIM) < (HEAD_DIM // 2)
    sin_signed = jnp.where(first_half, -sin, sin)

    def rope(t):
        rot = jnp.where(first_half, pltpu.roll(t, LANES - HEAD_DIM // 2, axis=1),
                        pltpu.roll(t, HEAD_DIM // 2, axis=1))
        return t * cos + rot * sin_signed

    q_scale = 1.0 / math.sqrt(HEAD_DIM)
    for j in range(ATTN_WIDTH // LANES):
        q_scr[:, j * LANES:(j + 1) * LANES] = (
            rope(qkv[:, j * LANES:(j + 1) * LANES]) * q_scale).astype(BF16)
    k_scr[CHUNK:CHUNK + ts, :] = rope(qkv[:, ATTN_WIDTH:ATTN_WIDTH + KV_WIDTH]).astype(BF16)
    v_scr[CHUNK:CHUNK + ts, :] = qkv[:, ATTN_WIDTH + KV_WIDTH:].astype(BF16)

    a_neg = -jnp.exp(alog_ref[...])
    causal = (lax.broadcasted_iota(jnp.int32, (CHUNK, CHUNK), 0)
              >= lax.broadcasted_iota(jnp.int32, (CHUNK, CHUNK), 1))

    n_ql = Q_PER_KV * CHUNK
    key_idx = lax.broadcasted_iota(jnp.int32, (CHUNK, n_ql), 0)
    qry_idx = lax.broadcasted_iota(jnp.int32, (CHUNK, n_ql), 1) % CHUNK
    upper = key_idx > qry_idx
    lane_head = lax.broadcasted_iota(jnp.int32, (1, n_ql), 1) // CHUNK

    def chunk_body(c, carry):
        r0 = pl.multiple_of(c * CHUNK, CHUNK)
        prev_bias = jnp.where((i == 0) & (c == 0), MASK_VALUE, 0.0).astype(F32)
        kband = k_scr[pl.ds(r0, 2 * CHUNK), :]
        vband = v_scr[pl.ds(r0, 2 * CHUNK), :]
        qblk = q_scr[pl.ds(r0, CHUNK), :]
        def attn_stages(g):
            qg = jnp.concatenate(
                [qblk[:, (g * Q_PER_KV + j) * HEAD_DIM:(g * Q_PER_KV + j + 1) * HEAD_DIM]
                 for j in range(Q_PER_KV)], axis=0)
            s_t = _dot_nt(kband[:, g * HEAD_DIM:(g + 1) * HEAD_DIM], qg)
            yield
            sf = jnp.where(upper, s_t[:CHUNK] + prev_bias, s_t[CHUNK:])
            sink = jnp.zeros((1, n_ql), F32)
            for j in range(Q_PER_KV):
                sink = jnp.where(lane_head == j, sinks_ref[g * Q_PER_KV + j], sink)
            m = jnp.maximum(jnp.max(sf, axis=0, keepdims=True), sink)
            pf = jnp.exp(sf - m)
            denom = jnp.sum(pf, axis=0, keepdims=True) + jnp.exp(sink - m)
            p_t = jnp.concatenate([jnp.where(upper, pf, 0.0), jnp.where(upper, 0.0, pf)],
                                  axis=0).astype(BF16)
            yield
            o_t = _dot_tn(vband[:, g * HEAD_DIM:(g + 1) * HEAD_DIM], p_t)
            o_t = o_t * (1.0 / denom)
            yield
            tiles = []
            for pp in range(Q_PER_KV // 2):
                blk = jnp.concatenate([o_t[:, (2 * pp) * CHUNK:(2 * pp + 1) * CHUNK],
                                       o_t[:, (2 * pp + 1) * CHUNK:(2 * pp + 2) * CHUNK]], axis=0)
                tiles.append(blk.T)
            mix_scr[pl.ds(r0, CHUNK), g * Q_PER_KV * HEAD_DIM:(g + 1) * Q_PER_KV * HEAD_DIM] = (
                jnp.concatenate(tiles, axis=1).astype(BF16))

        def ssd_stages():
            win = xbc_scr[pl.ds(r0, SUBLANES + CHUNK), :]
            conv = convb_ref[...]
            for t in range(CONV_WIDTH):
                lo = SUBLANES - (CONV_WIDTH - 1) + t
                conv = conv + convw_ref[t:t + 1, :] * win[lo:lo + CHUNK]
            xbc = _silu(conv)
            xs = xbc[:, :SSM_WIDTH]
            b_mat = xbc[:, SSM_WIDTH:SSM_WIDTH + SSM_GROUPS * D_STATE]
            c_mat = xbc[:, SSM_WIDTH + SSM_GROUPS * D_STATE:]
            yield

            dt = _softplus(dt_scr[0:SSM_HEADS, pl.ds(r0, CHUNK)] + dtb_ref[...])
            a = dt * a_neg
            pad = jnp.zeros((SUBLANES, CHUNK), F32)
            cs = _dot(jnp.concatenate(_split_bf16(a, 3) + [pad], axis=0).astype(BF16),
                      scan_ref[...])
            cs = cs[0:8] + cs[8:16] + cs[16:24]
            a_cs = cs[:, :CHUNK]
            a_tot = cs[:, CHUNK:]
            ea = jnp.exp(a_cs)
            w_end = dt * jnp.exp(a_tot - a_cs)
            rows = (_split_bf16(a_cs, 3) + _split_bf16(ea, 2) + _split_bf16(w_end, 2)
                    + _split_bf16(dt, 2))
            assert len(rows) == n_slots
            rows.append(jnp.zeros((LANES - n_slots * HEAD_SLOT, CHUNK), F32))
            packed = jnp.concatenate(rows, axis=0).T.astype(BF16)
            ex = _dot(packed, expand_ref[...])

            def ex_block(name):
                _, _, col0, width = layout[name]
                return ex[:, col0:col0 + width]

            acs_b = ex_block("acs")
            ea_e = ex_block("ea")
            xdt = (xs * ex_block("dt")).astype(BF16)
            xw = (xs * ex_block("w")).astype(BF16)
            yield
            ys = []
            for g in range(SSM_GROUPS):
                gs = slice(g * GROUP_WIDTH, (g + 1) * GROUP_WIDTH)
                bg = b_mat[:, g * D_STATE:(g + 1) * D_STATE].astype(BF16)
                cg = c_mat[:, g * D_STATE:(g + 1) * D_STATE].astype(BF16)
                cb = _dot_nt(cg, bg)
                s_old = state_scr[g]
                y_off = _dot(cg, s_old.astype(BF16)) * ea_e[:, gs]
                yd = []
                for j in range(HEADS_PER_GROUP):
                    hh = g * HEADS_PER_GROUP + j
                    seg = acs_b[:, hh * LANES:(hh + 1) * LANES] - a_cs[hh:hh + 1, :]
                    dec = jnp.exp(jnp.where(causal, seg, MASK_VALUE))
                    yd.append(_dot((cb * dec).astype(BF16),
                                   xdt[:, hh * SSM_HEAD_DIM:(hh + 1) * SSM_HEAD_DIM]))
                yield
                y_g = jnp.concatenate(yd, axis=1) + y_off + xs[:, gs] * dskip_ref[:, gs]
                state_scr[g] = s_old * ea_e[CHUNK - 1:CHUNK, gs] + _dot_tn(bg, xw[:, gs])
                y_g = y_g * _silu(z_scr[pl.ds(r0, CHUNK), gs])
                y_g = y_g * lax.rsqrt(jnp.mean(y_g * y_g, axis=-1, keepdims=True) + EPS)
                ys.append(y_g * snw_ref[:, gs])
                yield
            mix_scr[pl.ds(r0, CHUNK), ATTN_WIDTH:] = jnp.concatenate(ys, axis=1).astype(BF16)

        streams = {"a0": attn_stages(0), "a1": attn_stages(1), "s": ssd_stages()}
        for name in STAGE_ORDER:
            next(streams[name], None)
        for stream in streams.values():
            assert next(stream, "done") == "done"
        return carry

    lax.fori_loop(0, n_chunks, chunk_body, 0)

    k_scr[0:CHUNK, :] = k_scr[ts:ts + CHUNK, :]
    v_scr[0:CHUNK, :] = v_scr[ts:ts + CHUNK, :]
    xbc_scr[0:SUBLANES, :] = xbc_scr[ts:ts + SUBLANES, :]

    o_ref[0] = x + gate1 * _dot(mix_scr[...], wout_ref[...])


def _const_spec(shape):
    nd = len(shape)
    return pl.BlockSpec(shape, lambda b, i: (0,) * nd, pipeline_mode=pl.Buffered(1))


def _mixer(x, pos3, mod3, n1w, invf, wqkv, wz, wxbc, wdt, convw, convb, dtb, alog, dskip, sinks,
           snw, wout, scan, expand):
    b, s, d = x.shape
    ts = MIXER_TILE
    in_specs = [
        pl.BlockSpec((1, ts, d), lambda bb, i: (bb, i, 0)),
        pl.BlockSpec((1, ts, 1), lambda bb, i: (bb, i, 0)),
        pl.BlockSpec((1, N_MOD, d), lambda bb, i: (bb, 0, 0)),
        _const_spec(n1w.shape), _const_spec(invf.shape), _const_spec(wqkv.shape),
        _const_spec(wz.shape), _const_spec(wxbc.shape), _const_spec(wdt.shape),
        _const_spec(convw.shape), _const_spec(convb.shape), _const_spec(dtb.shape),
        _const_spec(alog.shape), _const_spec(dskip.shape),
        pl.BlockSpec(memory_space=pltpu.SMEM),
        _const_spec(snw.shape), _const_spec(wout.shape), _const_spec(scan.shape),
        _const_spec(expand.shape),
    ]
    scratch = [
        pltpu.VMEM((ts, ATTN_WIDTH), BF16),
        pltpu.VMEM((CHUNK + ts, KV_WIDTH), BF16),
        pltpu.VMEM((CHUNK + ts, KV_WIDTH), BF16),
        pltpu.VMEM((ts, SSM_WIDTH), F32),
        pltpu.VMEM((SUBLANES + ts, CONV_CH), F32),
        pltpu.VMEM((2 * SUBLANES, ts), F32),
        pltpu.VMEM((SSM_GROUPS, D_STATE, GROUP_WIDTH), F32),
        pltpu.VMEM((ts, MIX_WIDTH), BF16),
    ]
    return pl.pallas_call(
        _mixer_kernel,
        grid=(b, s // ts),
        in_specs=in_specs,
        out_specs=pl.BlockSpec((1, ts, d), lambda bb, i: (bb, i, 0)),
        out_shape=jax.ShapeDtypeStruct((b, s, d), F32),
        scratch_shapes=scratch,
        compiler_params=pltpu.CompilerParams(
            dimension_semantics=("arbitrary", "arbitrary"), vmem_limit_bytes=VMEM_LIMIT_BYTES),
        name="mixer",
    )(x, pos3, mod3, n1w, invf, wqkv, wz, wxbc, wdt, convw, convb, dtb, alog, dskip, sinks, snw,
      wout, scan, expand)


def _ffn_kernel(x_ref, mod_ref, n2w_ref, wg_ref, wu_ref, wd_ref, nfw_ref, o_ref):
    mod = mod_ref[0]
    shift2, scale2, gate2 = mod[3:4], mod[4:5], mod[5:6]
    x = x_ref[0]
    h = _rmsnorm_mod(x, n2w_ref[...], shift2, scale2).astype(BF16)
    g = _dot(h, wg_ref[...])
    u = _dot(h, wu_ref[...])
    act = (_silu(g) * u).astype(BF16)
    x2 = x + gate2 * _dot(act, wd_ref[...])
    ms = jnp.mean(x2 * x2, axis=-1, keepdims=True)
    o_ref[0] = x2 * lax.rsqrt(ms + EPS) * nfw_ref[...]


def _ffn(x1, mod3, n2w, wg, wu, wd, nfw):
    b, s, d = x1.shape
    tm = FFN_TILE
    return pl.pallas_call(
        _ffn_kernel,
        grid=(b, s // tm),
        in_specs=[
            pl.BlockSpec((1, tm, d), lambda bb, i: (bb, i, 0)),
            pl.BlockSpec((1, N_MOD, d), lambda bb, i: (bb, 0, 0)),
            _const_spec(n2w.shape), _const_spec(wg.shape), _const_spec(wu.shape),
            _const_spec(wd.shape), _const_spec(nfw.shape),
        ],
        out_specs=pl.BlockSpec((1, tm, d), lambda bb, i: (bb, i, 0)),
        out_shape=jax.ShapeDtypeStruct((b, s, d), F32),
        compiler_params=pltpu.CompilerParams(
            dimension_semantics=("arbitrary", "arbitrary"), vmem_limit_bytes=VMEM_LIMIT_BYTES),
        name="ffn",
    )(x1, mod3, n2w, wg, wu, wd, nfw)


def _head_rows(v):
    return jnp.broadcast_to(v.astype(F32)[:, None], (SSM_HEADS, LANES))


def kernel(x, c, positions, w_ada, b_ada, norm1_w, w_in, conv_w, conv_b, dt_bias, a_log, d_skip,
           attn_sinks, ssm_norm_w, w_out, norm2_w, w_gate_up, w_down, final_norm_w):
    b, s, d = x.shape
    depth = w_ada.shape[0]
    half = HEAD_DIM // 2
    inv_freq = ROPE_THETA ** (-jnp.arange(half, dtype=F32) / half)
    invf = jnp.tile(inv_freq, LANES // half).reshape(1, LANES)
    scan = _scan_matrix()
    expand = _expand_matrix()
    pos3 = positions.reshape(b, s, 1)
    o1 = ATTN_WIDTH + 2 * KV_WIDTH
    o2 = o1 + SSM_WIDTH
    o3 = o2 + CONV_CH
    assert depth == 1, "the final norm is fused into the (single) layer's ffn call"
    layer = 0
    mod3 = _adaln_mod(c, w_ada[layer], b_ada[layer]).reshape(b, N_MOD, d)
    w_in_l = w_in[layer].astype(BF16)
    wdt = jnp.pad(w_in_l[:, o3:].T, ((0, 2 * SUBLANES - SSM_HEADS), (0, 0)))
    x1 = _mixer(
        x, pos3, mod3, norm1_w[layer].reshape(1, d), invf,
        w_in_l[:, :o1], w_in_l[:, o1:o2], w_in_l[:, o2:o3], wdt,
        conv_w[layer], conv_b[layer].reshape(1, CONV_CH),
        _head_rows(dt_bias[layer]), _head_rows(a_log[layer]),
        jnp.repeat(d_skip[layer].astype(F32), SSM_HEAD_DIM).reshape(1, SSM_WIDTH),
        attn_sinks[layer].astype(F32),
        ssm_norm_w[layer].reshape(1, SSM_WIDTH), w_out[layer].astype(BF16), scan, expand)
    wgu = w_gate_up[layer].astype(BF16)
    return _ffn(x1, mod3, norm2_w[layer].reshape(1, d), wgu[:, :D_FF], wgu[:, D_FF:],
                w_down[layer].astype(BF16), final_norm_w.reshape(1, d))
```

```python
import math

import jax
import jax.numpy as jnp
import numpy as np
from jax import lax
from jax.experimental import pallas as pl
from jax.experimental.pallas import tpu as pltpu

D_MODEL = 1024
HEAD_DIM = 64
N_Q_HEADS = 8
N_KV_HEADS = 2
Q_PER_KV = N_Q_HEADS // N_KV_HEADS
ATTN_WIDTH = N_Q_HEADS * HEAD_DIM
KV_WIDTH = N_KV_HEADS * HEAD_DIM
WINDOW = 128
ROPE_THETA = 10000.0
SSM_HEADS = 8
SSM_HEAD_DIM = 64
SSM_WIDTH = SSM_HEADS * SSM_HEAD_DIM
SSM_GROUPS = 2
HEADS_PER_GROUP = SSM_HEADS // SSM_GROUPS
GROUP_WIDTH = HEADS_PER_GROUP * SSM_HEAD_DIM
D_STATE = 128
CONV_WIDTH = 4
CHUNK = 128
CONV_CH = SSM_WIDTH + 2 * SSM_GROUPS * D_STATE
MIX_WIDTH = ATTN_WIDTH + SSM_WIDTH
D_FF = 2816
N_MOD = 6
EPS = 1e-6

LANES = 128
SUBLANES = 8
HEAD_SLOT = 8
ROPE_PACK = LANES // (HEAD_DIM // 2)
MASK_VALUE = -1e30
VMEM_LIMIT_BYTES = 56 * 1024 * 1024

MIXER_TILE = 512
FFN_TILE = 512
MOD_TILE = 1024

F32 = jnp.float32
BF16 = jnp.bfloat16

STAGE_ORDER = (
    "A.a0", "A.a1", "B.a0", "B.a1", "A.s", "B.s", "A.s", "B.s",
    "A.a0", "A.a1", "A.s",
    "B.a0", "A.s",
    "B.a1", "B.s",
    "A.a0", "A.a1", "A.s",
    "B.a0", "A.s",
    "B.a1", "B.s",
    "A.a0", "A.a1", "B.s",
    "B.a0", "B.s",
    "B.a1", "A.s", "B.s",
)

_EXPAND_ITEMS = (("acs", 3, LANES), ("ea", 2, SSM_HEAD_DIM), ("w", 2, SSM_HEAD_DIM),
                 ("dt", 2, SSM_HEAD_DIM))


def _expand_layout():
    slot = 0
    col = 0
    layout = {}
    for name, parts, rep in _EXPAND_ITEMS:
        layout[name] = (slot, parts, col, rep * SSM_HEADS)
        slot += parts
        col += rep * SSM_HEADS
    return layout, slot, col


def _expand_matrix():
    layout, nslots, ncols = _expand_layout()
    assert nslots * HEAD_SLOT <= LANES
    e = np.zeros((LANES, ncols), np.float32)
    for name, parts, rep in _EXPAND_ITEMS:
        slot0, _, col0, _ = layout[name]
        for p in range(parts):
            for hh in range(SSM_HEADS):
                e[(slot0 + p) * HEAD_SLOT + hh, col0 + hh * rep:col0 + (hh + 1) * rep] = 1.0
    return jnp.asarray(e, BF16)


def _scan_matrix():
    t = np.arange(CHUNK)
    upper = (t[:, None] <= t[None, :]).astype(np.float32)
    return jnp.asarray(np.concatenate([upper, np.ones((CHUNK, CHUNK), np.float32)], axis=1), BF16)


def _silu(x):
    return x * (1.0 / (1.0 + jnp.exp(-x)))


def _softplus(x):
    return jnp.maximum(x, 0.0) + jnp.log1p(jnp.exp(-jnp.abs(x)))


def _dot(a, b):
    return jnp.dot(a, b, preferred_element_type=F32)


def _dot_nt(a, b):
    return lax.dot_general(a, b, (((1,), (1,)), ((), ())), preferred_element_type=F32)


def _dot_tn(a, b):
    return lax.dot_general(a, b, (((0,), (0,)), ((), ())), preferred_element_type=F32)


def _rmsnorm_mod(x, norm_w, shift, scale):
    ms = jnp.mean(x * x, axis=-1, keepdims=True)
    return x * lax.rsqrt(ms + EPS) * (norm_w * (1.0 + scale)) + shift


def _split_bf16(v, parts):
    out = []
    r = v
    for _ in range(parts):
        p = r.astype(BF16).astype(F32)
        out.append(p)
        r = r - p
    return out


def _mod_kernel(c_ref, w_ref, b_ref, o_ref):
    sc = _silu(c_ref[...]).astype(BF16)
    o_ref[...] = _dot(sc, w_ref[...].astype(BF16)) + b_ref[...]


def _adaln_mod(c, w_ada, b_ada):
    b, d = c.shape
    n = w_ada.shape[1]
    return pl.pallas_call(
        _mod_kernel,
        grid=(n // MOD_TILE,),
        in_specs=[
            pl.BlockSpec((b, d), lambda j: (0, 0)),
            pl.BlockSpec((d, MOD_TILE), lambda j: (0, j)),
            pl.BlockSpec((1, MOD_TILE), lambda j: (0, j)),
        ],
        out_specs=pl.BlockSpec((b, MOD_TILE), lambda j: (0, j)),
        out_shape=jax.ShapeDtypeStruct((b, n), F32),
        compiler_params=pltpu.CompilerParams(
            dimension_semantics=("arbitrary",), vmem_limit_bytes=VMEM_LIMIT_BYTES),
        name="adaln_mod",
    )(c, w_ada, b_ada.reshape(1, n))


def _mixer_kernel(x_ref, pos_ref, mod_ref, n1w_ref, invf_ref, wqkv_ref, wz_ref, wxbc_ref,
                  wdt_ref, convw_ref, convb_ref, dtb_ref, alog_ref, dskip_ref, sinks_ref,
                  snw_ref, wout_ref, scan_ref, expand_ref,
                  o_ref,
                  q_scr, k_scr, v_scr, z_scr, xbc_scr, dt_scr, state_scr, mix_scr):
    ts = x_ref.shape[1]
    n_chunks = ts // CHUNK
    i = pl.program_id(1)
    layout, n_slots, _ = _expand_layout()

    @pl.when(i == 0)
    def _():
        k_scr[0:CHUNK, :] = jnp.zeros((CHUNK, KV_WIDTH), BF16)
        v_scr[0:CHUNK, :] = jnp.zeros((CHUNK, KV_WIDTH), BF16)
        xbc_scr[0:SUBLANES, :] = jnp.zeros((SUBLANES, CONV_CH), F32)
        state_scr[...] = jnp.zeros(state_scr.shape, F32)

    mod = mod_ref[0]
    shift1, scale1, gate1 = mod[0:1], mod[1:2], mod[2:3]
    x = x_ref[0]
    h = _rmsnorm_mod(x, n1w_ref[...], shift1, scale1).astype(BF16)

    def conv_silu(row0, cols):
        win = xbc_scr[pl.ds(row0, SUBLANES + CHUNK), cols]
        conv = convb_ref[:, cols]
        for t in range(CONV_WIDTH):
            lo = SUBLANES - (CONV_WIDTH - 1) + t
            conv = conv + convw_ref[t:t + 1, cols] * win[lo:lo + CHUNK]
        return _silu(conv)

    x_cols = slice(0, SSM_WIDTH)
    bc_cols = slice(SSM_WIDTH, CONV_CH)
    xbc_scr[SUBLANES:SUBLANES + ts, x_cols] = _dot(h, wxbc_ref[:, x_cols])

    half = HEAD_DIM // 2
    ang_d = pos_ref[0, 0].astype(F32) * invf_ref[...]
    lane_d = lax.broadcasted_iota(jnp.int32, ang_d.shape, 1)

    def spread(tbl):
        blocks = []
        for k in range(LANES // half):
            t = pltpu.roll(tbl, LANES - half * k, axis=1) if k else tbl
            t = jnp.where(lane_d < half, t, pltpu.roll(t, half, axis=1))
            t = jnp.where(lane_d < 2 * half, t, pltpu.roll(t, 2 * half, axis=1))
            blocks.append(t)
        return jnp.concatenate(blocks, axis=0)

    cos = spread(jnp.cos(ang_d))
    sin = spread(jnp.sin(ang_d))
    lane = lax.broadcasted_iota(jnp.int32, (ts, LANES), 1)
    first_half = (lane % HEAD_DIM) < (HEAD_DIM // 2)
    sin_signed = jnp.where(first_half, -sin, sin)

    def rope(t):
        rot = jnp.where(first_half, pltpu.roll(t, LANES - HEAD_DIM // 2, axis=1),
                        pltpu.roll(t, HEAD_DIM // 2, axis=1))
        return t * cos + rot * sin_signed

    xbc_scr[SUBLANES:SUBLANES + ts, bc_cols] = _dot(h, wxbc_ref[:, bc_cols])
    q = _dot(h, wqkv_ref[:, :ATTN_WIDTH])
    kv = _dot(h, wqkv_ref[:, ATTN_WIDTH:])
    q_scale = 1.0 / math.sqrt(HEAD_DIM)
    for j in range(ATTN_WIDTH // LANES):
        q_scr[:, j * LANES:(j + 1) * LANES] = (
            rope(q[:, j * LANES:(j + 1) * LANES]) * q_scale).astype(BF16)

    z = _dot(h, wz_ref[...])
    k_scr[CHUNK:CHUNK + ts, :] = rope(kv[:, :KV_WIDTH]).astype(BF16)
    v_scr[CHUNK:CHUNK + ts, :] = kv[:, KV_WIDTH:].astype(BF16)

    dt_scr[...] = _dot_nt(wdt_ref[...], h)
    z_scr[...] = _silu(z)

    a_neg = -jnp.exp(alog_ref[...])
    causal = (lax.broadcasted_iota(jnp.int32, (CHUNK, CHUNK), 0)
              >= lax.broadcasted_iota(jnp.int32, (CHUNK, CHUNK), 1))

    n_ql = Q_PER_KV * CHUNK
    key_idx = lax.broadcasted_iota(jnp.int32, (CHUNK, n_ql), 0)
    qry_idx = lax.broadcasted_iota(jnp.int32, (CHUNK, n_ql), 1) % CHUNK
    upper = key_idx > qry_idx
    lane_head = lax.broadcasted_iota(jnp.int32, (1, n_ql), 1) // CHUNK

    def chunk_streams(c):
        r0 = pl.multiple_of(c * CHUNK, CHUNK)
        prev_bias = jnp.where((i == 0) & (c == 0), MASK_VALUE, 0.0).astype(F32)
        kband = k_scr[pl.ds(r0, 2 * CHUNK), :]
        vband = v_scr[pl.ds(r0, 2 * CHUNK), :]
        qblk = q_scr[pl.ds(r0, CHUNK), :]
        def attn_stages(g):
            qg = jnp.concatenate(
                [qblk[:, (g * Q_PER_KV + j) * HEAD_DIM:(g * Q_PER_KV + j + 1) * HEAD_DIM]
                 for j in range(Q_PER_KV)], axis=0)
            s_t = _dot_nt(kband[:, g * HEAD_DIM:(g + 1) * HEAD_DIM], qg)
            yield
            sf = jnp.where(upper, s_t[:CHUNK] + prev_bias, s_t[CHUNK:])
            sink = jnp.zeros((1, n_ql), F32)
            for j in range(Q_PER_KV):
                sink = jnp.where(lane_head == j, sinks_ref[g * Q_PER_KV + j], sink)
            m = jnp.maximum(jnp.max(sf, axis=0, keepdims=True), sink)
            pf = jnp.exp(sf - m)
            denom = jnp.sum(pf, axis=0, keepdims=True) + jnp.exp(sink - m)
            p_t = jnp.concatenate([jnp.where(upper, pf, 0.0), jnp.where(upper, 0.0, pf)],
                                  axis=0).astype(BF16)
            yield
            o_t = _dot_tn(vband[:, g * HEAD_DIM:(g + 1) * HEAD_DIM], p_t)
            o_t = o_t * (1.0 / denom)
            yield
            tiles = []
            for pp in range(Q_PER_KV // 2):
                blk = jnp.concatenate([o_t[:, (2 * pp) * CHUNK:(2 * pp + 1) * CHUNK],
                                       o_t[:, (2 * pp + 1) * CHUNK:(2 * pp + 2) * CHUNK]], axis=0)
                tiles.append(blk.T)
            mix_scr[pl.ds(r0, CHUNK), g * Q_PER_KV * HEAD_DIM:(g + 1) * Q_PER_KV * HEAD_DIM] = (
                jnp.concatenate(tiles, axis=1).astype(BF16))

        def ssd_stages():
            xs = conv_silu(r0, x_cols)
            bc = conv_silu(r0, bc_cols).astype(BF16)
            yield
            dt = _softplus(dt_scr[0:SSM_HEADS, pl.ds(r0, CHUNK)] + dtb_ref[...])
            a = dt * a_neg
            pad = jnp.zeros((SUBLANES, CHUNK), F32)
            cs = _dot(jnp.concatenate(_split_bf16(a, 3) + [pad], axis=0).astype(BF16),
                      scan_ref[...])
            cs = cs[0:8] + cs[8:16] + cs[16:24]
            a_cs = cs[:, :CHUNK]
            a_tot = cs[:, CHUNK:]
            ea = jnp.exp(a_cs)
            w_end = dt * jnp.exp(a_tot - a_cs)
            rows = (_split_bf16(a_cs, 3) + _split_bf16(ea, 2) + _split_bf16(w_end, 2)
                    + _split_bf16(dt, 2))
            assert len(rows) == n_slots
            rows.append(jnp.zeros((LANES - n_slots * HEAD_SLOT, CHUNK), F32))
            packed = jnp.concatenate(rows, axis=0).T.astype(BF16)
            ex = _dot(packed, expand_ref[...])

            def ex_block(name):
                _, _, col0, width = layout[name]
                return ex[:, col0:col0 + width]

            acs_b = ex_block("acs")
            ea_e = ex_block("ea")
            xdt = (xs * ex_block("dt")).astype(BF16)
            xw = (xs * ex_block("w")).astype(BF16)
            yield
            ys = []
            for g in range(SSM_GROUPS):
                gs = slice(g * GROUP_WIDTH, (g + 1) * GROUP_WIDTH)
                bg = bc[:, g * D_STATE:(g + 1) * D_STATE]
                cg = bc[:, (SSM_GROUPS + g) * D_STATE:(SSM_GROUPS + g + 1) * D_STATE]
                cb = _dot_nt(cg, bg)
                s_old = state_scr[g]
                y_off = _dot(cg, s_old.astype(BF16)) * ea_e[:, gs]
                yd = []
                for j in range(HEADS_PER_GROUP):
                    hh = g * HEADS_PER_GROUP + j
                    seg = acs_b[:, hh * LANES:(hh + 1) * LANES] - a_cs[hh:hh + 1, :]
                    dec = jnp.exp(jnp.where(causal, seg, MASK_VALUE))
                    yd.append(_dot((cb * dec).astype(BF16),
                                   xdt[:, hh * SSM_HEAD_DIM:(hh + 1) * SSM_HEAD_DIM]))
                yield
                y_g = jnp.concatenate(yd, axis=1) + y_off + xs[:, gs] * dskip_ref[:, gs]
                state_scr[g] = s_old * ea_e[CHUNK - 1:CHUNK, gs] + _dot_tn(bg, xw[:, gs])
                y_g = y_g * z_scr[pl.ds(r0, CHUNK), gs]
                y_g = y_g * lax.rsqrt(jnp.mean(y_g * y_g, axis=-1, keepdims=True) + EPS)
                ys.append(y_g * snw_ref[:, gs])
                yield
            mix_scr[pl.ds(r0, CHUNK), ATTN_WIDTH:] = jnp.concatenate(ys, axis=1).astype(BF16)

        return {"a0": attn_stages(0), "a1": attn_stages(1), "s": ssd_stages()}

    def pair_body(cp, carry):
        streams = {}
        for tag, sub in (("A", 0), ("B", 1)):
            for name, gen in chunk_streams(2 * cp + sub).items():
                streams[tag + "." + name] = gen
        for name in STAGE_ORDER:
            next(streams[name], None)
        for stream in streams.values():
            assert next(stream, "done") == "done"
        return carry

    lax.fori_loop(0, n_chunks // 2, pair_body, 0)

    k_scr[0:CHUNK, :] = k_scr[ts:ts + CHUNK, :]
    v_scr[0:CHUNK, :] = v_scr[ts:ts + CHUNK, :]
    xbc_scr[0:SUBLANES, :] = xbc_scr[ts:ts + SUBLANES, :]

    o_ref[0] = x + gate1 * _dot(mix_scr[...], wout_ref[...])


def _const_spec(shape):
    nd = len(shape)
    return pl.BlockSpec(shape, lambda b, i: (0,) * nd, pipeline_mode=pl.Buffered(1))


def _mixer(x, pos3, mod3, n1w, invf, wqkv, wz, wxbc, wdt, convw, convb, dtb, alog, dskip, sinks,
           snw, wout, scan, expand):
    b, s, d = x.shape
    ts = MIXER_TILE
    in_specs = [
        pl.BlockSpec((1, ts, d), lambda bb, i: (bb, i, 0)),
        pl.BlockSpec((1, 1, ts // ROPE_PACK, LANES), lambda bb, i: (bb, i, 0, 0)),
        pl.BlockSpec((1, N_MOD, d), lambda bb, i: (bb, 0, 0)),
        _const_spec(n1w.shape), _const_spec(invf.shape), _const_spec(wqkv.shape),
        _const_spec(wz.shape), _const_spec(wxbc.shape), _const_spec(wdt.shape),
        _const_spec(convw.shape), _const_spec(convb.shape), _const_spec(dtb.shape),
        _const_spec(alog.shape), _const_spec(dskip.shape),
        pl.BlockSpec(memory_space=pltpu.SMEM),
        _const_spec(snw.shape), _const_spec(wout.shape), _const_spec(scan.shape),
        _const_spec(expand.shape),
    ]
    scratch = [
        pltpu.VMEM((ts, ATTN_WIDTH), BF16),
        pltpu.VMEM((CHUNK + ts, KV_WIDTH), BF16),
        pltpu.VMEM((CHUNK + ts, KV_WIDTH), BF16),
        pltpu.VMEM((ts, SSM_WIDTH), F32),
        pltpu.VMEM((SUBLANES + ts, CONV_CH), F32),
        pltpu.VMEM((2 * SUBLANES, ts), F32),
        pltpu.VMEM((SSM_GROUPS, D_STATE, GROUP_WIDTH), F32),
        pltpu.VMEM((ts, MIX_WIDTH), BF16),
    ]
    return pl.pallas_call(
        _mixer_kernel,
        grid=(b, s // ts),
        in_specs=in_specs,
        out_specs=pl.BlockSpec((1, ts, d), lambda bb, i: (bb, i, 0)),
        out_shape=jax.ShapeDtypeStruct((b, s, d), F32),
        scratch_shapes=scratch,
        compiler_params=pltpu.CompilerParams(
            dimension_semantics=("arbitrary", "arbitrary"), vmem_limit_bytes=VMEM_LIMIT_BYTES),
        name="mixer",
    )(x, pos3, mod3, n1w, invf, wqkv, wz, wxbc, wdt, convw, convb, dtb, alog, dskip, sinks, snw,
      wout, scan, expand)


def _ffn_kernel(x_ref, mod_ref, n2w_ref, wg_ref, wu_ref, wd_ref, nfw_ref, o_ref):
    mod = mod_ref[0]
    shift2, scale2, gate2 = mod[3:4], mod[4:5], mod[5:6]
    x = x_ref[0]
    h = _rmsnorm_mod(x, n2w_ref[...], shift2, scale2).astype(BF16)
    g = _dot(h, wg_ref[...])
    u = _dot(h, wu_ref[...])
    act = (_silu(g) * u).astype(BF16)
    x2 = x + gate2 * _dot(act, wd_ref[...])
    ms = jnp.mean(x2 * x2, axis=-1, keepdims=True)
    o_ref[0] = x2 * lax.rsqrt(ms + EPS) * nfw_ref[...]


def _ffn(x1, mod3, n2w, wg, wu, wd, nfw):
    b, s, d = x1.shape
    tm = FFN_TILE
    return pl.pallas_call(
        _ffn_kernel,
        grid=(b, s // tm),
        in_specs=[
            pl.BlockSpec((1, tm, d), lambda bb, i: (bb, i, 0)),
            pl.BlockSpec((1, N_MOD, d), lambda bb, i: (bb, 0, 0)),
            _const_spec(n2w.shape), _const_spec(wg.shape), _const_spec(wu.shape),
            _const_spec(wd.shape), _const_spec(nfw.shape),
        ],
        out_specs=pl.BlockSpec((1, tm, d), lambda bb, i: (bb, i, 0)),
        out_shape=jax.ShapeDtypeStruct((b, s, d), F32),
        compiler_params=pltpu.CompilerParams(
            dimension_semantics=("arbitrary", "arbitrary"), vmem_limit_bytes=VMEM_LIMIT_BYTES),
        name="ffn",
    )(x1, mod3, n2w, wg, wu, wd, nfw)


def _head_rows(v):
    return jnp.broadcast_to(v.astype(F32)[:, None], (SSM_HEADS, LANES))


def kernel(x, c, positions, w_ada, b_ada, norm1_w, w_in, conv_w, conv_b, dt_bias, a_log, d_skip,
           attn_sinks, ssm_norm_w, w_out, norm2_w, w_gate_up, w_down, final_norm_w):
    b, s, d = x.shape
    depth = w_ada.shape[0]
    half = HEAD_DIM // 2
    inv_freq = ROPE_THETA ** (-jnp.arange(half, dtype=F32) / half)
    invf = jnp.tile(inv_freq, LANES // half).reshape(1, LANES)
    scan = _scan_matrix()
    expand = _expand_matrix()
    ts = MIXER_TILE
    pos3 = positions.reshape(b, s // ts, ROPE_PACK, ts // ROPE_PACK).transpose(0, 1, 3, 2)
    pos3 = jnp.repeat(pos3, half, axis=-1)
    o1 = ATTN_WIDTH + 2 * KV_WIDTH
    o2 = o1 + SSM_WIDTH
    o3 = o2 + CONV_CH
    assert depth == 1, "the final norm is fused into the (single) layer's ffn call"
    layer = 0
    mod3 = _adaln_mod(c, w_ada[layer], b_ada[layer]).reshape(b, N_MOD, d)
    w_in_l = w_in[layer].astype(BF16)
    wdt = jnp.pad(w_in_l[:, o3:].T, ((0, 2 * SUBLANES - SSM_HEADS), (0, 0)))
    x1 = _mixer(
        x, pos3, mod3, norm1_w[layer].reshape(1, d), invf,
        w_in_l[:, :o1], w_in_l[:, o1:o2], w_in_l[:, o2:o3], wdt,
        conv_w[layer], conv_b[layer].reshape(1, CONV_CH),
        _head_rows(dt_bias[layer]), _head_rows(a_log[layer]),
        jnp.repeat(d_skip[layer].astype(F32), SSM_HEAD_DIM).reshape(1, SSM_WIDTH),
        attn_sinks[layer].astype(F32),
        ssm_norm_w[layer].reshape(1, SSM_WIDTH), w_out[layer].astype(BF16), scan, expand)
    wgu = w_gate_up[layer].astype(BF16)
    return _ffn(x1, mod3, norm2_w[layer].reshape(1, d), wgu[:, :D_FF], wgu[:, D_FF:],
                w_down[layer].astype(BF16), final_norm_w.reshape(1, d))
```

```python
import math

import jax
import jax.numpy as jnp
import numpy as np
from jax import lax
from jax.experimental import pallas as pl
from jax.experimental.pallas import tpu as pltpu

D_MODEL = 1024
HEAD_DIM = 64
N_Q_HEADS = 8
N_KV_HEADS = 2
Q_PER_KV = N_Q_HEADS // N_KV_HEADS
ATTN_WIDTH = N_Q_HEADS * HEAD_DIM
KV_WIDTH = N_KV_HEADS * HEAD_DIM
WINDOW = 128
ROPE_THETA = 10000.0
SSM_HEADS = 8
SSM_HEAD_DIM = 64
SSM_WIDTH = SSM_HEADS * SSM_HEAD_DIM
SSM_GROUPS = 2
HEADS_PER_GROUP = SSM_HEADS // SSM_GROUPS
GROUP_WIDTH = HEADS_PER_GROUP * SSM_HEAD_DIM
D_STATE = 128
CONV_WIDTH = 4
CHUNK = 128
CONV_CH = SSM_WIDTH + 2 * SSM_GROUPS * D_STATE
MIX_WIDTH = ATTN_WIDTH + SSM_WIDTH
D_FF = 2816
N_MOD = 6
EPS = 1e-6

LANES = 128
SUBLANES = 8
MXU_WIDTH = 256
HEAD_SLOT = 8
ROPE_PACK = LANES // (HEAD_DIM // 2)
MASK_VALUE = -1e30
VMEM_LIMIT_BYTES = 56 * 1024 * 1024

MIXER_TILE = 512
FFN_TILE = 512
MOD_TILE = 1024

F32 = jnp.float32
BF16 = jnp.bfloat16

IN_XBC = 0
IN_Q = CONV_CH // MXU_WIDTH
IN_KV = IN_Q + ATTN_WIDTH // MXU_WIDTH
IN_Z = IN_KV + 2 * KV_WIDTH // MXU_WIDTH
N_IN_PIECES = IN_Z + SSM_WIDTH // MXU_WIDTH
N_OUT_PIECES = D_MODEL // MXU_WIDTH
CHUNKS_PER_ITER = 2
ITERS_PER_TILE = MIXER_TILE // (CHUNKS_PER_ITER * CHUNK)
N_AHEAD_PIECES = IN_Q
IN_PER_ITER = N_AHEAD_PIECES // ITERS_PER_TILE
OUT_PER_ITER = N_OUT_PIECES // ITERS_PER_TILE
assert IN_PER_ITER * ITERS_PER_TILE == N_AHEAD_PIECES and OUT_PER_ITER * ITERS_PER_TILE == N_OUT_PIECES

STAGE_ORDER = (
    "A.a0", "A.a1", "B.a0", "B.a1", "A.s", "B.s", "A.s", "B.s",
    "A.a0", "A.a1", "A.s",
    "B.a0", "A.s",
    "B.a1", "B.s",
    "A.a0", "A.a1", "A.s",
    "B.a0", "A.s",
    "B.a1", "B.s",
    "A.a0", "A.a1", "B.s",
    "B.a0", "B.s",
    "B.a1", "A.s", "B.s",
)

_EXPAND_ITEMS = (("acs", 3, LANES), ("ea", 2, SSM_HEAD_DIM), ("w", 2, SSM_HEAD_DIM),
                 ("dt", 2, SSM_HEAD_DIM))


def _expand_layout():
    slot = 0
    col = 0
    layout = {}
    for name, parts, rep in _EXPAND_ITEMS:
        layout[name] = (slot, parts, col, rep * SSM_HEADS)
        slot += parts
        col += rep * SSM_HEADS
    return layout, slot, col


def _expand_matrix():
    layout, nslots, ncols = _expand_layout()
    assert nslots * HEAD_SLOT <= LANES
    e = np.zeros((LANES, ncols), np.float32)
    for name, parts, rep in _EXPAND_ITEMS:
        slot0, _, col0, _ = layout[name]
        for p in range(parts):
            for hh in range(SSM_HEADS):
                e[(slot0 + p) * HEAD_SLOT + hh, col0 + hh * rep:col0 + (hh + 1) * rep] = 1.0
    return jnp.asarray(e, BF16)


def _scan_matrix():
    t = np.arange(CHUNK)
    upper = (t[:, None] <= t[None, :]).astype(np.float32)
    return jnp.asarray(np.concatenate([upper, np.ones((CHUNK, CHUNK), np.float32)], axis=1), BF16)


def _silu(x):
    return x * (1.0 / (1.0 + jnp.exp(-x)))


def _softplus(x):
    return jnp.maximum(x, 0.0) + jnp.log1p(jnp.exp(-jnp.abs(x)))


def _dot(a, b):
    return jnp.dot(a, b, preferred_element_type=F32)


def _dot_nt(a, b):
    return lax.dot_general(a, b, (((1,), (1,)), ((), ())), preferred_element_type=F32)


def _dot_tn(a, b):
    return lax.dot_general(a, b, (((0,), (0,)), ((), ())), preferred_element_type=F32)


def _rmsnorm_mod(x, norm_w, shift, scale):
    ms = jnp.mean(x * x, axis=-1, keepdims=True)
    return x * lax.rsqrt(ms + EPS) * (norm_w * (1.0 + scale)) + shift


def _split_bf16(v, parts):
    out = []
    r = v
    for _ in range(parts):
        p = r.astype(BF16).astype(F32)
        out.append(p)
        r = r - p
    return out


def _mod_kernel(c_ref, w_ref, b_ref, o_ref):
    sc = _silu(c_ref[...]).astype(BF16)
    o_ref[...] = _dot(sc, w_ref[...].astype(BF16)) + b_ref[...]


def _adaln_mod(c, w_ada, b_ada):
    b, d = c.shape
    n = w_ada.shape[1]
    return pl.pallas_call(
        _mod_kernel,
        grid=(n // MOD_TILE,),
        in_specs=[
            pl.BlockSpec((b, d), lambda j: (0, 0)),
            pl.BlockSpec((d, MOD_TILE), lambda j: (0, j)),
            pl.BlockSpec((1, MOD_TILE), lambda j: (0, j)),
        ],
        out_specs=pl.BlockSpec((b, MOD_TILE), lambda j: (0, j)),
        out_shape=jax.ShapeDtypeStruct((b, n), F32),
        compiler_params=pltpu.CompilerParams(
            dimension_semantics=("arbitrary",), vmem_limit_bytes=VMEM_LIMIT_BYTES),
        name="adaln_mod",
    )(c, w_ada, b_ada.reshape(1, n))


def _mixer_kernel(tiles_per_row,
                  xn_ref, xp_ref, pos_ref, modn_ref, modp_ref, n1w_ref, invf_ref, win_ref, wdt_ref,
                  convw_ref, convb_ref, dtb_ref, alog_ref, dskip_ref, sinks_ref,
                  snw_ref, wout_ref, scan_ref, expand_ref,
                  o_ref,
                  raw_scr, h_scr, y_scr, q_scr, k_scr, v_scr, z_scr, xbc_scr, dt_scr, state_scr,
                  mix_scr):
    ts = xn_ref.shape[1]
    n_chunks = ts // CHUNK
    s = pl.program_id(0)
    i = lax.rem(s, tiles_per_row)
    cur = lax.rem(s, 2)
    layout, n_slots, _ = _expand_layout()

    def norm_next():
        modn = modn_ref[0]
        return _rmsnorm_mod(xn_ref[0], n1w_ref[...], modn[0:1], modn[1:2]).astype(BF16)

    @pl.when(s == 0)
    def _():
        modp = modp_ref[0]
        h0 = _rmsnorm_mod(xp_ref[0], n1w_ref[...], modp[0:1], modp[1:2]).astype(BF16)
        h_scr[...] = h0
        for j in range(N_AHEAD_PIECES):
            raw_scr[j] = _dot(h0, win_ref[j])
        y_scr[...] = jnp.zeros(y_scr.shape, F32)
        mix_scr[...] = jnp.zeros(mix_scr.shape, BF16)

    @pl.when(i == 0)
    def _():
        k_scr[0:CHUNK, :] = jnp.zeros((CHUNK, KV_WIDTH), BF16)
        v_scr[0:CHUNK, :] = jnp.zeros((CHUNK, KV_WIDTH), BF16)
        xbc_scr[0:SUBLANES, :] = jnp.zeros((SUBLANES, CONV_CH), F32)
        state_scr[...] = jnp.zeros(state_scr.shape, F32)

    for j in range(N_AHEAD_PIECES):
        xbc_scr[SUBLANES:SUBLANES + ts, j * MXU_WIDTH:(j + 1) * MXU_WIDTH] = raw_scr[IN_XBC + j]
    h = h_scr[...]
    proj = {j: _dot(h, win_ref[j]) for j in range(IN_Q, N_IN_PIECES)}

    half = HEAD_DIM // 2
    ang_d = pos_ref[0, 0].astype(F32) * invf_ref[...]
    lane_d = lax.broadcasted_iota(jnp.int32, ang_d.shape, 1)

    def spread(tbl):
        blocks = []
        for k in range(ROPE_PACK):
            t = pltpu.roll(tbl, LANES - half * k, axis=1) if k else tbl
            t = jnp.where(lane_d < half, t, pltpu.roll(t, half, axis=1))
            t = jnp.where(lane_d < 2 * half, t, pltpu.roll(t, 2 * half, axis=1))
            blocks.append(t)
        return jnp.concatenate(blocks, axis=0)

    cos = spread(jnp.cos(ang_d))
    sin = spread(jnp.sin(ang_d))
    lane = lax.broadcasted_iota(jnp.int32, (ts, LANES), 1)
    first_half = (lane % HEAD_DIM) < half
    sin_signed = jnp.where(first_half, -sin, sin)

    def rope(t):
        rot = jnp.where(first_half, pltpu.roll(t, LANES - half, axis=1),
                        pltpu.roll(t, half, axis=1))
        return t * cos + rot * sin_signed

    q_scale = 1.0 / math.sqrt(HEAD_DIM)
    for j in range(ATTN_WIDTH // LANES):
        piece, off = divmod(j * LANES, MXU_WIDTH)
        q_scr[:, j * LANES:(j + 1) * LANES] = (
            rope(proj[IN_Q + piece][:, off:off + LANES]) * q_scale).astype(BF16)
    k_scr[CHUNK:CHUNK + ts, :] = rope(proj[IN_KV][:, 0:KV_WIDTH]).astype(BF16)
    v_scr[CHUNK:CHUNK + ts, :] = proj[IN_KV][:, KV_WIDTH:2 * KV_WIDTH].astype(BF16)
    for j in range(SSM_WIDTH // MXU_WIDTH):
        z_scr[:, j * MXU_WIDTH:(j + 1) * MXU_WIDTH] = _silu(proj[IN_Z + j])
    dt_scr[...] = _dot_nt(wdt_ref[...], h)

    h_scr[...] = norm_next()

    a_neg = -jnp.exp(alog_ref[...])
    causal = (lax.broadcasted_iota(jnp.int32, (CHUNK, CHUNK), 0)
              >= lax.broadcasted_iota(jnp.int32, (CHUNK, CHUNK), 1))
    x_cols = slice(0, SSM_WIDTH)
    bc_cols = slice(SSM_WIDTH, CONV_CH)

    def conv_silu(row0, cols):
        win = xbc_scr[pl.ds(row0, SUBLANES + CHUNK), cols]
        conv = convb_ref[:, cols]
        for t in range(CONV_WIDTH):
            lo = SUBLANES - (CONV_WIDTH - 1) + t
            conv = conv + convw_ref[t:t + 1, cols] * win[lo:lo + CHUNK]
        return _silu(conv)

    n_ql = Q_PER_KV * CHUNK
    key_idx = lax.broadcasted_iota(jnp.int32, (CHUNK, n_ql), 0)
    qry_idx = lax.broadcasted_iota(jnp.int32, (CHUNK, n_ql), 1) % CHUNK
    upper = key_idx > qry_idx
    lane_head = lax.broadcasted_iota(jnp.int32, (1, n_ql), 1) // CHUNK

    def chunk_streams(c, tick):
        r0 = pl.multiple_of(c * CHUNK, CHUNK)
        prev_bias = jnp.where((i == 0) & (c == 0), MASK_VALUE, 0.0).astype(F32)
        kband = k_scr[pl.ds(r0, 2 * CHUNK), :]
        vband = v_scr[pl.ds(r0, 2 * CHUNK), :]
        qblk = q_scr[pl.ds(r0, CHUNK), :]

        def attn_stages(g):
            qg = jnp.concatenate(
                [qblk[:, (g * Q_PER_KV + j) * HEAD_DIM:(g * Q_PER_KV + j + 1) * HEAD_DIM]
                 for j in range(Q_PER_KV)], axis=0)
            s_t = _dot_nt(kband[:, g * HEAD_DIM:(g + 1) * HEAD_DIM], qg)
            yield
            sf = jnp.where(upper, s_t[:CHUNK] + prev_bias, s_t[CHUNK:])
            sink = jnp.zeros((1, n_ql), F32)
            for j in range(Q_PER_KV):
                sink = jnp.where(lane_head == j, sinks_ref[g * Q_PER_KV + j], sink)
            m = jnp.maximum(jnp.max(sf, axis=0, keepdims=True), sink)
            pf = jnp.exp(sf - m)
            denom = jnp.sum(pf, axis=0, keepdims=True) + jnp.exp(sink - m)
            p_t = jnp.concatenate([jnp.where(upper, pf, 0.0), jnp.where(upper, 0.0, pf)],
                                  axis=0).astype(BF16)
            yield
            o_t = _dot_tn(vband[:, g * HEAD_DIM:(g + 1) * HEAD_DIM], p_t)
            o_t = o_t * (1.0 / denom)
            yield
            tiles = []
            for pp in range(Q_PER_KV // 2):
                blk = jnp.concatenate([o_t[:, (2 * pp) * CHUNK:(2 * pp + 1) * CHUNK],
                                       o_t[:, (2 * pp + 1) * CHUNK:(2 * pp + 2) * CHUNK]], axis=0)
                tiles.append(blk.T)
            mix_scr[cur, pl.ds(r0, CHUNK),
                    g * Q_PER_KV * HEAD_DIM:(g + 1) * Q_PER_KV * HEAD_DIM] = (
                jnp.concatenate(tiles, axis=1).astype(BF16))

        def ssd_stages():
            xs = conv_silu(r0, x_cols)
            bc = conv_silu(r0, bc_cols).astype(BF16)
            yield
            dt = _softplus(dt_scr[0:SSM_HEADS, pl.ds(r0, CHUNK)] + dtb_ref[...])
            a = dt * a_neg
            pad = jnp.zeros((SUBLANES, CHUNK), F32)
            cs = _dot(jnp.concatenate(_split_bf16(a, 3) + [pad], axis=0).astype(BF16),
                      scan_ref[...])
            cs = cs[0:8] + cs[8:16] + cs[16:24]
            a_cs = cs[:, :CHUNK]
            a_tot = cs[:, CHUNK:]
            ea = jnp.exp(a_cs)
            w_end = dt * jnp.exp(a_tot - a_cs)
            rows = (_split_bf16(a_cs, 3) + _split_bf16(ea, 2) + _split_bf16(w_end, 2)
                    + _split_bf16(dt, 2))
            assert len(rows) == n_slots
            rows.append(jnp.zeros((LANES - n_slots * HEAD_SLOT, CHUNK), F32))
            packed = jnp.concatenate(rows, axis=0).T.astype(BF16)
            ex = _dot(packed, expand_ref[...])

            def ex_block(name):
                _, _, col0, width = layout[name]
                return ex[:, col0:col0 + width]

            acs_b = ex_block("acs")
            ea_e = ex_block("ea")
            xdt = (xs * ex_block("dt")).astype(BF16)
            xw = (xs * ex_block("w")).astype(BF16)
            yield
            ys = []
            for g in range(SSM_GROUPS):
                gs = slice(g * GROUP_WIDTH, (g + 1) * GROUP_WIDTH)
                bg = bc[:, g * D_STATE:(g + 1) * D_STATE]
                cg = bc[:, (SSM_GROUPS + g) * D_STATE:(SSM_GROUPS + g + 1) * D_STATE]
                cb = _dot_nt(cg, bg)
                s_old = state_scr[g]
                y_off = _dot(cg, s_old.astype(BF16)) * ea_e[:, gs]
                yd = []
                for j in range(HEADS_PER_GROUP):
                    hh = g * HEADS_PER_GROUP + j
                    seg = acs_b[:, hh * LANES:(hh + 1) * LANES] - a_cs[hh:hh + 1, :]
                    dec = jnp.exp(jnp.where(causal, seg, MASK_VALUE))
                    if j == 0:
                        tick()
                    yd.append(_dot((cb * dec).astype(BF16),
                                   xdt[:, hh * SSM_HEAD_DIM:(hh + 1) * SSM_HEAD_DIM]))
                yield
                y_g = jnp.concatenate(yd, axis=1) + y_off + xs[:, gs] * dskip_ref[:, gs]
                state_scr[g] = s_old * ea_e[CHUNK - 1:CHUNK, gs] + _dot_tn(bg, xw[:, gs])
                y_g = y_g * z_scr[pl.ds(r0, CHUNK), gs]
                y_g = y_g * lax.rsqrt(jnp.mean(y_g * y_g, axis=-1, keepdims=True) + EPS)
                ys.append(y_g * snw_ref[:, gs])
                yield
            mix_scr[cur, pl.ds(r0, CHUNK), ATTN_WIDTH:] = jnp.concatenate(ys, axis=1).astype(BF16)

        return {"a0": attn_stages(0), "a1": attn_stages(1), "s": ssd_stages()}

    def pair_body(cp, carry):
        def filler_units():
            for jj in range(max(IN_PER_ITER, OUT_PER_ITER)):
                if jj < IN_PER_ITER:
                    j = cp * IN_PER_ITER + jj
                    raw_scr[j] = _dot(h_scr[...], win_ref[j])
                    yield
                if jj < OUT_PER_ITER:
                    j = cp * OUT_PER_ITER + jj
                    y_scr[j] = _dot(mix_scr[1 - cur], wout_ref[j])
                    yield

        fill = filler_units()
        tick = lambda: next(fill, None)
        streams = {}
        for tag, sub in (("A", 0), ("B", 1)):
            for name, gen in chunk_streams(CHUNKS_PER_ITER * cp + sub, tick).items():
                streams[tag + "." + name] = gen
        for name in STAGE_ORDER:
            next(streams[name], None)
        for stream in streams.values():
            assert next(stream, "done") == "done"
        for _ in fill:
            pass
        return carry

    lax.fori_loop(0, ITERS_PER_TILE, pair_body, 0)

    k_scr[0:CHUNK, :] = k_scr[ts:ts + CHUNK, :]
    v_scr[0:CHUNK, :] = v_scr[ts:ts + CHUNK, :]
    xbc_scr[0:SUBLANES, :] = xbc_scr[ts:ts + SUBLANES, :]

    gate1 = modp_ref[0][2:3]
    for j in range(N_OUT_PIECES):
        cols = slice(j * MXU_WIDTH, (j + 1) * MXU_WIDTH)
        o_ref[0, :, cols] = xp_ref[0, :, cols] + gate1[:, cols] * y_scr[j]


def _const_spec(shape):
    nd = len(shape)
    return pl.BlockSpec(shape, lambda *_: (0,) * nd, pipeline_mode=pl.Buffered(1))


def _mixer(x, pos4, mod3, n1w, invf, win, wdt, convw, convb, dtb, alog, dskip, sinks, snw, wout, scan,
           expand):
    b, s, d = x.shape
    ts = MIXER_TILE
    assert ts == ITERS_PER_TILE * CHUNKS_PER_ITER * CHUNK and s % ts == 0
    tpr = s // ts
    n_tiles = b * tpr

    def nxt(step):
        t = jnp.minimum(step + 1, n_tiles - 1)
        return t // tpr, t % tpr

    def prv(step):
        t = jnp.maximum(step - 1, 0)
        return t // tpr, t % tpr

    def cur(step):
        t = jnp.minimum(step, n_tiles - 1)
        return t // tpr, t % tpr

    in_specs = [
        pl.BlockSpec((1, ts, d), lambda st: (*nxt(st), 0)),
        pl.BlockSpec((1, ts, d), lambda st: (*prv(st), 0)),
        pl.BlockSpec((1, 1, ts // ROPE_PACK, LANES), lambda st: (*cur(st), 0, 0)),
        pl.BlockSpec((1, N_MOD, d), lambda st: (nxt(st)[0], 0, 0)),
        pl.BlockSpec((1, N_MOD, d), lambda st: (prv(st)[0], 0, 0)),
        _const_spec(n1w.shape), _const_spec(invf.shape), _const_spec(win.shape),
        _const_spec(wdt.shape),
        _const_spec(convw.shape), _const_spec(convb.shape), _const_spec(dtb.shape),
        _const_spec(alog.shape), _const_spec(dskip.shape),
        pl.BlockSpec(memory_space=pltpu.SMEM),
        _const_spec(snw.shape), _const_spec(wout.shape), _const_spec(scan.shape),
        _const_spec(expand.shape),
    ]
    scratch = [
        pltpu.VMEM((N_AHEAD_PIECES, ts, MXU_WIDTH), F32),
        pltpu.VMEM((ts, d), BF16),
        pltpu.VMEM((N_OUT_PIECES, ts, MXU_WIDTH), F32),
        pltpu.VMEM((ts, ATTN_WIDTH), BF16),
        pltpu.VMEM((CHUNK + ts, KV_WIDTH), BF16),
        pltpu.VMEM((CHUNK + ts, KV_WIDTH), BF16),
        pltpu.VMEM((ts, SSM_WIDTH), F32),
        pltpu.VMEM((SUBLANES + ts, CONV_CH), F32),
        pltpu.VMEM((2 * SUBLANES, ts), F32),
        pltpu.VMEM((SSM_GROUPS, D_STATE, GROUP_WIDTH), F32),
        pltpu.VMEM((2, ts, MIX_WIDTH), BF16),
    ]
    return pl.pallas_call(
        lambda *refs: _mixer_kernel(tpr, *refs),
        grid=(n_tiles + 1,),
        in_specs=in_specs,
        out_specs=pl.BlockSpec((1, ts, d), lambda st: (*prv(st), 0)),
        out_shape=jax.ShapeDtypeStruct((b, s, d), F32),
        scratch_shapes=scratch,
        compiler_params=pltpu.CompilerParams(
            dimension_semantics=("arbitrary",), vmem_limit_bytes=VMEM_LIMIT_BYTES),
        name="mixer",
    )(x, x, pos4, mod3, mod3, n1w, invf, win, wdt, convw, convb, dtb, alog, dskip, sinks, snw, wout,
      scan, expand)


def _ffn_kernel(x_ref, mod_ref, n2w_ref, wg_ref, wu_ref, wd_ref, nfw_ref, o_ref):
    mod = mod_ref[0]
    shift2, scale2, gate2 = mod[3:4], mod[4:5], mod[5:6]
    x = x_ref[0]
    h = _rmsnorm_mod(x, n2w_ref[...], shift2, scale2).astype(BF16)
    g = _dot(h, wg_ref[...])
    u = _dot(h, wu_ref[...])
    act = (_silu(g) * u).astype(BF16)
    x2 = x + gate2 * _dot(act, wd_ref[...])
    ms = jnp.mean(x2 * x2, axis=-1, keepdims=True)
    o_ref[0] = x2 * lax.rsqrt(ms + EPS) * nfw_ref[...]


def _ffn(x1, mod3, n2w, wg, wu, wd, nfw):
    b, s, d = x1.shape
    tm = FFN_TILE
    return pl.pallas_call(
        _ffn_kernel,
        grid=(b, s // tm),
        in_specs=[
            pl.BlockSpec((1, tm, d), lambda bb, i: (bb, i, 0)),
            pl.BlockSpec((1, N_MOD, d), lambda bb, i: (bb, 0, 0)),
            _const_spec(n2w.shape), _const_spec(wg.shape), _const_spec(wu.shape),
            _const_spec(wd.shape), _const_spec(nfw.shape),
        ],
        out_specs=pl.BlockSpec((1, tm, d), lambda bb, i: (bb, i, 0)),
        out_shape=jax.ShapeDtypeStruct((b, s, d), F32),
        compiler_params=pltpu.CompilerParams(
            dimension_semantics=("arbitrary", "arbitrary"), vmem_limit_bytes=VMEM_LIMIT_BYTES),
        name="ffn",
    )(x1, mod3, n2w, wg, wu, wd, nfw)


def _head_rows(v):
    return jnp.broadcast_to(v.astype(F32)[:, None], (SSM_HEADS, LANES))


def _column_pieces(w):
    k, n = w.shape
    return w.reshape(k, n // MXU_WIDTH, MXU_WIDTH).transpose(1, 0, 2)


def kernel(x, c, positions, w_ada, b_ada, norm1_w, w_in, conv_w, conv_b, dt_bias, a_log, d_skip,
           attn_sinks, ssm_norm_w, w_out, norm2_w, w_gate_up, w_down, final_norm_w):
    b, s, d = x.shape
    depth = w_ada.shape[0]
    half = HEAD_DIM // 2
    inv_freq = ROPE_THETA ** (-jnp.arange(half, dtype=F32) / half)
    invf = jnp.tile(inv_freq, LANES // half).reshape(1, LANES)
    scan = _scan_matrix()
    expand = _expand_matrix()
    ts = MIXER_TILE
    pos4 = positions.reshape(b, s // ts, ROPE_PACK, ts // ROPE_PACK).transpose(0, 1, 3, 2)
    pos4 = jnp.repeat(pos4, half, axis=-1)
    o1 = ATTN_WIDTH + 2 * KV_WIDTH
    o2 = o1 + SSM_WIDTH
    o3 = o2 + CONV_CH
    assert depth == 1, "the final norm is fused into the (single) layer's ffn call"
    layer = 0
    mod3 = _adaln_mod(c, w_ada[layer], b_ada[layer]).reshape(b, N_MOD, d)
    w_in_l = w_in[layer].astype(BF16)
    w_cat = jnp.concatenate([w_in_l[:, o2:o3], w_in_l[:, :o1], w_in_l[:, o1:o2]], axis=1)
    assert w_cat.shape[1] == N_IN_PIECES * MXU_WIDTH
    wdt = jnp.pad(w_in_l[:, o3:].T, ((0, 2 * SUBLANES - SSM_HEADS), (0, 0)))
    x1 = _mixer(
        x, pos4, mod3, norm1_w[layer].reshape(1, d), invf, _column_pieces(w_cat), wdt,
        conv_w[layer], conv_b[layer].reshape(1, CONV_CH),
        _head_rows(dt_bias[layer]), _head_rows(a_log[layer]),
        jnp.repeat(d_skip[layer].astype(F32), SSM_HEAD_DIM).reshape(1, SSM_WIDTH),
        attn_sinks[layer].astype(F32),
        ssm_norm_w[layer].reshape(1, SSM_WIDTH), _column_pieces(w_out[layer].astype(BF16)),
        scan, expand)
    wgu = w_gate_up[layer].astype(BF16)
    return _ffn(x1, mod3, norm2_w[layer].reshape(1, d), wgu[:, :D_FF], wgu[:, D_FF:],
                w_down[layer].astype(BF16), final_norm_w.reshape(1, d))
```

```python
import math

import jax
import jax.numpy as jnp
import numpy as np
from jax import lax
from jax.experimental import pallas as pl
from jax.experimental.pallas import tpu as pltpu

D_MODEL = 1024
HEAD_DIM = 64
N_Q_HEADS = 8
N_KV_HEADS = 2
Q_PER_KV = N_Q_HEADS // N_KV_HEADS
ATTN_WIDTH = N_Q_HEADS * HEAD_DIM
KV_WIDTH = N_KV_HEADS * HEAD_DIM
WINDOW = 128
ROPE_THETA = 10000.0
SSM_HEADS = 8
SSM_HEAD_DIM = 64
SSM_WIDTH = SSM_HEADS * SSM_HEAD_DIM
SSM_GROUPS = 2
HEADS_PER_GROUP = SSM_HEADS // SSM_GROUPS
GROUP_WIDTH = HEADS_PER_GROUP * SSM_HEAD_DIM
D_STATE = 128
CONV_WIDTH = 4
CHUNK = 128
CONV_CH = SSM_WIDTH + 2 * SSM_GROUPS * D_STATE
MIX_WIDTH = ATTN_WIDTH + SSM_WIDTH
D_FF = 2816
N_MOD = 6
EPS = 1e-6

LANES = 128
SUBLANES = 8
MXU_WIDTH = 256
HEAD_SLOT = 8
ROPE_PACK = LANES // (HEAD_DIM // 2)
MASK_VALUE = -1e30
VMEM_LIMIT_BYTES = 56 * 1024 * 1024

MIXER_TILE = 512
FFN_TILE = 512
MOD_TILE = 1024

F32 = jnp.float32
BF16 = jnp.bfloat16

IN_XBC = 0
IN_Q = CONV_CH // MXU_WIDTH
IN_KV = IN_Q + ATTN_WIDTH // MXU_WIDTH
IN_Z = IN_KV + 2 * KV_WIDTH // MXU_WIDTH
N_IN_PIECES = IN_Z + SSM_WIDTH // MXU_WIDTH
N_OUT_PIECES = D_MODEL // MXU_WIDTH
CHUNKS_PER_ITER = 4
ITERS_PER_TILE = MIXER_TILE // (CHUNKS_PER_ITER * CHUNK)
N_AHEAD_PIECES = IN_Q
IN_PER_ITER = N_AHEAD_PIECES // ITERS_PER_TILE
OUT_PER_ITER = N_OUT_PIECES // ITERS_PER_TILE
assert IN_PER_ITER * ITERS_PER_TILE == N_AHEAD_PIECES and OUT_PER_ITER * ITERS_PER_TILE == N_OUT_PIECES

ATTN_STAGES = 4
SSD_STAGES = 7
_SSD_NEEDS_PREV_PAST = {2: 4, 4: 6}


def _stage_order(n_chunks):
    prog = {(k, name): 0 for k in range(n_chunks) for name in ("a0", "a1", "s")}
    order = []
    while any(prog[k, "s"] < SSD_STAGES for k in range(n_chunks)):
        for k in range(n_chunks):
            for name in ("a0", "a1"):
                if prog[k, name] < ATTN_STAGES:
                    prog[k, name] += 1
                    order.append((k, name))
        for k in range(n_chunks):
            stage = prog[k, "s"]
            need = _SSD_NEEDS_PREV_PAST.get(stage) if k else None
            if stage < SSD_STAGES and (need is None or prog[k - 1, "s"] >= need):
                prog[k, "s"] += 1
                order.append((k, "s"))
    assert all(v == (SSD_STAGES if name == "s" else ATTN_STAGES) for (_, name), v in prog.items())
    return order

_EXPAND_ITEMS = (("acs", 3, LANES), ("ea", 2, SSM_HEAD_DIM), ("w", 2, SSM_HEAD_DIM),
                 ("dt", 2, SSM_HEAD_DIM))


def _expand_layout():
    slot = 0
    col = 0
    layout = {}
    for name, parts, rep in _EXPAND_ITEMS:
        layout[name] = (slot, parts, col, rep * SSM_HEADS)
        slot += parts
        col += rep * SSM_HEADS
    return layout, slot, col


def _expand_matrix():
    layout, nslots, ncols = _expand_layout()
    assert nslots * HEAD_SLOT <= LANES
    e = np.zeros((LANES, ncols), np.float32)
    for name, parts, rep in _EXPAND_ITEMS:
        slot0, _, col0, _ = layout[name]
        for p in range(parts):
            for hh in range(SSM_HEADS):
                e[(slot0 + p) * HEAD_SLOT + hh, col0 + hh * rep:col0 + (hh + 1) * rep] = 1.0
    return jnp.asarray(e, BF16)


def _scan_matrix():
    t = np.arange(CHUNK)
    upper = (t[:, None] <= t[None, :]).astype(np.float32)
    return jnp.asarray(np.concatenate([upper, np.ones((CHUNK, CHUNK), np.float32)], axis=1), BF16)


def _silu(x):
    hx = 0.5 * x
    return hx + hx * jnp.tanh(hx)


def _softplus(x):
    return jnp.maximum(x, 0.0) + jnp.log1p(jnp.exp(-jnp.abs(x)))


def _dot(a, b):
    return jnp.dot(a, b, preferred_element_type=F32)


def _dot_nt(a, b):
    return lax.dot_general(a, b, (((1,), (1,)), ((), ())), preferred_element_type=F32)


def _dot_tn(a, b):
    return lax.dot_general(a, b, (((0,), (0,)), ((), ())), preferred_element_type=F32)


def _rmsnorm_mod(x, norm_w, shift, scale):
    ms = jnp.mean(x * x, axis=-1, keepdims=True)
    return x * lax.rsqrt(ms + EPS) * (norm_w * (1.0 + scale)) + shift


def _split_bf16(v, parts):
    out = []
    r = v
    for _ in range(parts):
        p = r.astype(BF16).astype(F32)
        out.append(p)
        r = r - p
    return out


def _mod_kernel(c_ref, w_ref, b_ref, o_ref):
    sc = _silu(c_ref[...]).astype(BF16)
    o_ref[...] = _dot(sc, w_ref[...].astype(BF16)) + b_ref[...]


def _adaln_mod(c, w_ada, b_ada):
    b, d = c.shape
    n = w_ada.shape[1]
    return pl.pallas_call(
        _mod_kernel,
        grid=(n // MOD_TILE,),
        in_specs=[
            pl.BlockSpec((b, d), lambda j: (0, 0)),
            pl.BlockSpec((d, MOD_TILE), lambda j: (0, j)),
            pl.BlockSpec((1, MOD_TILE), lambda j: (0, j)),
        ],
        out_specs=pl.BlockSpec((b, MOD_TILE), lambda j: (0, j)),
        out_shape=jax.ShapeDtypeStruct((b, n), F32),
        compiler_params=pltpu.CompilerParams(
            dimension_semantics=("arbitrary",), vmem_limit_bytes=VMEM_LIMIT_BYTES),
        name="adaln_mod",
    )(c, w_ada, b_ada.reshape(1, n))


def _mixer_kernel(tiles_per_row,
                  xn_ref, xp_ref, pos_ref, modn_ref, modp_ref, n1w_ref, invf_ref, win_ref, wdt_ref,
                  convw_ref, convb_ref, dtb_ref, alog_ref, dskip_ref, sinks_ref,
                  snw_ref, wout_ref, scan_ref, expand_ref,
                  o_ref,
                  raw_scr, h_scr, y_scr, q_scr, k_scr, v_scr, z_scr, xbc_scr, dt_scr, state_scr,
                  mix_scr):
    ts = xn_ref.shape[1]
    n_chunks = ts // CHUNK
    s = pl.program_id(0)
    i = lax.rem(s, tiles_per_row)
    cur = lax.rem(s, 2)
    layout, n_slots, _ = _expand_layout()

    def norm_next():
        modn = modn_ref[0]
        return _rmsnorm_mod(xn_ref[0], n1w_ref[...], modn[0:1], modn[1:2]).astype(BF16)

    @pl.when(s == 0)
    def _():
        modp = modp_ref[0]
        h0 = _rmsnorm_mod(xp_ref[0], n1w_ref[...], modp[0:1], modp[1:2]).astype(BF16)
        h_scr[...] = h0
        for j in range(N_AHEAD_PIECES):
            raw_scr[j] = _dot(h0, win_ref[j])
        y_scr[...] = jnp.zeros(y_scr.shape, F32)
        mix_scr[...] = jnp.zeros(mix_scr.shape, BF16)

    @pl.when(i == 0)
    def _():
        k_scr[0:CHUNK, :] = jnp.zeros((CHUNK, KV_WIDTH), BF16)
        v_scr[0:CHUNK, :] = jnp.zeros((CHUNK, KV_WIDTH), BF16)
        xbc_scr[0:SUBLANES, :] = jnp.zeros((SUBLANES, CONV_CH), F32)
        state_scr[...] = jnp.zeros(state_scr.shape, F32)

    for j in range(N_AHEAD_PIECES):
        xbc_scr[SUBLANES:SUBLANES + ts, j * MXU_WIDTH:(j + 1) * MXU_WIDTH] = raw_scr[IN_XBC + j]
    h = h_scr[...]
    proj = {j: _dot(h, win_ref[j]) for j in range(IN_Q, N_IN_PIECES)}

    half = HEAD_DIM // 2
    ang_d = pos_ref[0, 0].astype(F32) * invf_ref[...]
    lane_d = lax.broadcasted_iota(jnp.int32, ang_d.shape, 1)

    def spread(tbl):
        blocks = []
        for k in range(ROPE_PACK):
            t = pltpu.roll(tbl, LANES - half * k, axis=1) if k else tbl
            t = jnp.where(lane_d < half, t, pltpu.roll(t, half, axis=1))
            t = jnp.where(lane_d < 2 * half, t, pltpu.roll(t, 2 * half, axis=1))
            blocks.append(t)
        return jnp.concatenate(blocks, axis=0)

    cos = spread(jnp.cos(ang_d))
    sin = spread(jnp.sin(ang_d))
    lane = lax.broadcasted_iota(jnp.int32, (ts, LANES), 1)
    first_half = (lane % HEAD_DIM) < half
    sin_signed = jnp.where(first_half, -sin, sin)

    def rope(t):
        rot = jnp.where(first_half, pltpu.roll(t, LANES - half, axis=1),
                        pltpu.roll(t, half, axis=1))
        return t * cos + rot * sin_signed

    q_scale = 1.0 / math.sqrt(HEAD_DIM)
    for j in range(ATTN_WIDTH // LANES):
        piece, off = divmod(j * LANES, MXU_WIDTH)
        q_scr[:, j * LANES:(j + 1) * LANES] = (
            rope(proj[IN_Q + piece][:, off:off + LANES]) * q_scale).astype(BF16)
    k_scr[CHUNK:CHUNK + ts, :] = rope(proj[IN_KV][:, 0:KV_WIDTH]).astype(BF16)
    v_scr[CHUNK:CHUNK + ts, :] = proj[IN_KV][:, KV_WIDTH:2 * KV_WIDTH].astype(BF16)
    for j in range(SSM_WIDTH // MXU_WIDTH):
        z_scr[:, j * MXU_WIDTH:(j + 1) * MXU_WIDTH] = _silu(proj[IN_Z + j])
    dt_scr[...] = _dot_nt(wdt_ref[...], h)

    h_scr[...] = norm_next()

    a_neg = -jnp.exp(alog_ref[...])
    causal = (lax.broadcasted_iota(jnp.int32, (CHUNK, CHUNK), 0)
              >= lax.broadcasted_iota(jnp.int32, (CHUNK, CHUNK), 1))
    x_cols = slice(0, SSM_WIDTH)
    bc_cols = slice(SSM_WIDTH, CONV_CH)
    group_head = lax.broadcasted_iota(jnp.int32, (CHUNK, GROUP_WIDTH), 1) // SSM_HEAD_DIM

    def conv_silu(row0, cols):
        win = xbc_scr[pl.ds(row0, SUBLANES + CHUNK), cols]
        conv = convb_ref[:, cols]
        for t in range(CONV_WIDTH):
            lo = SUBLANES - (CONV_WIDTH - 1) + t
            conv = conv + convw_ref[t:t + 1, cols] * win[lo:lo + CHUNK]
        return _silu(conv)

    n_ql = Q_PER_KV * CHUNK
    key_idx = lax.broadcasted_iota(jnp.int32, (CHUNK, n_ql), 0)
    qry_idx = lax.broadcasted_iota(jnp.int32, (CHUNK, n_ql), 1) % CHUNK
    upper = key_idx > qry_idx
    lane_head = lax.broadcasted_iota(jnp.int32, (1, n_ql), 1) // CHUNK

    def chunk_streams(c, tick):
        r0 = c * CHUNK if isinstance(c, int) else pl.multiple_of(c * CHUNK, CHUNK)
        prev_bias = jnp.where((i == 0) & (c == 0), MASK_VALUE, 0.0).astype(F32)
        kband = k_scr[pl.ds(r0, 2 * CHUNK), :]
        vband = v_scr[pl.ds(r0, 2 * CHUNK), :]
        qblk = q_scr[pl.ds(r0, CHUNK), :]

        def attn_stages(g):
            qg = jnp.concatenate(
                [qblk[:, (g * Q_PER_KV + j) * HEAD_DIM:(g * Q_PER_KV + j + 1) * HEAD_DIM]
                 for j in range(Q_PER_KV)], axis=0)
            s_t = _dot_nt(kband[:, g * HEAD_DIM:(g + 1) * HEAD_DIM], qg)
            yield
            sf = jnp.where(upper, s_t[:CHUNK] + prev_bias, s_t[CHUNK:])
            sink = jnp.zeros((1, n_ql), F32)
            for j in range(Q_PER_KV):
                sink = jnp.where(lane_head == j, sinks_ref[g * Q_PER_KV + j], sink)
            m = jnp.maximum(jnp.max(sf, axis=0, keepdims=True), sink)
            pf = jnp.exp(sf - m)
            denom = jnp.sum(pf, axis=0, keepdims=True) + jnp.exp(sink - m)
            p_t = jnp.concatenate([jnp.where(upper, pf, 0.0), jnp.where(upper, 0.0, pf)],
                                  axis=0).astype(BF16)
            yield
            o_t = _dot_tn(vband[:, g * HEAD_DIM:(g + 1) * HEAD_DIM], p_t)
            o_t = o_t * (1.0 / denom)
            yield
            tiles = []
            for pp in range(Q_PER_KV // 2):
                blk = jnp.concatenate([o_t[:, (2 * pp) * CHUNK:(2 * pp + 1) * CHUNK],
                                       o_t[:, (2 * pp + 1) * CHUNK:(2 * pp + 2) * CHUNK]], axis=0)
                tiles.append(blk.T)
            mix_scr[cur, pl.ds(r0, CHUNK),
                    g * Q_PER_KV * HEAD_DIM:(g + 1) * Q_PER_KV * HEAD_DIM] = (
                jnp.concatenate(tiles, axis=1).astype(BF16))

        def ssd_stages():
            xs = conv_silu(r0, x_cols)
            bc = conv_silu(r0, bc_cols)
            yield
            dt = _softplus(dt_scr[0:SSM_HEADS, pl.ds(r0, CHUNK)] + dtb_ref[...])
            a = dt * a_neg
            pad = jnp.zeros((SUBLANES, CHUNK), F32)
            cs = _dot(jnp.concatenate(_split_bf16(a, 3) + [pad], axis=0).astype(BF16),
                      scan_ref[...])
            cs = cs[0:8] + cs[8:16] + cs[16:24]
            a_cs = cs[:, :CHUNK]
            a_tot = cs[:, CHUNK:]
            ea = jnp.exp(a_cs)
            w_end = dt * jnp.exp(a_tot - a_cs)
            rows = (_split_bf16(a_cs, 3) + _split_bf16(ea, 2) + _split_bf16(w_end, 2)
                    + _split_bf16(dt, 2))
            assert len(rows) == n_slots
            rows.append(jnp.zeros((LANES - n_slots * HEAD_SLOT, CHUNK), F32))
            packed = jnp.concatenate(rows, axis=0).T.astype(BF16)
            ex = _dot(packed, expand_ref[...])

            def ex_block(name):
                _, _, col0, width = layout[name]
                return ex[:, col0:col0 + width]

            acs_b = ex_block("acs")
            ea_e = ex_block("ea")
            xdt = (xs * ex_block("dt")).astype(BF16)
            xw = (xs * ex_block("w")).astype(BF16)
            yield
            ys = []
            for g in range(SSM_GROUPS):
                gs = slice(g * GROUP_WIDTH, (g + 1) * GROUP_WIDTH)
                bg_t = bc[:, g * D_STATE:(g + 1) * D_STATE].T.astype(BF16)
                cg = bc[:, (SSM_GROUPS + g) * D_STATE:(SSM_GROUPS + g + 1) * D_STATE].astype(BF16)
                cb = _dot(cg, bg_t)
                s_old = state_scr[g]
                y_off = _dot(cg, s_old.astype(BF16)) * ea_e[:, gs]
                ms = []
                xm = []
                for j in range(HEADS_PER_GROUP):
                    hh = g * HEADS_PER_GROUP + j
                    seg = acs_b[:, hh * LANES:(hh + 1) * LANES] - a_cs[hh:hh + 1, :]
                    dec = jnp.exp(jnp.where(causal, seg, MASK_VALUE))
                    ms.append((cb * dec).astype(BF16))
                    xm.append(jnp.where(group_head == j, xdt[:, gs], jnp.zeros((), BF16)))
                tick()
                y_diag = _dot(jnp.concatenate(ms, axis=1), jnp.concatenate(xm, axis=0))
                yield
                y_g = y_diag + y_off + xs[:, gs] * dskip_ref[:, gs]
                state_scr[g] = s_old * ea_e[CHUNK - 1:CHUNK, gs] + _dot(bg_t, xw[:, gs])
                y_g = y_g * z_scr[pl.ds(r0, CHUNK), gs]
                y_g = y_g * lax.rsqrt(jnp.mean(y_g * y_g, axis=-1, keepdims=True) + EPS)
                ys.append(y_g * snw_ref[:, gs])
                yield
            mix_scr[cur, pl.ds(r0, CHUNK), ATTN_WIDTH:] = jnp.concatenate(ys, axis=1).astype(BF16)

        return {"a0": attn_stages(0), "a1": attn_stages(1), "s": ssd_stages()}

    def pair_body(cp, carry):
        def filler_units():
            for jj in range(max(IN_PER_ITER, OUT_PER_ITER)):
                if jj < IN_PER_ITER:
                    j = cp * IN_PER_ITER + jj
                    raw_scr[j] = _dot(h_scr[...], win_ref[j])
                    yield
                if jj < OUT_PER_ITER:
                    j = cp * OUT_PER_ITER + jj
                    y_scr[j] = _dot(mix_scr[1 - cur], wout_ref[j])
                    yield

        fill = filler_units()
        tick = lambda: next(fill, None)
        streams = {}
        for sub in range(CHUNKS_PER_ITER):
            for name, gen in chunk_streams(CHUNKS_PER_ITER * cp + sub, tick).items():
                streams[sub, name] = gen
        for key in _stage_order(CHUNKS_PER_ITER):
            next(streams[key], None)
        for stream in streams.values():
            assert next(stream, "done") == "done"
        for _ in fill:
            pass
        return carry

    if ITERS_PER_TILE == 1:
        pair_body(0, 0)
    else:
        lax.fori_loop(0, ITERS_PER_TILE, pair_body, 0)

    k_scr[0:CHUNK, :] = k_scr[ts:ts + CHUNK, :]
    v_scr[0:CHUNK, :] = v_scr[ts:ts + CHUNK, :]
    xbc_scr[0:SUBLANES, :] = xbc_scr[ts:ts + SUBLANES, :]

    gate1 = modp_ref[0][2:3]
    for j in range(N_OUT_PIECES):
        cols = slice(j * MXU_WIDTH, (j + 1) * MXU_WIDTH)
        o_ref[0, :, cols] = xp_ref[0, :, cols] + gate1[:, cols] * y_scr[j]


def _const_spec(shape):
    nd = len(shape)
    return pl.BlockSpec(shape, lambda *_: (0,) * nd, pipeline_mode=pl.Buffered(1))


def _mixer(x, pos4, mod3, n1w, invf, win, wdt, convw, convb, dtb, alog, dskip, sinks, snw, wout, scan,
           expand):
    b, s, d = x.shape
    ts = MIXER_TILE
    assert ts == ITERS_PER_TILE * CHUNKS_PER_ITER * CHUNK and s % ts == 0
    tpr = s // ts
    n_tiles = b * tpr

    def nxt(step):
        t = jnp.minimum(step + 1, n_tiles - 1)
        return t // tpr, t % tpr

    def prv(step):
        t = jnp.maximum(step - 1, 0)
        return t // tpr, t % tpr

    def cur(step):
        t = jnp.minimum(step, n_tiles - 1)
        return t // tpr, t % tpr

    in_specs = [
        pl.BlockSpec((1, ts, d), lambda st: (*nxt(st), 0)),
        pl.BlockSpec((1, ts, d), lambda st: (*prv(st), 0)),
        pl.BlockSpec((1, 1, ts // ROPE_PACK, LANES), lambda st: (*cur(st), 0, 0)),
        pl.BlockSpec((1, N_MOD, d), lambda st: (nxt(st)[0], 0, 0)),
        pl.BlockSpec((1, N_MOD, d), lambda st: (prv(st)[0], 0, 0)),
        _const_spec(n1w.shape), _const_spec(invf.shape), _const_spec(win.shape),
        _const_spec(wdt.shape),
        _const_spec(convw.shape), _const_spec(convb.shape), _const_spec(dtb.shape),
        _const_spec(alog.shape), _const_spec(dskip.shape),
        pl.BlockSpec(memory_space=pltpu.SMEM),
        _const_spec(snw.shape), _const_spec(wout.shape), _const_spec(scan.shape),
        _const_spec(expand.shape),
    ]
    scratch = [
        pltpu.VMEM((N_AHEAD_PIECES, ts, MXU_WIDTH), F32),
        pltpu.VMEM((ts, d), BF16),
        pltpu.VMEM((N_OUT_PIECES, ts, MXU_WIDTH), F32),
        pltpu.VMEM((ts, ATTN_WIDTH), BF16),
        pltpu.VMEM((CHUNK + ts, KV_WIDTH), BF16),
        pltpu.VMEM((CHUNK + ts, KV_WIDTH), BF16),
        pltpu.VMEM((ts, SSM_WIDTH), F32),
        pltpu.VMEM((SUBLANES + ts, CONV_CH), F32),
        pltpu.VMEM((2 * SUBLANES, ts), F32),
        pltpu.VMEM((SSM_GROUPS, D_STATE, GROUP_WIDTH), F32),
        pltpu.VMEM((2, ts, MIX_WIDTH), BF16),
    ]
    return pl.pallas_call(
        lambda *refs: _mixer_kernel(tpr, *refs),
        grid=(n_tiles + 1,),
        in_specs=in_specs,
        out_specs=pl.BlockSpec((1, ts, d), lambda st: (*prv(st), 0)),
        out_shape=jax.ShapeDtypeStruct((b, s, d), F32),
        scratch_shapes=scratch,
        compiler_params=pltpu.CompilerParams(
            dimension_semantics=("arbitrary",), vmem_limit_bytes=VMEM_LIMIT_BYTES),
        name="mixer",
    )(x, x, pos4, mod3, mod3, n1w, invf, win, wdt, convw, convb, dtb, alog, dskip, sinks, snw, wout,
      scan, expand)


def _ffn_kernel(x_ref, mod_ref, n2w_ref, wg_ref, wu_ref, wd_ref, nfw_ref, o_ref):
    mod = mod_ref[0]
    shift2, scale2, gate2 = mod[3:4], mod[4:5], mod[5:6]
    x = x_ref[0]
    h = _rmsnorm_mod(x, n2w_ref[...], shift2, scale2).astype(BF16)
    g = _dot(h, wg_ref[...])
    u = _dot(h, wu_ref[...])
    act = (_silu(g) * u).astype(BF16)
    x2 = x + gate2 * _dot(act, wd_ref[...])
    ms = jnp.mean(x2 * x2, axis=-1, keepdims=True)
    o_ref[0] = x2 * lax.rsqrt(ms + EPS) * nfw_ref[...]


def _ffn(x1, mod3, n2w, wg, wu, wd, nfw):
    b, s, d = x1.shape
    tm = FFN_TILE
    return pl.pallas_call(
        _ffn_kernel,
        grid=(b, s // tm),
        in_specs=[
            pl.BlockSpec((1, tm, d), lambda bb, i: (bb, i, 0)),
            pl.BlockSpec((1, N_MOD, d), lambda bb, i: (bb, 0, 0)),
            _const_spec(n2w.shape), _const_spec(wg.shape), _const_spec(wu.shape),
            _const_spec(wd.shape), _const_spec(nfw.shape),
        ],
        out_specs=pl.BlockSpec((1, tm, d), lambda bb, i: (bb, i, 0)),
        out_shape=jax.ShapeDtypeStruct((b, s, d), F32),
        compiler_params=pltpu.CompilerParams(
            dimension_semantics=("arbitrary", "arbitrary"), vmem_limit_bytes=VMEM_LIMIT_BYTES),
        name="ffn",
    )(x1, mod3, n2w, wg, wu, wd, nfw)


def _head_rows(v):
    return jnp.broadcast_to(v.astype(F32)[:, None], (SSM_HEADS, LANES))


def _column_pieces(w):
    k, n = w.shape
    return w.reshape(k, n // MXU_WIDTH, MXU_WIDTH).transpose(1, 0, 2)


def kernel(x, c, positions, w_ada, b_ada, norm1_w, w_in, conv_w, conv_b, dt_bias, a_log, d_skip,
           attn_sinks, ssm_norm_w, w_out, norm2_w, w_gate_up, w_down, final_norm_w):
    b, s, d = x.shape
    depth = w_ada.shape[0]
    half = HEAD_DIM // 2
    inv_freq = ROPE_THETA ** (-jnp.arange(half, dtype=F32) / half)
    invf = jnp.tile(inv_freq, LANES // half).reshape(1, LANES)
    scan = _scan_matrix()
    expand = _expand_matrix()
    ts = MIXER_TILE
    pos4 = positions.reshape(b, s // ts, ROPE_PACK, ts // ROPE_PACK).transpose(0, 1, 3, 2)
    pos4 = jnp.repeat(pos4, half, axis=-1)
    o1 = ATTN_WIDTH + 2 * KV_WIDTH
    o2 = o1 + SSM_WIDTH
    o3 = o2 + CONV_CH
    assert depth == 1, "the final norm is fused into the (single) layer's ffn call"
    layer = 0
    mod3 = _adaln_mod(c, w_ada[layer], b_ada[layer]).reshape(b, N_MOD, d)
    w_in_l = w_in[layer].astype(BF16)
    w_cat = jnp.concatenate([w_in_l[:, o2:o3], w_in_l[:, :o1], w_in_l[:, o1:o2]], axis=1)
    assert w_cat.shape[1] == N_IN_PIECES * MXU_WIDTH
    wdt = jnp.pad(w_in_l[:, o3:].T, ((0, 2 * SUBLANES - SSM_HEADS), (0, 0)))
    x1 = _mixer(
        x, pos4, mod3, norm1_w[layer].reshape(1, d), invf, _column_pieces(w_cat), wdt,
        conv_w[layer], conv_b[layer].reshape(1, CONV_CH),
        _head_rows(dt_bias[layer]), _head_rows(a_log[layer]),
        jnp.repeat(d_skip[layer].astype(F32), SSM_HEAD_DIM).reshape(1, SSM_WIDTH),
        attn_sinks[layer].astype(F32),
        ssm_norm_w[layer].reshape(1, SSM_WIDTH), _column_pieces(w_out[layer].astype(BF16)),
        scan, expand)
    wgu = w_gate_up[layer].astype(BF16)
    return _ffn(x1, mod3, norm2_w[layer].reshape(1, d), wgu[:, :D_FF], wgu[:, D_FF:],
                w_down[layer].astype(BF16), final_norm_w.reshape(1, d))
```

```python
import math

import jax
import jax.numpy as jnp
import numpy as np
from jax import lax
from jax.experimental import pallas as pl
from jax.experimental.pallas import tpu as pltpu

D_MODEL = 1024
HEAD_DIM = 64
N_Q_HEADS = 8
N_KV_HEADS = 2
Q_PER_KV = N_Q_HEADS // N_KV_HEADS
ATTN_WIDTH = N_Q_HEADS * HEAD_DIM
KV_WIDTH = N_KV_HEADS * HEAD_DIM
WINDOW = 128
ROPE_THETA = 10000.0
SSM_HEADS = 8
SSM_HEAD_DIM = 64
SSM_WIDTH = SSM_HEADS * SSM_HEAD_DIM
SSM_GROUPS = 2
HEADS_PER_GROUP = SSM_HEADS // SSM_GROUPS
GROUP_WIDTH = HEADS_PER_GROUP * SSM_HEAD_DIM
D_STATE = 128
CONV_WIDTH = 4
CHUNK = 128
CONV_CH = SSM_WIDTH + 2 * SSM_GROUPS * D_STATE
MIX_WIDTH = ATTN_WIDTH + SSM_WIDTH
D_FF = 2816
N_MOD = 6
EPS = 1e-6

LANES = 128
SUBLANES = 8
MXU_WIDTH = 256
HEAD_SLOT = 8
ROPE_PACK = LANES // (HEAD_DIM // 2)
MASK_VALUE = -1e30
VMEM_LIMIT_BYTES = 56 * 1024 * 1024

MIXER_TILE = 512
FFN_TILE = 512
MOD_TILE = 1024

F32 = jnp.float32
BF16 = jnp.bfloat16

IN_XBC = 0
IN_Q = CONV_CH // MXU_WIDTH
IN_KV = IN_Q + ATTN_WIDTH // MXU_WIDTH
IN_Z = IN_KV + 2 * KV_WIDTH // MXU_WIDTH
N_IN_PIECES = IN_Z + SSM_WIDTH // MXU_WIDTH
N_OUT_PIECES = D_MODEL // MXU_WIDTH
N_AHEAD_PIECES = IN_Q
FILL_SPLIT = 1
assert GROUP_WIDTH == MXU_WIDTH

ATTN_STAGES = 4
SSD_STAGES = 7
_SSD_NEEDS_PREV_PAST = {2: 4, 4: 6}


def _stage_order(n_chunks):
    prog = {(k, name): 0 for k in range(n_chunks) for name in ("a0", "a1", "s")}
    order = []
    n_stages = {"a0": ATTN_STAGES, "a1": ATTN_STAGES, "s": SSD_STAGES}
    while any(v < n_stages[name] for (_, name), v in prog.items()):
        for k in range(n_chunks):
            for name in ("a0", "a1"):
                if prog[k, name] < ATTN_STAGES:
                    prog[k, name] += 1
                    order.append((k, name))
        for k in range(n_chunks):
            stage = prog[k, "s"]
            need = _SSD_NEEDS_PREV_PAST.get(stage) if k else None
            if stage < SSD_STAGES and (need is None or prog[k - 1, "s"] >= need):
                prog[k, "s"] += 1
                order.append((k, "s"))
    assert all(v == (SSD_STAGES if name == "s" else ATTN_STAGES) for (_, name), v in prog.items())
    return order

_EXPAND_ITEMS = (("acs", 3, LANES), ("ea", 2, SSM_HEAD_DIM), ("w", 2, SSM_HEAD_DIM),
                 ("dt", 2, SSM_HEAD_DIM))


def _expand_layout():
    slot = 0
    col = 0
    layout = {}
    for name, parts, rep in _EXPAND_ITEMS:
        layout[name] = (slot, parts, col, rep * SSM_HEADS)
        slot += parts
        col += rep * SSM_HEADS
    return layout, slot, col


def _expand_matrix():
    layout, nslots, ncols = _expand_layout()
    assert nslots * HEAD_SLOT <= LANES
    e = np.zeros((LANES, ncols), np.float32)
    for name, parts, rep in _EXPAND_ITEMS:
        slot0, _, col0, _ = layout[name]
        for p in range(parts):
            for hh in range(SSM_HEADS):
                e[(slot0 + p) * HEAD_SLOT + hh, col0 + hh * rep:col0 + (hh + 1) * rep] = 1.0
    return jnp.asarray(e, BF16)


def _scan_matrix():
    t = np.arange(CHUNK)
    upper = (t[:, None] <= t[None, :]).astype(np.float32)
    return jnp.asarray(np.concatenate([upper, np.ones((CHUNK, CHUNK), np.float32)], axis=1), BF16)


def _silu(x):
    hx = 0.5 * x
    return hx + hx * jnp.tanh(hx)


def _softplus(x):
    return jnp.maximum(x, 0.0) + jnp.log1p(jnp.exp(-jnp.abs(x)))


def _dot(a, b):
    return jnp.dot(a, b, preferred_element_type=F32)


def _dot_nt(a, b):
    return lax.dot_general(a, b, (((1,), (1,)), ((), ())), preferred_element_type=F32)


def _dot_tn(a, b):
    return lax.dot_general(a, b, (((0,), (0,)), ((), ())), preferred_element_type=F32)


def _rmsnorm_mod(x, norm_w, shift, scale):
    ms = jnp.mean(x * x, axis=-1, keepdims=True)
    return x * lax.rsqrt(ms + EPS) * (norm_w * (1.0 + scale)) + shift


def _split_bf16(v, parts):
    out = []
    r = v
    for _ in range(parts):
        p = r.astype(BF16).astype(F32)
        out.append(p)
        r = r - p
    return out


def _mod_kernel(c_ref, w_ref, b_ref, o_ref):
    sc = _silu(c_ref[...]).astype(BF16)
    o_ref[...] = _dot(sc, w_ref[...].astype(BF16)) + b_ref[...]


def _adaln_mod(c, w_ada, b_ada):
    b, d = c.shape
    n = w_ada.shape[1]
    return pl.pallas_call(
        _mod_kernel,
        grid=(n // MOD_TILE,),
        in_specs=[
            pl.BlockSpec((b, d), lambda j: (0, 0)),
            pl.BlockSpec((d, MOD_TILE), lambda j: (0, j)),
            pl.BlockSpec((1, MOD_TILE), lambda j: (0, j)),
        ],
        out_specs=pl.BlockSpec((b, MOD_TILE), lambda j: (0, j)),
        out_shape=jax.ShapeDtypeStruct((b, n), F32),
        compiler_params=pltpu.CompilerParams(
            dimension_semantics=("arbitrary",), vmem_limit_bytes=VMEM_LIMIT_BYTES),
        name="adaln_mod",
    )(c, w_ada, b_ada.reshape(1, n))


def _mixer_kernel(tiles_per_row,
                  xn_ref, xp_ref, pos_ref, modn_ref, modp_ref, n1w_ref, invf_ref, win_ref, wdt_ref,
                  convw_ref, convb_ref, dtb_ref, alog_ref, dskip_ref, sinks_ref,
                  snw_ref, wout_ref, scan_ref, expand_ref,
                  o_ref,
                  raw_scr, h_scr, y_scr, q_scr, k_scr, v_scr, xbc_scr, dt_scr, state_scr,
                  mix_scr):
    ts = xn_ref.shape[1]
    n_chunks = ts // CHUNK
    s = pl.program_id(0)
    i = lax.rem(s, tiles_per_row)
    cur = lax.rem(s, 2)
    layout, n_slots, _ = _expand_layout()

    def norm_next():
        modn = modn_ref[0]
        return _rmsnorm_mod(xn_ref[0], n1w_ref[...], modn[0:1], modn[1:2]).astype(BF16)

    @pl.when(s == 0)
    def _():
        modp = modp_ref[0]
        h0 = _rmsnorm_mod(xp_ref[0], n1w_ref[...], modp[0:1], modp[1:2]).astype(BF16)
        h_scr[...] = h0
        for j in range(N_AHEAD_PIECES):
            raw_scr[j] = _dot(h0, win_ref[j])
        y_scr[...] = jnp.zeros(y_scr.shape, F32)
        mix_scr[...] = jnp.zeros(mix_scr.shape, BF16)

    @pl.when(i == 0)
    def _():
        k_scr[0:CHUNK, :] = jnp.zeros((CHUNK, KV_WIDTH), BF16)
        v_scr[0:CHUNK, :] = jnp.zeros((CHUNK, KV_WIDTH), BF16)
        xbc_scr[0:SUBLANES, :] = jnp.zeros((SUBLANES, CONV_CH), F32)
        state_scr[...] = jnp.zeros(state_scr.shape, F32)

    for j in range(IN_Q):
        xbc_scr[SUBLANES:SUBLANES + ts, j * MXU_WIDTH:(j + 1) * MXU_WIDTH] = raw_scr[IN_XBC + j]
    h = h_scr[...]
    pieces = {j: _dot(h, win_ref[j]) for j in range(N_AHEAD_PIECES, N_IN_PIECES)}

    def proj(j):
        return pieces[j]

    half = HEAD_DIM // 2
    ang_d = pos_ref[0, 0].astype(F32) * invf_ref[...]
    lane_d = lax.broadcasted_iota(jnp.int32, ang_d.shape, 1)

    def spread(tbl):
        blocks = []
        for k in range(ROPE_PACK):
            t = pltpu.roll(tbl, LANES - half * k, axis=1) if k else tbl
            t = jnp.where(lane_d < half, t, pltpu.roll(t, half, axis=1))
            t = jnp.where(lane_d < 2 * half, t, pltpu.roll(t, 2 * half, axis=1))
            blocks.append(t)
        return jnp.concatenate(blocks, axis=0)

    cos = spread(jnp.cos(ang_d))
    sin = spread(jnp.sin(ang_d))
    lane = lax.broadcasted_iota(jnp.int32, (ts, LANES), 1)
    first_half = (lane % HEAD_DIM) < half
    sin_signed = jnp.where(first_half, -sin, sin)

    def rope(t, rows):
        lane_c = lax.broadcasted_iota(jnp.int32, t.shape, 1)
        rot = jnp.where((lane_c % HEAD_DIM) < half, pltpu.roll(t, LANES - half, axis=1),
                        pltpu.roll(t, half, axis=1))
        return t * cos[rows] + rot * sin_signed[rows]

    q_scale = 1.0 / math.sqrt(HEAD_DIM)

    def rope_rows(c):
        rows = slice(c * CHUNK, (c + 1) * CHUNK)
        for j in range(ATTN_WIDTH // LANES):
            piece, off = divmod(j * LANES, MXU_WIDTH)
            q_scr[rows, j * LANES:(j + 1) * LANES] = (
                rope(proj(IN_Q + piece)[rows, off:off + LANES], rows) * q_scale).astype(BF16)
        krows = slice(CHUNK + c * CHUNK, CHUNK + (c + 1) * CHUNK)
        k_scr[krows, :] = rope(proj(IN_KV)[rows, 0:KV_WIDTH], rows).astype(BF16)
        v_scr[krows, :] = proj(IN_KV)[rows, KV_WIDTH:2 * KV_WIDTH].astype(BF16)

    dt_scr[...] = _dot_nt(wdt_ref[...], h)

    h_scr[...] = norm_next()

    a_neg = -jnp.exp(alog_ref[...])
    causal = (lax.broadcasted_iota(jnp.int32, (CHUNK, CHUNK), 0)
              >= lax.broadcasted_iota(jnp.int32, (CHUNK, CHUNK), 1))
    x_cols = slice(0, SSM_WIDTH)
    bc_cols = slice(SSM_WIDTH, CONV_CH)
    group_head = lax.broadcasted_iota(jnp.int32, (CHUNK, GROUP_WIDTH), 1) // SSM_HEAD_DIM

    def conv_silu(row0, cols):
        win = xbc_scr[pl.ds(row0, SUBLANES + CHUNK), cols]
        conv = convb_ref[:, cols]
        for t in range(CONV_WIDTH):
            back = CONV_WIDTH - 1 - t
            tap = pltpu.roll(win, back, axis=0) if back else win
            conv = conv + convw_ref[t:t + 1, cols] * tap[SUBLANES:SUBLANES + CHUNK]
        return _silu(conv)

    n_ql = Q_PER_KV * CHUNK
    key_idx = lax.broadcasted_iota(jnp.int32, (CHUNK, n_ql), 0)
    qry_idx = lax.broadcasted_iota(jnp.int32, (CHUNK, n_ql), 1) % CHUNK
    upper = key_idx > qry_idx
    lane_head = lax.broadcasted_iota(jnp.int32, (1, n_ql), 1) // CHUNK

    def chunk_streams(c, tick):
        r0 = c * CHUNK
        prev_bias = jnp.where((i == 0) & (c == 0), MASK_VALUE, 0.0).astype(F32)

        def attn_stages(g):
            if g == 0:
                rope_rows(c)
            kband = k_scr[pl.ds(r0, 2 * CHUNK), :]
            vband = v_scr[pl.ds(r0, 2 * CHUNK), :]
            qblk = q_scr[pl.ds(r0, CHUNK), :]
            qg = jnp.concatenate(
                [qblk[:, (g * Q_PER_KV + j) * HEAD_DIM:(g * Q_PER_KV + j + 1) * HEAD_DIM]
                 for j in range(Q_PER_KV)], axis=0)
            s_t = _dot_nt(kband[:, g * HEAD_DIM:(g + 1) * HEAD_DIM], qg)
            yield
            sf = jnp.where(upper, s_t[:CHUNK] + prev_bias, s_t[CHUNK:])
            sink = jnp.zeros((1, n_ql), F32)
            for j in range(Q_PER_KV):
                sink = jnp.where(lane_head == j, sinks_ref[g * Q_PER_KV + j], sink)
            m = jnp.maximum(jnp.max(sf, axis=0, keepdims=True), sink)
            pf = jnp.exp(sf - m)
            denom = jnp.sum(pf, axis=0, keepdims=True) + jnp.exp(sink - m)
            p_t = jnp.concatenate([jnp.where(upper, pf, 0.0), jnp.where(upper, 0.0, pf)],
                                  axis=0).astype(BF16)
            yield
            tick()
            o_t = _dot_tn(vband[:, g * HEAD_DIM:(g + 1) * HEAD_DIM], p_t)
            o_t = o_t * (1.0 / denom)
            yield
            tiles = []
            for pp in range(Q_PER_KV // 2):
                blk = jnp.concatenate([o_t[:, (2 * pp) * CHUNK:(2 * pp + 1) * CHUNK],
                                       o_t[:, (2 * pp + 1) * CHUNK:(2 * pp + 2) * CHUNK]], axis=0)
                tiles.append(blk.T)
            mix_scr[cur, pl.ds(r0, CHUNK),
                    g * Q_PER_KV * HEAD_DIM:(g + 1) * Q_PER_KV * HEAD_DIM] = (
                jnp.concatenate(tiles, axis=1).astype(BF16))

        def ssd_stages():
            xs = conv_silu(r0, x_cols)
            bc = conv_silu(r0, bc_cols)
            yield
            dt = _softplus(dt_scr[0:SSM_HEADS, pl.ds(r0, CHUNK)] + dtb_ref[...])
            a = dt * a_neg
            pad = jnp.zeros((SUBLANES, CHUNK), F32)
            cs = _dot(jnp.concatenate(_split_bf16(a, 3) + [pad], axis=0).astype(BF16),
                      scan_ref[...])
            cs = cs[0:8] + cs[8:16] + cs[16:24]
            a_cs = cs[:, :CHUNK]
            a_tot = cs[:, CHUNK:]
            ea = jnp.exp(a_cs)
            w_end = dt * jnp.exp(a_tot - a_cs)
            rows = (_split_bf16(a_cs, 3) + _split_bf16(ea, 2) + _split_bf16(w_end, 2)
                    + _split_bf16(dt, 2))
            assert len(rows) == n_slots
            rows.append(jnp.zeros((LANES - n_slots * HEAD_SLOT, CHUNK), F32))
            packed = jnp.concatenate(rows, axis=0).T.astype(BF16)
            ex = _dot(packed, expand_ref[...])

            def ex_block(name):
                _, _, col0, width = layout[name]
                return ex[:, col0:col0 + width]

            acs_b = ex_block("acs")
            ea_e = ex_block("ea")
            xdt = (xs * ex_block("dt")).astype(BF16)
            xw = (xs * ex_block("w")).astype(BF16)
            yield
            ys = []
            for g in range(SSM_GROUPS):
                gs = slice(g * GROUP_WIDTH, (g + 1) * GROUP_WIDTH)
                bg_t = bc[:, g * D_STATE:(g + 1) * D_STATE].T.astype(BF16)
                cg = bc[:, (SSM_GROUPS + g) * D_STATE:(SSM_GROUPS + g + 1) * D_STATE].astype(BF16)
                cb = _dot(cg, bg_t)
                s_old = state_scr[g]
                y_off = _dot(cg, s_old.astype(BF16)) * ea_e[:, gs]
                ms = []
                xm = []
                for j in range(HEADS_PER_GROUP):
                    hh = g * HEADS_PER_GROUP + j
                    seg = acs_b[:, hh * LANES:(hh + 1) * LANES] - a_cs[hh:hh + 1, :]
                    dec = jnp.exp(jnp.where(causal, seg, MASK_VALUE))
                    ms.append((cb * dec).astype(BF16))
                    xm.append(jnp.where(group_head == j, xdt[:, gs], jnp.zeros((), BF16)))
                tick()
                y_diag = _dot(jnp.concatenate(ms, axis=1), jnp.concatenate(xm, axis=0))
                yield
                y_g = y_diag + y_off + xs[:, gs] * dskip_ref[:, gs]
                state_scr[g] = s_old * ea_e[CHUNK - 1:CHUNK, gs] + _dot(bg_t, xw[:, gs])
                y_g = y_g * _silu(proj(IN_Z + g)[r0:r0 + CHUNK, :])
                y_g = y_g * lax.rsqrt(jnp.mean(y_g * y_g, axis=-1, keepdims=True) + EPS)
                ys.append(y_g * snw_ref[:, gs])
                yield
            mix_scr[cur, pl.ds(r0, CHUNK), ATTN_WIDTH:] = jnp.concatenate(ys, axis=1).astype(BF16)

        return {"a0": attn_stages(0), "a1": attn_stages(1), "s": ssd_stages()}

    def filler_units():
        for j in range(max(N_AHEAD_PIECES, N_OUT_PIECES)):
            for rr in range(FILL_SPLIT):
                rows = slice(rr * ts // FILL_SPLIT, (rr + 1) * ts // FILL_SPLIT)
                if j < N_AHEAD_PIECES:
                    raw_scr[j, rows, :] = _dot(h_scr[rows, :], win_ref[j])
                    yield
                if j < N_OUT_PIECES:
                    y_scr[j, rows, :] = _dot(mix_scr[1 - cur, rows, :], wout_ref[j])
                    yield

    fill = filler_units()
    def tick(units=1):
        for _ in range(units):
            next(fill, None)
    streams = {}
    for c in range(n_chunks):
        for name, gen in chunk_streams(c, tick).items():
            streams[c, name] = gen
    for key in _stage_order(n_chunks):
        next(streams[key], None)
    for stream in streams.values():
        assert next(stream, "done") == "done"
    for _ in fill:
        pass

    k_scr[0:CHUNK, :] = k_scr[ts:ts + CHUNK, :]
    v_scr[0:CHUNK, :] = v_scr[ts:ts + CHUNK, :]
    xbc_scr[0:SUBLANES, :] = xbc_scr[ts:ts + SUBLANES, :]

    gate1 = modp_ref[0][2:3]
    for j in range(N_OUT_PIECES):
        cols = slice(j * MXU_WIDTH, (j + 1) * MXU_WIDTH)
        o_ref[0, :, cols] = xp_ref[0, :, cols] + gate1[:, cols] * y_scr[j]


def _const_spec(shape):
    nd = len(shape)
    return pl.BlockSpec(shape, lambda *_: (0,) * nd, pipeline_mode=pl.Buffered(1))


def _mixer(x, pos4, mod3, n1w, invf, win, wdt, convw, convb, dtb, alog, dskip, sinks, snw, wout, scan,
           expand):
    b, s, d = x.shape
    ts = MIXER_TILE
    assert ts % CHUNK == 0 and s % ts == 0
    tpr = s // ts
    n_tiles = b * tpr

    def nxt(step):
        t = jnp.minimum(step + 1, n_tiles - 1)
        return t // tpr, t % tpr

    def prv(step):
        t = jnp.maximum(step - 1, 0)
        return t // tpr, t % tpr

    def cur(step):
        t = jnp.minimum(step, n_tiles - 1)
        return t // tpr, t % tpr

    in_specs = [
        pl.BlockSpec((1, ts, d), lambda st: (*nxt(st), 0)),
        pl.BlockSpec((1, ts, d), lambda st: (*prv(st), 0)),
        pl.BlockSpec((1, 1, ts // ROPE_PACK, LANES), lambda st: (*cur(st), 0, 0)),
        pl.BlockSpec((1, N_MOD, d), lambda st: (nxt(st)[0], 0, 0)),
        pl.BlockSpec((1, N_MOD, d), lambda st: (prv(st)[0], 0, 0)),
        _const_spec(n1w.shape), _const_spec(invf.shape), _const_spec(win.shape),
        _const_spec(wdt.shape),
        _const_spec(convw.shape), _const_spec(convb.shape), _const_spec(dtb.shape),
        _const_spec(alog.shape), _const_spec(dskip.shape),
        pl.BlockSpec(memory_space=pltpu.SMEM),
        _const_spec(snw.shape), _const_spec(wout.shape), _const_spec(scan.shape),
        _const_spec(expand.shape),
    ]
    scratch = [
        pltpu.VMEM((N_AHEAD_PIECES, ts, MXU_WIDTH), F32),
        pltpu.VMEM((ts, d), BF16),
        pltpu.VMEM((N_OUT_PIECES, ts, MXU_WIDTH), F32),
        pltpu.VMEM((ts, ATTN_WIDTH), BF16),
        pltpu.VMEM((CHUNK + ts, KV_WIDTH), BF16),
        pltpu.VMEM((CHUNK + ts, KV_WIDTH), BF16),
        pltpu.VMEM((SUBLANES + ts, CONV_CH), F32),
        pltpu.VMEM((2 * SUBLANES, ts), F32),
        pltpu.VMEM((SSM_GROUPS, D_STATE, GROUP_WIDTH), F32),
        pltpu.VMEM((2, ts, MIX_WIDTH), BF16),
    ]
    return pl.pallas_call(
        lambda *refs: _mixer_kernel(tpr, *refs),
        grid=(n_tiles + 1,),
        in_specs=in_specs,
        out_specs=pl.BlockSpec((1, ts, d), lambda st: (*prv(st), 0)),
        out_shape=jax.ShapeDtypeStruct((b, s, d), F32),
        scratch_shapes=scratch,
        compiler_params=pltpu.CompilerParams(
            dimension_semantics=("arbitrary",), vmem_limit_bytes=VMEM_LIMIT_BYTES),
        name="mixer",
    )(x, x, pos4, mod3, mod3, n1w, invf, win, wdt, convw, convb, dtb, alog, dskip, sinks, snw, wout,
      scan, expand)


def _ffn_kernel(x_ref, mod_ref, n2w_ref, wg_ref, wu_ref, wd_ref, nfw_ref, o_ref):
    mod = mod_ref[0]
    shift2, scale2, gate2 = mod[3:4], mod[4:5], mod[5:6]
    x = x_ref[0]
    h = _rmsnorm_mod(x, n2w_ref[...], shift2, scale2).astype(BF16)
    g = _dot(h, wg_ref[...])
    u = _dot(h, wu_ref[...])
    act = (_silu(g) * u).astype(BF16)
    x2 = x + gate2 * _dot(act, wd_ref[...])
    ms = jnp.mean(x2 * x2, axis=-1, keepdims=True)
    o_ref[0] = x2 * lax.rsqrt(ms + EPS) * nfw_ref[...]


def _ffn(x1, mod3, n2w, wg, wu, wd, nfw):
    b, s, d = x1.shape
    tm = FFN_TILE
    return pl.pallas_call(
        _ffn_kernel,
        grid=(b, s // tm),
        in_specs=[
            pl.BlockSpec((1, tm, d), lambda bb, i: (bb, i, 0)),
            pl.BlockSpec((1, N_MOD, d), lambda bb, i: (bb, 0, 0)),
            _const_spec(n2w.shape), _const_spec(wg.shape), _const_spec(wu.shape),
            _const_spec(wd.shape), _const_spec(nfw.shape),
        ],
        out_specs=pl.BlockSpec((1, tm, d), lambda bb, i: (bb, i, 0)),
        out_shape=jax.ShapeDtypeStruct((b, s, d), F32),
        compiler_params=pltpu.CompilerParams(
            dimension_semantics=("arbitrary", "arbitrary"), vmem_limit_bytes=VMEM_LIMIT_BYTES),
        name="ffn",
    )(x1, mod3, n2w, wg, wu, wd, nfw)


def _head_rows(v):
    return jnp.broadcast_to(v.astype(F32)[:, None], (SSM_HEADS, LANES))


def _column_pieces(w):
    k, n = w.shape
    return w.reshape(k, n // MXU_WIDTH, MXU_WIDTH).transpose(1, 0, 2)


def kernel(x, c, positions, w_ada, b_ada, norm1_w, w_in, conv_w, conv_b, dt_bias, a_log, d_skip,
           attn_sinks, ssm_norm_w, w_out, norm2_w, w_gate_up, w_down, final_norm_w):
    b, s, d = x.shape
    depth = w_ada.shape[0]
    half = HEAD_DIM // 2
    inv_freq = ROPE_THETA ** (-jnp.arange(half, dtype=F32) / half)
    invf = jnp.tile(inv_freq, LANES // half).reshape(1, LANES)
    scan = _scan_matrix()
    expand = _expand_matrix()
    ts = MIXER_TILE
    pos4 = positions.reshape(b, s // ts, ROPE_PACK, ts // ROPE_PACK).transpose(0, 1, 3, 2)
    pos4 = jnp.repeat(pos4, half, axis=-1)
    o1 = ATTN_WIDTH + 2 * KV_WIDTH
    o2 = o1 + SSM_WIDTH
    o3 = o2 + CONV_CH
    assert depth == 1, "the final norm is fused into the (single) layer's ffn call"
    layer = 0
    mod3 = _adaln_mod(c, w_ada[layer], b_ada[layer]).reshape(b, N_MOD, d)
    w_in_l = w_in[layer].astype(BF16)
    w_cat = jnp.concatenate([w_in_l[:, o2:o3], w_in_l[:, :o1], w_in_l[:, o1:o2]], axis=1)
    assert w_cat.shape[1] == N_IN_PIECES * MXU_WIDTH
    wdt = jnp.pad(w_in_l[:, o3:].T, ((0, 2 * SUBLANES - SSM_HEADS), (0, 0)))
    x1 = _mixer(
        x, pos4, mod3, norm1_w[layer].reshape(1, d), invf, _column_pieces(w_cat), wdt,
        conv_w[layer], conv_b[layer].reshape(1, CONV_CH),
        _head_rows(dt_bias[layer]), _head_rows(a_log[layer]),
        jnp.repeat(d_skip[layer].astype(F32), SSM_HEAD_DIM).reshape(1, SSM_WIDTH),
        attn_sinks[layer].astype(F32),
        ssm_norm_w[layer].reshape(1, SSM_WIDTH), _column_pieces(w_out[layer].astype(BF16)),
        scan, expand)
    wgu = w_gate_up[layer].astype(BF16)
    return _ffn(x1, mod3, norm2_w[layer].reshape(1, d), wgu[:, :D_FF], wgu[:, D_FF:],
                w_down[layer].astype(BF16), final_norm_w.reshape(1, d))
```

```python
import math

import jax
import jax.numpy as jnp
import numpy as np
from jax import lax
from jax.experimental import pallas as pl
from jax.experimental.pallas import tpu as pltpu

D_MODEL = 1024
HEAD_DIM = 64
N_Q_HEADS = 8
N_KV_HEADS = 2
Q_PER_KV = N_Q_HEADS // N_KV_HEADS
ATTN_WIDTH = N_Q_HEADS * HEAD_DIM
KV_WIDTH = N_KV_HEADS * HEAD_DIM
WINDOW = 128
ROPE_THETA = 10000.0
SSM_HEADS = 8
SSM_HEAD_DIM = 64
SSM_WIDTH = SSM_HEADS * SSM_HEAD_DIM
SSM_GROUPS = 2
HEADS_PER_GROUP = SSM_HEADS // SSM_GROUPS
GROUP_WIDTH = HEADS_PER_GROUP * SSM_HEAD_DIM
D_STATE = 128
CONV_WIDTH = 4
CHUNK = 128
CONV_CH = SSM_WIDTH + 2 * SSM_GROUPS * D_STATE
MIX_WIDTH = ATTN_WIDTH + SSM_WIDTH
D_FF = 2816
N_MOD = 6
EPS = 1e-6

LANES = 128
SUBLANES = 8
MXU_WIDTH = 256
HEAD_SLOT = 8
ROPE_PACK = LANES // (HEAD_DIM // 2)
MASK_VALUE = -1e30
VMEM_LIMIT_BYTES = 56 * 1024 * 1024

MIXER_TILE = 512
FFN_TILE = 512
MOD_TILE = 1024

F32 = jnp.float32
BF16 = jnp.bfloat16

IN_XBC = 0
IN_Q = CONV_CH // MXU_WIDTH
IN_KV = IN_Q + ATTN_WIDTH // MXU_WIDTH
IN_Z = IN_KV + 2 * KV_WIDTH // MXU_WIDTH
N_IN_PIECES = IN_Z + SSM_WIDTH // MXU_WIDTH
N_OUT_PIECES = D_MODEL // MXU_WIDTH
N_AHEAD_PIECES = IN_Q
FILL_SPLIT = 1
assert GROUP_WIDTH == MXU_WIDTH

ATTN_STAGES = 4
SSD_STAGES = 7
_SSD_NEEDS_PREV_PAST = {2: 4, 4: 6}


def _stage_order(n_chunks):
    prog = {(k, name): 0 for k in range(n_chunks) for name in ("a0", "a1", "s")}
    order = []
    n_stages = {"a0": ATTN_STAGES, "a1": ATTN_STAGES, "s": SSD_STAGES}
    while any(v < n_stages[name] for (_, name), v in prog.items()):
        for k in range(n_chunks):
            for name in ("a0", "a1"):
                if prog[k, name] < ATTN_STAGES:
                    prog[k, name] += 1
                    order.append((k, name))
        for k in range(n_chunks):
            stage = prog[k, "s"]
            need = _SSD_NEEDS_PREV_PAST.get(stage) if k else None
            if stage < SSD_STAGES and (need is None or prog[k - 1, "s"] >= need):
                prog[k, "s"] += 1
                order.append((k, "s"))
    assert all(v == (SSD_STAGES if name == "s" else ATTN_STAGES) for (_, name), v in prog.items())
    return order

_EXPAND_ITEMS = (("acs", 3, LANES), ("ea", 2, SSM_HEAD_DIM), ("w", 2, SSM_HEAD_DIM),
                 ("dt", 2, SSM_HEAD_DIM))


def _expand_layout():
    slot = 0
    col = 0
    layout = {}
    for name, parts, rep in _EXPAND_ITEMS:
        layout[name] = (slot, parts, col, rep * SSM_HEADS)
        slot += parts
        col += rep * SSM_HEADS
    return layout, slot, col


def _expand_matrix():
    layout, nslots, ncols = _expand_layout()
    assert nslots * HEAD_SLOT <= LANES
    e = np.zeros((LANES, ncols), np.float32)
    for name, parts, rep in _EXPAND_ITEMS:
        slot0, _, col0, _ = layout[name]
        for p in range(parts):
            for hh in range(SSM_HEADS):
                e[(slot0 + p) * HEAD_SLOT + hh, col0 + hh * rep:col0 + (hh + 1) * rep] = 1.0
    return jnp.asarray(e, BF16)


def _scan_matrix():
    t = np.arange(CHUNK)
    upper = (t[:, None] <= t[None, :]).astype(np.float32)
    return jnp.asarray(np.concatenate([upper, np.ones((CHUNK, CHUNK), np.float32)], axis=1), BF16)


def _silu(x):
    hx = 0.5 * x
    return hx + hx * jnp.tanh(hx)


def _softplus(x):
    return jnp.maximum(x, 0.0) + jnp.log1p(jnp.exp(-jnp.abs(x)))


def _dot(a, b):
    return jnp.dot(a, b, preferred_element_type=F32)


def _dot_nt(a, b):
    return lax.dot_general(a, b, (((1,), (1,)), ((), ())), preferred_element_type=F32)


def _dot_tn(a, b):
    return lax.dot_general(a, b, (((0,), (0,)), ((), ())), preferred_element_type=F32)


def _rmsnorm_mod(x, norm_w, shift, scale):
    ms = jnp.mean(x * x, axis=-1, keepdims=True)
    return x * lax.rsqrt(ms + EPS) * (norm_w * (1.0 + scale)) + shift


def _piece(j):
    return slice(j * MXU_WIDTH, (j + 1) * MXU_WIDTH)


def _split_bf16(v, parts):
    out = []
    r = v
    for _ in range(parts):
        p = r.astype(BF16).astype(F32)
        out.append(p)
        r = r - p
    return out


def _mod_kernel(c_ref, w_ref, b_ref, o_ref):
    sc = _silu(c_ref[...]).astype(BF16)
    o_ref[...] = _dot(sc, w_ref[...].astype(BF16)) + b_ref[...]


def _adaln_mod(c, w_ada, b_ada):
    b, d = c.shape
    n = w_ada.shape[1]
    return pl.pallas_call(
        _mod_kernel,
        grid=(n // MOD_TILE,),
        in_specs=[
            pl.BlockSpec((b, d), lambda j: (0, 0)),
            pl.BlockSpec((d, MOD_TILE), lambda j: (0, j)),
            pl.BlockSpec((1, MOD_TILE), lambda j: (0, j)),
        ],
        out_specs=pl.BlockSpec((b, MOD_TILE), lambda j: (0, j)),
        out_shape=jax.ShapeDtypeStruct((b, n), F32),
        compiler_params=pltpu.CompilerParams(
            dimension_semantics=("arbitrary",), vmem_limit_bytes=VMEM_LIMIT_BYTES),
        name="adaln_mod",
    )(c, w_ada, b_ada.reshape(1, n))


def _mixer_kernel(tiles_per_row,
                  xn_ref, xp_ref, pos_ref, modn_ref, modp_ref, n1w_ref, invf_ref, win_ref, wdt_ref,
                  convw_ref, convb_ref, dtb_ref, alog_ref, dskip_ref, sinks_ref,
                  snw_ref, wout_ref, scan_ref, expand_ref,
                  o_ref,
                  raw_scr, h_scr, y_scr, q_scr, k_scr, v_scr, xbc_scr, dt_scr, state_scr,
                  mix_scr):
    ts = xn_ref.shape[1]
    n_chunks = ts // CHUNK
    s = pl.program_id(0)
    i = lax.rem(s, tiles_per_row)
    cur = lax.rem(s, 2)
    layout, n_slots, _ = _expand_layout()

    def norm_next():
        modn = modn_ref[0]
        return _rmsnorm_mod(xn_ref[0], n1w_ref[...], modn[0:1], modn[1:2]).astype(BF16)

    @pl.when(s == 0)
    def _():
        modp = modp_ref[0]
        h0 = _rmsnorm_mod(xp_ref[0], n1w_ref[...], modp[0:1], modp[1:2]).astype(BF16)
        h_scr[...] = h0
        for j in range(N_AHEAD_PIECES):
            raw_scr[j] = _dot(h0, win_ref[:, _piece(j)])
        y_scr[...] = jnp.zeros(y_scr.shape, F32)
        mix_scr[...] = jnp.zeros(mix_scr.shape, BF16)

    @pl.when(i == 0)
    def _():
        k_scr[0:CHUNK, :] = jnp.zeros((CHUNK, KV_WIDTH), BF16)
        v_scr[0:CHUNK, :] = jnp.zeros((CHUNK, KV_WIDTH), BF16)
        xbc_scr[0:SUBLANES, :] = jnp.zeros((SUBLANES, CONV_CH), F32)
        state_scr[...] = jnp.zeros(state_scr.shape, F32)

    for j in range(IN_Q):
        xbc_scr[SUBLANES:SUBLANES + ts, j * MXU_WIDTH:(j + 1) * MXU_WIDTH] = raw_scr[IN_XBC + j]
    h = h_scr[...]
    pieces = {j: _dot(h, win_ref[:, _piece(j)])
              for j in range(N_AHEAD_PIECES, N_IN_PIECES)}

    def proj(j):
        return pieces[j]

    half = HEAD_DIM // 2
    ang_d = pos_ref[0, 0].astype(F32) * invf_ref[...]
    lane_d = lax.broadcasted_iota(jnp.int32, ang_d.shape, 1)

    def spread(tbl):
        blocks = []
        for k in range(ROPE_PACK):
            t = pltpu.roll(tbl, LANES - half * k, axis=1) if k else tbl
            t = jnp.where(lane_d < half, t, pltpu.roll(t, half, axis=1))
            t = jnp.where(lane_d < 2 * half, t, pltpu.roll(t, 2 * half, axis=1))
            blocks.append(t)
        return jnp.concatenate(blocks, axis=0)

    cos = spread(jnp.cos(ang_d))
    sin = spread(jnp.sin(ang_d))
    lane = lax.broadcasted_iota(jnp.int32, (ts, LANES), 1)
    first_half = (lane % HEAD_DIM) < half
    sin_signed = jnp.where(first_half, -sin, sin)

    def rope(t, rows):
        lane_c = lax.broadcasted_iota(jnp.int32, t.shape, 1)
        rot = jnp.where((lane_c % HEAD_DIM) < half, pltpu.roll(t, LANES - half, axis=1),
                        pltpu.roll(t, half, axis=1))
        return t * cos[rows] + rot * sin_signed[rows]

    q_scale = 1.0 / math.sqrt(HEAD_DIM)

    def rope_rows(c):
        rows = slice(c * CHUNK, (c + 1) * CHUNK)
        for j in range(ATTN_WIDTH // LANES):
            piece, off = divmod(j * LANES, MXU_WIDTH)
            q_scr[rows, j * LANES:(j + 1) * LANES] = (
                rope(proj(IN_Q + piece)[rows, off:off + LANES], rows) * q_scale).astype(BF16)
        krows = slice(CHUNK + c * CHUNK, CHUNK + (c + 1) * CHUNK)
        k_scr[krows, :] = rope(proj(IN_KV)[rows, 0:KV_WIDTH], rows).astype(BF16)
        v_scr[krows, :] = proj(IN_KV)[rows, KV_WIDTH:2 * KV_WIDTH].astype(BF16)

    dt_scr[...] = _dot_nt(wdt_ref[...], h)

    h_scr[...] = norm_next()

    a_neg = -jnp.exp(alog_ref[...])
    causal = (lax.broadcasted_iota(jnp.int32, (CHUNK, CHUNK), 0)
              >= lax.broadcasted_iota(jnp.int32, (CHUNK, CHUNK), 1))
    x_cols = slice(0, SSM_WIDTH)
    bc_cols = slice(SSM_WIDTH, CONV_CH)
    group_head = lax.broadcasted_iota(jnp.int32, (CHUNK, GROUP_WIDTH), 1) // SSM_HEAD_DIM

    def conv_silu(row0, cols):
        win = xbc_scr[pl.ds(row0, SUBLANES + CHUNK), cols]
        conv = convb_ref[:, cols]
        for t in range(CONV_WIDTH):
            back = CONV_WIDTH - 1 - t
            tap = pltpu.roll(win, back, axis=0) if back else win
            conv = conv + convw_ref[t:t + 1, cols] * tap[SUBLANES:SUBLANES + CHUNK]
        return _silu(conv)

    n_ql = Q_PER_KV * CHUNK
    key_idx = lax.broadcasted_iota(jnp.int32, (CHUNK, n_ql), 0)
    qry_idx = lax.broadcasted_iota(jnp.int32, (CHUNK, n_ql), 1) % CHUNK
    upper = key_idx > qry_idx
    lane_head = lax.broadcasted_iota(jnp.int32, (1, n_ql), 1) // CHUNK

    def chunk_streams(c, tick):
        r0 = c * CHUNK
        prev_bias = jnp.where((i == 0) & (c == 0), MASK_VALUE, 0.0).astype(F32)

        def attn_stages(g):
            if g == 0:
                rope_rows(c)
            kband = k_scr[pl.ds(r0, 2 * CHUNK), :]
            vband = v_scr[pl.ds(r0, 2 * CHUNK), :]
            qblk = q_scr[pl.ds(r0, CHUNK), :]
            qg = jnp.concatenate(
                [qblk[:, (g * Q_PER_KV + j) * HEAD_DIM:(g * Q_PER_KV + j + 1) * HEAD_DIM]
                 for j in range(Q_PER_KV)], axis=0)
            s_t = _dot_nt(kband[:, g * HEAD_DIM:(g + 1) * HEAD_DIM], qg)
            yield
            sf = jnp.where(upper, s_t[:CHUNK] + prev_bias, s_t[CHUNK:])
            sink = jnp.zeros((1, n_ql), F32)
            for j in range(Q_PER_KV):
                sink = jnp.where(lane_head == j, sinks_ref[g * Q_PER_KV + j], sink)
            m = jnp.maximum(jnp.max(sf, axis=0, keepdims=True), sink)
            pf = jnp.exp(sf - m)
            denom = jnp.sum(pf, axis=0, keepdims=True) + jnp.exp(sink - m)
            p_t = jnp.concatenate([jnp.where(upper, pf, 0.0), jnp.where(upper, 0.0, pf)],
                                  axis=0).astype(BF16)
            yield
            tick()
            o_t = _dot_tn(vband[:, g * HEAD_DIM:(g + 1) * HEAD_DIM], p_t)
            o_t = o_t * (1.0 / denom)
            yield
            tiles = []
            for pp in range(Q_PER_KV // 2):
                blk = jnp.concatenate([o_t[:, (2 * pp) * CHUNK:(2 * pp + 1) * CHUNK],
                                       o_t[:, (2 * pp + 1) * CHUNK:(2 * pp + 2) * CHUNK]], axis=0)
                tiles.append(blk.T)
            mix_scr[cur, pl.ds(r0, CHUNK),
                    g * Q_PER_KV * HEAD_DIM:(g + 1) * Q_PER_KV * HEAD_DIM] = (
                jnp.concatenate(tiles, axis=1).astype(BF16))

        def ssd_stages():
            xs = conv_silu(r0, x_cols)
            bc = conv_silu(r0, bc_cols)
            yield
            dt = _softplus(dt_scr[0:SSM_HEADS, pl.ds(r0, CHUNK)] + dtb_ref[...])
            a = dt * a_neg
            pad = jnp.zeros((SUBLANES, CHUNK), F32)
            cs = _dot(jnp.concatenate(_split_bf16(a, 3) + [pad], axis=0).astype(BF16),
                      scan_ref[...])
            cs = cs[0:8] + cs[8:16] + cs[16:24]
            a_cs = cs[:, :CHUNK]
            a_tot = cs[:, CHUNK:]
            ea = jnp.exp(a_cs)
            w_end = dt * jnp.exp(a_tot - a_cs)
            rows = (_split_bf16(a_cs, 3) + _split_bf16(ea, 2) + _split_bf16(w_end, 2)
                    + _split_bf16(dt, 2))
            assert len(rows) == n_slots
            rows.append(jnp.zeros((LANES - n_slots * HEAD_SLOT, CHUNK), F32))
            packed = jnp.concatenate(rows, axis=0).T.astype(BF16)
            ex = _dot(packed, expand_ref[...])

            def ex_block(name):
                _, _, col0, width = layout[name]
                return ex[:, col0:col0 + width]

            acs_b = ex_block("acs")
            ea_e = ex_block("ea")
            xdt = (xs * ex_block("dt")).astype(BF16)
            xw = (xs * ex_block("w")).astype(BF16)
            yield
            ys = []
            for g in range(SSM_GROUPS):
                gs = slice(g * GROUP_WIDTH, (g + 1) * GROUP_WIDTH)
                bg_t = bc[:, g * D_STATE:(g + 1) * D_STATE].T.astype(BF16)
                cg = bc[:, (SSM_GROUPS + g) * D_STATE:(SSM_GROUPS + g + 1) * D_STATE].astype(BF16)
                cb = _dot(cg, bg_t)
                s_old = state_scr[g]
                y_off = _dot(cg, s_old.astype(BF16)) * ea_e[:, gs]
                ms = []
                xm = []
                for j in range(HEADS_PER_GROUP):
                    hh = g * HEADS_PER_GROUP + j
                    seg = acs_b[:, hh * LANES:(hh + 1) * LANES] - a_cs[hh:hh + 1, :]
                    dec = jnp.exp(jnp.where(causal, seg, MASK_VALUE))
                    ms.append((cb * dec).astype(BF16))
                    xm.append(jnp.where(group_head == j, xdt[:, gs], jnp.zeros((), BF16)))
                tick()
                y_diag = _dot(jnp.concatenate(ms, axis=1), jnp.concatenate(xm, axis=0))
                yield
                y_g = y_diag + y_off + xs[:, gs] * dskip_ref[:, gs]
                state_scr[g] = s_old * ea_e[CHUNK - 1:CHUNK, gs] + _dot(bg_t, xw[:, gs])
                y_g = y_g * _silu(proj(IN_Z + g)[r0:r0 + CHUNK, :])
                y_g = y_g * lax.rsqrt(jnp.mean(y_g * y_g, axis=-1, keepdims=True) + EPS)
                ys.append(y_g * snw_ref[:, gs])
                yield
            mix_scr[cur, pl.ds(r0, CHUNK), ATTN_WIDTH:] = jnp.concatenate(ys, axis=1).astype(BF16)

        return {"a0": attn_stages(0), "a1": attn_stages(1), "s": ssd_stages()}

    def filler_units():
        for j in range(max(N_AHEAD_PIECES, N_OUT_PIECES)):
            for rr in range(FILL_SPLIT):
                rows = slice(rr * ts // FILL_SPLIT, (rr + 1) * ts // FILL_SPLIT)
                if j < N_AHEAD_PIECES:
                    raw_scr[j, rows, :] = _dot(h_scr[rows, :], win_ref[:, _piece(j)])
                    yield
                if j < N_OUT_PIECES:
                    y_scr[j, rows, :] = _dot(mix_scr[1 - cur, rows, :], wout_ref[:, _piece(j)])
                    yield

    fill = filler_units()
    def tick(units=1):
        for _ in range(units):
            next(fill, None)
    streams = {}
    for c in range(n_chunks):
        for name, gen in chunk_streams(c, tick).items():
            streams[c, name] = gen
    for key in _stage_order(n_chunks):
        next(streams[key], None)
    for stream in streams.values():
        assert next(stream, "done") == "done"
    for _ in fill:
        pass

    k_scr[0:CHUNK, :] = k_scr[ts:ts + CHUNK, :]
    v_scr[0:CHUNK, :] = v_scr[ts:ts + CHUNK, :]
    xbc_scr[0:SUBLANES, :] = xbc_scr[ts:ts + SUBLANES, :]

    gate1 = modp_ref[0][2:3]
    for j in range(N_OUT_PIECES):
        cols = slice(j * MXU_WIDTH, (j + 1) * MXU_WIDTH)
        o_ref[0, :, cols] = xp_ref[0, :, cols] + gate1[:, cols] * y_scr[j]


def _const_spec(shape):
    nd = len(shape)
    return pl.BlockSpec(shape, lambda *_: (0,) * nd, pipeline_mode=pl.Buffered(1))


def _mixer(x, pos4, mod3, n1w, invf, win, wdt, convw, convb, dtb, alog, dskip, sinks, snw, wout, scan,
           expand):
    b, s, d = x.shape
    ts = MIXER_TILE
    assert ts % CHUNK == 0 and s % ts == 0
    tpr = s // ts
    n_tiles = b * tpr

    def nxt(step):
        t = jnp.minimum(step + 1, n_tiles - 1)
        return t // tpr, t % tpr

    def prv(step):
        t = jnp.maximum(step - 1, 0)
        return t // tpr, t % tpr

    def cur(step):
        t = jnp.minimum(step, n_tiles - 1)
        return t // tpr, t % tpr

    in_specs = [
        pl.BlockSpec((1, ts, d), lambda st: (*nxt(st), 0)),
        pl.BlockSpec((1, ts, d), lambda st: (*prv(st), 0)),
        pl.BlockSpec((1, 1, ts // ROPE_PACK, LANES), lambda st: (*cur(st), 0, 0)),
        pl.BlockSpec((1, N_MOD, d), lambda st: (nxt(st)[0], 0, 0)),
        pl.BlockSpec((1, N_MOD, d), lambda st: (prv(st)[0], 0, 0)),
        _const_spec(n1w.shape), _const_spec(invf.shape), _const_spec(win.shape),
        _const_spec(wdt.shape),
        _const_spec(convw.shape), _const_spec(convb.shape), _const_spec(dtb.shape),
        _const_spec(alog.shape), _const_spec(dskip.shape),
        pl.BlockSpec(memory_space=pltpu.SMEM),
        _const_spec(snw.shape), _const_spec(wout.shape), _const_spec(scan.shape),
        _const_spec(expand.shape),
    ]
    scratch = [
        pltpu.VMEM((N_AHEAD_PIECES, ts, MXU_WIDTH), F32),
        pltpu.VMEM((ts, d), BF16),
        pltpu.VMEM((N_OUT_PIECES, ts, MXU_WIDTH), F32),
        pltpu.VMEM((ts, ATTN_WIDTH), BF16),
        pltpu.VMEM((CHUNK + ts, KV_WIDTH), BF16),
        pltpu.VMEM((CHUNK + ts, KV_WIDTH), BF16),
        pltpu.VMEM((SUBLANES + ts, CONV_CH), F32),
        pltpu.VMEM((2 * SUBLANES, ts), F32),
        pltpu.VMEM((SSM_GROUPS, D_STATE, GROUP_WIDTH), F32),
        pltpu.VMEM((2, ts, MIX_WIDTH), BF16),
    ]
    return pl.pallas_call(
        lambda *refs: _mixer_kernel(tpr, *refs),
        grid=(n_tiles + 1,),
        in_specs=in_specs,
        out_specs=pl.BlockSpec((1, ts, d), lambda st: (*prv(st), 0)),
        out_shape=jax.ShapeDtypeStruct((b, s, d), F32),
        scratch_shapes=scratch,
        compiler_params=pltpu.CompilerParams(
            dimension_semantics=("arbitrary",), vmem_limit_bytes=VMEM_LIMIT_BYTES),
        name="mixer",
    )(x, x, pos4, mod3, mod3, n1w, invf, win, wdt, convw, convb, dtb, alog, dskip, sinks, snw, wout,
      scan, expand)


def _ffn_kernel(x_ref, mod_ref, n2w_ref, wgu_ref, wd_ref, nfw_ref, o_ref):
    mod = mod_ref[0]
    shift2, scale2, gate2 = mod[3:4], mod[4:5], mod[5:6]
    x = x_ref[0]
    h = _rmsnorm_mod(x, n2w_ref[...], shift2, scale2).astype(BF16)
    g = _dot(h, wgu_ref[:, :D_FF])
    u = _dot(h, wgu_ref[:, D_FF:])
    act = (_silu(g) * u).astype(BF16)
    x2 = x + gate2 * _dot(act, wd_ref[...])
    ms = jnp.mean(x2 * x2, axis=-1, keepdims=True)
    o_ref[0] = x2 * lax.rsqrt(ms + EPS) * nfw_ref[...]


def _ffn(x1, mod3, n2w, wgu, wd, nfw):
    b, s, d = x1.shape
    tm = FFN_TILE
    return pl.pallas_call(
        _ffn_kernel,
        grid=(b, s // tm),
        in_specs=[
            pl.BlockSpec((1, tm, d), lambda bb, i: (bb, i, 0)),
            pl.BlockSpec((1, N_MOD, d), lambda bb, i: (bb, 0, 0)),
            _const_spec(n2w.shape), _const_spec(wgu.shape), _const_spec(wd.shape),
            _const_spec(nfw.shape),
        ],
        out_specs=pl.BlockSpec((1, tm, d), lambda bb, i: (bb, i, 0)),
        out_shape=jax.ShapeDtypeStruct((b, s, d), F32),
        compiler_params=pltpu.CompilerParams(
            dimension_semantics=("arbitrary", "arbitrary"), vmem_limit_bytes=VMEM_LIMIT_BYTES),
        name="ffn",
    )(x1, mod3, n2w, wgu, wd, nfw)


def _head_rows(v):
    return jnp.broadcast_to(v.astype(F32)[:, None], (SSM_HEADS, LANES))


def kernel(x, c, positions, w_ada, b_ada, norm1_w, w_in, conv_w, conv_b, dt_bias, a_log, d_skip,
           attn_sinks, ssm_norm_w, w_out, norm2_w, w_gate_up, w_down, final_norm_w):
    b, s, d = x.shape
    depth = w_ada.shape[0]
    half = HEAD_DIM // 2
    inv_freq = ROPE_THETA ** (-jnp.arange(half, dtype=F32) / half)
    invf = jnp.tile(inv_freq, LANES // half).reshape(1, LANES)
    scan = _scan_matrix()
    expand = _expand_matrix()
    ts = MIXER_TILE
    pos4 = positions.reshape(b, s // ts, ROPE_PACK, ts // ROPE_PACK).transpose(0, 1, 3, 2)
    pos4 = jnp.repeat(pos4, half, axis=-1)
    o1 = ATTN_WIDTH + 2 * KV_WIDTH
    o2 = o1 + SSM_WIDTH
    o3 = o2 + CONV_CH
    assert depth == 1, "the final norm is fused into the (single) layer's ffn call"
    layer = 0
    mod3 = _adaln_mod(c, w_ada[layer], b_ada[layer]).reshape(b, N_MOD, d)
    w_in_l = w_in[layer].astype(BF16)
    w_cat = jnp.concatenate([w_in_l[:, o2:o3], w_in_l[:, :o1], w_in_l[:, o1:o2]], axis=1)
    assert w_cat.shape[1] == N_IN_PIECES * MXU_WIDTH
    wdt = jnp.pad(w_in_l[:, o3:].T, ((0, 2 * SUBLANES - SSM_HEADS), (0, 0)))
    x1 = _mixer(
        x, pos4, mod3, norm1_w[layer].reshape(1, d), invf, w_cat, wdt,
        conv_w[layer], conv_b[layer].reshape(1, CONV_CH),
        _head_rows(dt_bias[layer]), _head_rows(a_log[layer]),
        jnp.repeat(d_skip[layer].astype(F32), SSM_HEAD_DIM).reshape(1, SSM_WIDTH),
        attn_sinks[layer].astype(F32),
        ssm_norm_w[layer].reshape(1, SSM_WIDTH), w_out[layer].astype(BF16), scan, expand)
    return _ffn(x1, mod3, norm2_w[layer].reshape(1, d), w_gate_up[layer].astype(BF16),
                w_down[layer].astype(BF16), final_norm_w.reshape(1, d))
```

```python
import math

import jax
import jax.numpy as jnp
import numpy as np
from jax import lax
from jax.experimental import pallas as pl
from jax.experimental.pallas import tpu as pltpu

D_MODEL = 1024
HEAD_DIM = 64
N_Q_HEADS = 8
N_KV_HEADS = 2
Q_PER_KV = N_Q_HEADS // N_KV_HEADS
ATTN_WIDTH = N_Q_HEADS * HEAD_DIM
KV_WIDTH = N_KV_HEADS * HEAD_DIM
WINDOW = 128
ROPE_THETA = 10000.0
SSM_HEADS = 8
SSM_HEAD_DIM = 64
SSM_WIDTH = SSM_HEADS * SSM_HEAD_DIM
SSM_GROUPS = 2
HEADS_PER_GROUP = SSM_HEADS // SSM_GROUPS
GROUP_WIDTH = HEADS_PER_GROUP * SSM_HEAD_DIM
D_STATE = 128
CONV_WIDTH = 4
CHUNK = 128
CONV_CH = SSM_WIDTH + 2 * SSM_GROUPS * D_STATE
MIX_WIDTH = ATTN_WIDTH + SSM_WIDTH
D_FF = 2816
N_MOD = 6
EPS = 1e-6

LANES = 128
SUBLANES = 8
MXU_WIDTH = 256
HEAD_SLOT = 8
ROPE_PACK = LANES // (HEAD_DIM // 2)
MASK_VALUE = -1e30
VMEM_LIMIT_BYTES = 56 * 1024 * 1024

MIXER_TILE = 512
FFN_TILE = 1024
MOD_TILE = 1024

F32 = jnp.float32
BF16 = jnp.bfloat16

IN_Q = 0
IN_KV = IN_Q + ATTN_WIDTH // MXU_WIDTH
IN_Z = IN_KV + 2 * KV_WIDTH // MXU_WIDTH
IN_XBC = IN_Z + SSM_WIDTH // MXU_WIDTH
N_AHEAD_PIECES = CONV_CH // MXU_WIDTH
N_IN_PIECES = IN_XBC + N_AHEAD_PIECES
N_OUT_PIECES = D_MODEL // MXU_WIDTH
FILL_SPLIT = 1
assert GROUP_WIDTH == MXU_WIDTH

ATTN_STAGES = 4
SSD_STAGES = 7
_SSD_NEEDS_PREV_PAST = {2: 4, 4: 6}


def _stage_order(n_chunks):
    prog = {(k, name): 0 for k in range(n_chunks) for name in ("a0", "a1", "s")}
    order = []
    n_stages = {"a0": ATTN_STAGES, "a1": ATTN_STAGES, "s": SSD_STAGES}
    while any(v < n_stages[name] for (_, name), v in prog.items()):
        for k in range(n_chunks):
            for name in ("a0", "a1"):
                if prog[k, name] < ATTN_STAGES:
                    prog[k, name] += 1
                    order.append((k, name))
        for k in range(n_chunks):
            stage = prog[k, "s"]
            need = _SSD_NEEDS_PREV_PAST.get(stage) if k else None
            if stage < SSD_STAGES and (need is None or prog[k - 1, "s"] >= need):
                prog[k, "s"] += 1
                order.append((k, "s"))
    assert all(v == (SSD_STAGES if name == "s" else ATTN_STAGES) for (_, name), v in prog.items())
    return order

_EXPAND_ITEMS = (("acs", 3, LANES), ("ea", 2, SSM_HEAD_DIM), ("w", 2, SSM_HEAD_DIM),
                 ("dt", 2, SSM_HEAD_DIM))


def _expand_layout():
    slot = 0
    col = 0
    layout = {}
    for name, parts, rep in _EXPAND_ITEMS:
        layout[name] = (slot, parts, col, rep * SSM_HEADS)
        slot += parts
        col += rep * SSM_HEADS
    return layout, slot, col


def _expand_matrix():
    layout, nslots, ncols = _expand_layout()
    assert nslots * HEAD_SLOT <= LANES
    e = np.zeros((LANES, ncols), np.float32)
    for name, parts, rep in _EXPAND_ITEMS:
        slot0, _, col0, _ = layout[name]
        for p in range(parts):
            for hh in range(SSM_HEADS):
                e[(slot0 + p) * HEAD_SLOT + hh, col0 + hh * rep:col0 + (hh + 1) * rep] = 1.0
    return jnp.asarray(e, BF16)


def _scan_matrix():
    t = np.arange(CHUNK)
    upper = (t[:, None] <= t[None, :]).astype(np.float32)
    return jnp.asarray(np.concatenate([upper, np.ones((CHUNK, CHUNK), np.float32)], axis=1), BF16)


def _silu(x):
    hx = 0.5 * x
    return hx + hx * jnp.tanh(hx)


def _softplus(x):
    return jnp.maximum(x, 0.0) + jnp.log1p(jnp.exp(-jnp.abs(x)))


def _dot(a, b):
    return jnp.dot(a, b, preferred_element_type=F32)


def _dot_nt(a, b):
    return lax.dot_general(a, b, (((1,), (1,)), ((), ())), preferred_element_type=F32)


def _dot_tn(a, b):
    return lax.dot_general(a, b, (((0,), (0,)), ((), ())), preferred_element_type=F32)


def _rmsnorm_mod(x, norm_w, shift, scale):
    ms = jnp.mean(x * x, axis=-1, keepdims=True)
    return x * lax.rsqrt(ms + EPS) * (norm_w * (1.0 + scale)) + shift


def _piece(j):
    return slice(j * MXU_WIDTH, (j + 1) * MXU_WIDTH)


def _split_bf16(v, parts):
    out = []
    r = v
    for _ in range(parts):
        p = r.astype(BF16).astype(F32)
        out.append(p)
        r = r - p
    return out


def _mod_kernel(c_ref, w_ref, b_ref, o_ref):
    sc = _silu(c_ref[...]).astype(BF16)
    o_ref[...] = _dot(sc, w_ref[...].astype(BF16)) + b_ref[...]


def _adaln_mod(c, w_ada, b_ada):
    b, d = c.shape
    n = w_ada.shape[1]
    return pl.pallas_call(
        _mod_kernel,
        grid=(n // MOD_TILE,),
        in_specs=[
            pl.BlockSpec((b, d), lambda j: (0, 0)),
            pl.BlockSpec((d, MOD_TILE), lambda j: (0, j)),
            pl.BlockSpec((1, MOD_TILE), lambda j: (0, j)),
        ],
        out_specs=pl.BlockSpec((b, MOD_TILE), lambda j: (0, j)),
        out_shape=jax.ShapeDtypeStruct((b, n), F32),
        compiler_params=pltpu.CompilerParams(
            dimension_semantics=("arbitrary",), vmem_limit_bytes=VMEM_LIMIT_BYTES),
        name="adaln_mod",
    )(c, w_ada, b_ada.reshape(1, n))


def _mixer_kernel(tiles_per_row,
                  xn_ref, xp_ref, pos_ref, modn_ref, modp_ref, n1w_ref, invf_ref, win_ref, wdt_ref,
                  convw_ref, convb_ref, dtb_ref, alog_ref, dskip_ref, sinks_ref,
                  snw_ref, wout_ref, scan_ref, expand_ref,
                  o_ref,
                  raw_scr, h_scr, y_scr, q_scr, k_scr, v_scr, xbc_scr, dt_scr, state_scr,
                  mix_scr):
    ts = xn_ref.shape[1]
    n_chunks = ts // CHUNK
    s = pl.program_id(0)
    i = lax.rem(s, tiles_per_row)
    cur = lax.rem(s, 2)
    layout, n_slots, _ = _expand_layout()

    def norm_next():
        modn = modn_ref[0]
        return _rmsnorm_mod(xn_ref[0], n1w_ref[...], modn[0:1], modn[1:2]).astype(BF16)

    @pl.when(s == 0)
    def _():
        modp = modp_ref[0]
        h0 = _rmsnorm_mod(xp_ref[0], n1w_ref[...], modp[0:1], modp[1:2]).astype(BF16)
        h_scr[...] = h0
        for j in range(N_AHEAD_PIECES):
            raw_scr[j] = _dot(h0, win_ref[:, _piece(IN_XBC + j)])
        y_scr[...] = jnp.zeros(y_scr.shape, F32)
        mix_scr[...] = jnp.zeros(mix_scr.shape, BF16)

    @pl.when(i == 0)
    def _():
        k_scr[0:CHUNK, :] = jnp.zeros((CHUNK, KV_WIDTH), BF16)
        v_scr[0:CHUNK, :] = jnp.zeros((CHUNK, KV_WIDTH), BF16)
        xbc_scr[0:SUBLANES, :] = jnp.zeros((SUBLANES, CONV_CH), F32)
        state_scr[...] = jnp.zeros(state_scr.shape, F32)

    for j in range(N_AHEAD_PIECES):
        xbc_scr[SUBLANES:SUBLANES + ts, _piece(j)] = raw_scr[j]
    h = h_scr[...]
    pieces = {j: _dot(h, win_ref[:, _piece(j)]) for j in range(IN_Q, IN_XBC)}

    def proj(j):
        return pieces[j]

    half = HEAD_DIM // 2
    ang_d = pos_ref[0, 0].astype(F32) * invf_ref[...]
    lane_d = lax.broadcasted_iota(jnp.int32, ang_d.shape, 1)

    def spread(tbl):
        blocks = []
        for k in range(ROPE_PACK):
            t = pltpu.roll(tbl, LANES - half * k, axis=1) if k else tbl
            t = jnp.where(lane_d < half, t, pltpu.roll(t, half, axis=1))
            t = jnp.where(lane_d < 2 * half, t, pltpu.roll(t, 2 * half, axis=1))
            blocks.append(t)
        return jnp.concatenate(blocks, axis=0)

    cos = spread(jnp.cos(ang_d))
    sin = spread(jnp.sin(ang_d))
    lane = lax.broadcasted_iota(jnp.int32, (ts, LANES), 1)
    first_half = (lane % HEAD_DIM) < half
    sin_signed = jnp.where(first_half, -sin, sin)

    def rope(t, rows):
        lane_c = lax.broadcasted_iota(jnp.int32, t.shape, 1)
        rot = jnp.where((lane_c % HEAD_DIM) < half, pltpu.roll(t, LANES - half, axis=1),
                        pltpu.roll(t, half, axis=1))
        return t * cos[rows] + rot * sin_signed[rows]

    q_scale = 1.0 / math.sqrt(HEAD_DIM)

    def rope_rows(c):
        rows = slice(c * CHUNK, (c + 1) * CHUNK)
        for j in range(ATTN_WIDTH // LANES):
            piece, off = divmod(j * LANES, MXU_WIDTH)
            q_scr[rows, j * LANES:(j + 1) * LANES] = (
                rope(proj(IN_Q + piece)[rows, off:off + LANES], rows) * q_scale).astype(BF16)
        krows = slice(CHUNK + c * CHUNK, CHUNK + (c + 1) * CHUNK)
        k_scr[krows, :] = rope(proj(IN_KV)[rows, 0:KV_WIDTH], rows).astype(BF16)
        v_scr[krows, :] = proj(IN_KV)[rows, KV_WIDTH:2 * KV_WIDTH].astype(BF16)

    dt_scr[...] = _dot_nt(wdt_ref[...], h)

    h_scr[...] = norm_next()

    a_neg = -jnp.exp(alog_ref[...])
    causal = (lax.broadcasted_iota(jnp.int32, (CHUNK, CHUNK), 0)
              >= lax.broadcasted_iota(jnp.int32, (CHUNK, CHUNK), 1))
    x_cols = slice(0, SSM_WIDTH)
    bc_cols = slice(SSM_WIDTH, CONV_CH)
    group_head = lax.broadcasted_iota(jnp.int32, (CHUNK, GROUP_WIDTH), 1) // SSM_HEAD_DIM

    def conv_silu(row0, cols):
        win = xbc_scr[pl.ds(row0, SUBLANES + CHUNK), cols]
        conv = convb_ref[:, cols]
        for t in range(CONV_WIDTH):
            back = CONV_WIDTH - 1 - t
            tap = pltpu.roll(win, back, axis=0) if back else win
            conv = conv + convw_ref[t:t + 1, cols] * tap[SUBLANES:SUBLANES + CHUNK]
        return _silu(conv)

    n_ql = Q_PER_KV * CHUNK
    key_idx = lax.broadcasted_iota(jnp.int32, (CHUNK, n_ql), 0)
    qry_idx = lax.broadcasted_iota(jnp.int32, (CHUNK, n_ql), 1) % CHUNK
    upper = key_idx > qry_idx
    lane_head = lax.broadcasted_iota(jnp.int32, (1, n_ql), 1) // CHUNK

    def chunk_streams(c, tick):
        r0 = c * CHUNK
        prev_bias = jnp.where((i == 0) & (c == 0), MASK_VALUE, 0.0).astype(F32)

        def attn_stages(g):
            if g == 0:
                rope_rows(c)
            kband = k_scr[pl.ds(r0, 2 * CHUNK), :]
            vband = v_scr[pl.ds(r0, 2 * CHUNK), :]
            qblk = q_scr[pl.ds(r0, CHUNK), :]
            qg = jnp.concatenate(
                [qblk[:, (g * Q_PER_KV + j) * HEAD_DIM:(g * Q_PER_KV + j + 1) * HEAD_DIM]
                 for j in range(Q_PER_KV)], axis=0)
            s_t = _dot_nt(kband[:, g * HEAD_DIM:(g + 1) * HEAD_DIM], qg)
            yield
            sf = jnp.where(upper, s_t[:CHUNK] + prev_bias, s_t[CHUNK:])
            sink = jnp.zeros((1, n_ql), F32)
            for j in range(Q_PER_KV):
                sink = jnp.where(lane_head == j, sinks_ref[g * Q_PER_KV + j], sink)
            m = jnp.maximum(jnp.max(sf, axis=0, keepdims=True), sink)
            pf = jnp.exp(sf - m)
            denom = jnp.sum(pf, axis=0, keepdims=True) + jnp.exp(sink - m)
            p_t = jnp.concatenate([jnp.where(upper, pf, 0.0), jnp.where(upper, 0.0, pf)],
                                  axis=0).astype(BF16)
            yield
            tick()
            o_t = _dot_tn(vband[:, g * HEAD_DIM:(g + 1) * HEAD_DIM], p_t)
            o_t = o_t * (1.0 / denom)
            yield
            tiles = []
            for pp in range(Q_PER_KV // 2):
                blk = jnp.concatenate([o_t[:, (2 * pp) * CHUNK:(2 * pp + 1) * CHUNK],
                                       o_t[:, (2 * pp + 1) * CHUNK:(2 * pp + 2) * CHUNK]], axis=0)
                tiles.append(blk.T)
            mix_scr[cur, pl.ds(r0, CHUNK),
                    g * Q_PER_KV * HEAD_DIM:(g + 1) * Q_PER_KV * HEAD_DIM] = (
                jnp.concatenate(tiles, axis=1).astype(BF16))

        def ssd_stages():
            xs = conv_silu(r0, x_cols)
            bc = conv_silu(r0, bc_cols)
            yield
            dt = _softplus(dt_scr[0:SSM_HEADS, pl.ds(r0, CHUNK)] + dtb_ref[...])
            a = dt * a_neg
            pad = jnp.zeros((SUBLANES, CHUNK), F32)
            cs = _dot(jnp.concatenate(_split_bf16(a, 3) + [pad], axis=0).astype(BF16),
                      scan_ref[...])
            cs = cs[0:8] + cs[8:16] + cs[16:24]
            a_cs = cs[:, :CHUNK]
            a_tot = cs[:, CHUNK:]
            ea = jnp.exp(a_cs)
            w_end = dt * jnp.exp(a_tot - a_cs)
            rows = (_split_bf16(a_cs, 3) + _split_bf16(ea, 2) + _split_bf16(w_end, 2)
                    + _split_bf16(dt, 2))
            assert len(rows) == n_slots
            rows.append(jnp.zeros((LANES - n_slots * HEAD_SLOT, CHUNK), F32))
            packed = jnp.concatenate(rows, axis=0).T.astype(BF16)
            ex = _dot(packed, expand_ref[...])

            def ex_block(name):
                _, _, col0, width = layout[name]
                return ex[:, col0:col0 + width]

            acs_b = ex_block("acs")
            ea_e = ex_block("ea")
            xdt = (xs * ex_block("dt")).astype(BF16)
            xw = (xs * ex_block("w")).astype(BF16)
            yield
            ys = []
            for g in range(SSM_GROUPS):
                gs = slice(g * GROUP_WIDTH, (g + 1) * GROUP_WIDTH)
                bg_t = bc[:, g * D_STATE:(g + 1) * D_STATE].T.astype(BF16)
                cg = bc[:, (SSM_GROUPS + g) * D_STATE:(SSM_GROUPS + g + 1) * D_STATE].astype(BF16)
                cb = _dot(cg, bg_t)
                s_old = state_scr[g]
                y_off = _dot(cg, s_old.astype(BF16)) * ea_e[:, gs]
                ms = []
                xm = []
                for j in range(HEADS_PER_GROUP):
                    hh = g * HEADS_PER_GROUP + j
                    seg = acs_b[:, hh * LANES:(hh + 1) * LANES] - a_cs[hh:hh + 1, :]
                    dec = jnp.exp(jnp.where(causal, seg, MASK_VALUE))
                    ms.append((cb * dec).astype(BF16))
                    xm.append(jnp.where(group_head == j, xdt[:, gs], jnp.zeros((), BF16)))
                tick()
                y_diag = _dot(jnp.concatenate(ms, axis=1), jnp.concatenate(xm, axis=0))
                yield
                y_g = y_diag + y_off + xs[:, gs] * dskip_ref[:, gs]
                state_scr[g] = s_old * ea_e[CHUNK - 1:CHUNK, gs] + _dot(bg_t, xw[:, gs])
                y_g = y_g * _silu(proj(IN_Z + g)[r0:r0 + CHUNK, :])
                y_g = y_g * lax.rsqrt(jnp.mean(y_g * y_g, axis=-1, keepdims=True) + EPS)
                ys.append(y_g * snw_ref[:, gs])
                yield
            mix_scr[cur, pl.ds(r0, CHUNK), ATTN_WIDTH:] = jnp.concatenate(ys, axis=1).astype(BF16)

        return {"a0": attn_stages(0), "a1": attn_stages(1), "s": ssd_stages()}

    def filler_units():
        for j in range(max(N_AHEAD_PIECES, N_OUT_PIECES)):
            for rr in range(FILL_SPLIT):
                rows = slice(rr * ts // FILL_SPLIT, (rr + 1) * ts // FILL_SPLIT)
                if j < N_AHEAD_PIECES:
                    raw_scr[j, rows, :] = _dot(h_scr[rows, :], win_ref[:, _piece(IN_XBC + j)])
                    yield
                if j < N_OUT_PIECES:
                    y_scr[j, rows, :] = _dot(mix_scr[1 - cur, rows, :], wout_ref[:, _piece(j)])
                    yield

    fill = filler_units()
    def tick(units=1):
        for _ in range(units):
            next(fill, None)
    streams = {}
    for c in range(n_chunks):
        for name, gen in chunk_streams(c, tick).items():
            streams[c, name] = gen
    for key in _stage_order(n_chunks):
        next(streams[key], None)
    for stream in streams.values():
        assert next(stream, "done") == "done"
    for _ in fill:
        pass

    k_scr[0:CHUNK, :] = k_scr[ts:ts + CHUNK, :]
    v_scr[0:CHUNK, :] = v_scr[ts:ts + CHUNK, :]
    xbc_scr[0:SUBLANES, :] = xbc_scr[ts:ts + SUBLANES, :]

    gate1 = modp_ref[0][2:3]
    for j in range(N_OUT_PIECES):
        cols = slice(j * MXU_WIDTH, (j + 1) * MXU_WIDTH)
        o_ref[0, :, cols] = xp_ref[0, :, cols] + gate1[:, cols] * y_scr[j]


def _const_spec(shape):
    nd = len(shape)
    return pl.BlockSpec(shape, lambda *_: (0,) * nd, pipeline_mode=pl.Buffered(1))


def _mixer(x, pos4, mod3, n1w, invf, win, wdt, convw, convb, dtb, alog, dskip, sinks, snw, wout, scan,
           expand):
    b, s, d = x.shape
    ts = MIXER_TILE
    assert ts % CHUNK == 0 and s % ts == 0
    tpr = s // ts
    n_tiles = b * tpr

    def nxt(step):
        t = jnp.minimum(step + 1, n_tiles - 1)
        return t // tpr, t % tpr

    def prv(step):
        t = jnp.maximum(step - 1, 0)
        return t // tpr, t % tpr

    def cur(step):
        t = jnp.minimum(step, n_tiles - 1)
        return t // tpr, t % tpr

    in_specs = [
        pl.BlockSpec((1, ts, d), lambda st: (*nxt(st), 0)),
        pl.BlockSpec((1, ts, d), lambda st: (*prv(st), 0)),
        pl.BlockSpec((1, 1, ts // ROPE_PACK, LANES), lambda st: (*cur(st), 0, 0)),
        pl.BlockSpec((1, N_MOD, d), lambda st: (nxt(st)[0], 0, 0)),
        pl.BlockSpec((1, N_MOD, d), lambda st: (prv(st)[0], 0, 0)),
        _const_spec(n1w.shape), _const_spec(invf.shape), _const_spec(win.shape),
        _const_spec(wdt.shape),
        _const_spec(convw.shape), _const_spec(convb.shape), _const_spec(dtb.shape),
        _const_spec(alog.shape), _const_spec(dskip.shape),
        pl.BlockSpec(memory_space=pltpu.SMEM),
        _const_spec(snw.shape), _const_spec(wout.shape), _const_spec(scan.shape),
        _const_spec(expand.shape),
    ]
    scratch = [
        pltpu.VMEM((N_AHEAD_PIECES, ts, MXU_WIDTH), F32),
        pltpu.VMEM((ts, d), BF16),
        pltpu.VMEM((N_OUT_PIECES, ts, MXU_WIDTH), F32),
        pltpu.VMEM((ts, ATTN_WIDTH), BF16),
        pltpu.VMEM((CHUNK + ts, KV_WIDTH), BF16),
        pltpu.VMEM((CHUNK + ts, KV_WIDTH), BF16),
        pltpu.VMEM((SUBLANES + ts, CONV_CH), F32),
        pltpu.VMEM((2 * SUBLANES, ts), F32),
        pltpu.VMEM((SSM_GROUPS, D_STATE, GROUP_WIDTH), F32),
        pltpu.VMEM((2, ts, MIX_WIDTH), BF16),
    ]
    return pl.pallas_call(
        lambda *refs: _mixer_kernel(tpr, *refs),
        grid=(n_tiles + 1,),
        in_specs=in_specs,
        out_specs=pl.BlockSpec((1, ts, d), lambda st: (*prv(st), 0)),
        out_shape=jax.ShapeDtypeStruct((b, s, d), F32),
        scratch_shapes=scratch,
        compiler_params=pltpu.CompilerParams(
            dimension_semantics=("arbitrary",), vmem_limit_bytes=VMEM_LIMIT_BYTES),
        name="mixer",
    )(x, x, pos4, mod3, mod3, n1w, invf, win, wdt, convw, convb, dtb, alog, dskip, sinks, snw, wout,
      scan, expand)


def _ffn_kernel(x_ref, mod_ref, n2w_ref, wgu_ref, wd_ref, nfw_ref, o_ref):
    mod = mod_ref[0]
    shift2, scale2, gate2 = mod[3:4], mod[4:5], mod[5:6]
    x = x_ref[0]
    h = _rmsnorm_mod(x, n2w_ref[...], shift2, scale2).astype(BF16)
    g = _dot(h, wgu_ref[:, :D_FF])
    u = _dot(h, wgu_ref[:, D_FF:])
    act = (_silu(g) * u).astype(BF16)
    x2 = x + gate2 * _dot(act, wd_ref[...])
    ms = jnp.mean(x2 * x2, axis=-1, keepdims=True)
    o_ref[0] = x2 * lax.rsqrt(ms + EPS) * nfw_ref[...]


def _ffn(x1, mod3, n2w, wgu, wd, nfw):
    b, s, d = x1.shape
    tm = FFN_TILE
    return pl.pallas_call(
        _ffn_kernel,
        grid=(b, s // tm),
        in_specs=[
            pl.BlockSpec((1, tm, d), lambda bb, i: (bb, i, 0)),
            pl.BlockSpec((1, N_MOD, d), lambda bb, i: (bb, 0, 0)),
            _const_spec(n2w.shape), _const_spec(wgu.shape), _const_spec(wd.shape),
            _const_spec(nfw.shape),
        ],
        out_specs=pl.BlockSpec((1, tm, d), lambda bb, i: (bb, i, 0)),
        out_shape=jax.ShapeDtypeStruct((b, s, d), F32),
        compiler_params=pltpu.CompilerParams(
            dimension_semantics=("arbitrary", "arbitrary"), vmem_limit_bytes=VMEM_LIMIT_BYTES),
        name="ffn",
    )(x1, mod3, n2w, wgu, wd, nfw)


def _head_rows(v):
    return jnp.broadcast_to(v.astype(F32)[:, None], (SSM_HEADS, LANES))


def kernel(x, c, positions, w_ada, b_ada, norm1_w, w_in, conv_w, conv_b, dt_bias, a_log, d_skip,
           attn_sinks, ssm_norm_w, w_out, norm2_w, w_gate_up, w_down, final_norm_w):
    b, s, d = x.shape
    depth = w_ada.shape[0]
    half = HEAD_DIM // 2
    inv_freq = ROPE_THETA ** (-jnp.arange(half, dtype=F32) / half)
    invf = jnp.tile(inv_freq, LANES // half).reshape(1, LANES)
    scan = _scan_matrix()
    expand = _expand_matrix()
    ts = MIXER_TILE
    pos4 = positions.reshape(b, s // ts, ROPE_PACK, ts // ROPE_PACK).transpose(0, 1, 3, 2)
    pos4 = jnp.repeat(pos4, half, axis=-1)
    assert depth == 1, "the final norm is fused into the (single) layer's ffn call"
    layer = 0
    mod3 = _adaln_mod(c, w_ada[layer], b_ada[layer]).reshape(b, N_MOD, d)
    w_in_l = w_in[layer].astype(BF16)
    n_main = N_IN_PIECES * MXU_WIDTH
    assert w_in_l.shape[1] == n_main + SSM_HEADS
    wdt = jnp.pad(w_in_l[:, n_main:].T, ((0, 2 * SUBLANES - SSM_HEADS), (0, 0)))
    x1 = _mixer(
        x, pos4, mod3, norm1_w[layer].reshape(1, d), invf, w_in_l, wdt,
        conv_w[layer], conv_b[layer].reshape(1, CONV_CH),
        _head_rows(dt_bias[layer]), _head_rows(a_log[layer]),
        jnp.repeat(d_skip[layer].astype(F32), SSM_HEAD_DIM).reshape(1, SSM_WIDTH),
        attn_sinks[layer].astype(F32),
        ssm_norm_w[layer].reshape(1, SSM_WIDTH), w_out[layer].astype(BF16), scan, expand)
    return _ffn(x1, mod3, norm2_w[layer].reshape(1, d), w_gate_up[layer].astype(BF16),
                w_down[layer].astype(BF16), final_norm_w.reshape(1, d))
```

```python
import math

import jax
import jax.numpy as jnp
import numpy as np
from jax import lax
from jax.experimental import pallas as pl
from jax.experimental.pallas import tpu as pltpu

D_MODEL = 1024
HEAD_DIM = 64
N_Q_HEADS = 8
N_KV_HEADS = 2
Q_PER_KV = N_Q_HEADS // N_KV_HEADS
ATTN_WIDTH = N_Q_HEADS * HEAD_DIM
KV_WIDTH = N_KV_HEADS * HEAD_DIM
WINDOW = 128
ROPE_THETA = 10000.0
SSM_HEADS = 8
SSM_HEAD_DIM = 64
SSM_WIDTH = SSM_HEADS * SSM_HEAD_DIM
SSM_GROUPS = 2
HEADS_PER_GROUP = SSM_HEADS // SSM_GROUPS
GROUP_WIDTH = HEADS_PER_GROUP * SSM_HEAD_DIM
D_STATE = 128
CONV_WIDTH = 4
CHUNK = 128
CONV_CH = SSM_WIDTH + 2 * SSM_GROUPS * D_STATE
MIX_WIDTH = ATTN_WIDTH + SSM_WIDTH
D_FF = 2816
N_MOD = 6
EPS = 1e-6

LANES = 128
SUBLANES = 8
MXU_WIDTH = 256
HEAD_SLOT = 8
ROPE_PACK = LANES // (HEAD_DIM // 2)
MASK_VALUE = -1e30
VMEM_LIMIT_BYTES = 56 * 1024 * 1024

MIXER_TILE = 512
FFN_TILE = 1024
MOD_TILE = 2048

F32 = jnp.float32
BF16 = jnp.bfloat16

IN_Q = 0
IN_KV = IN_Q + ATTN_WIDTH // MXU_WIDTH
IN_Z = IN_KV + 2 * KV_WIDTH // MXU_WIDTH
IN_XBC = IN_Z + SSM_WIDTH // MXU_WIDTH
N_AHEAD_PIECES = CONV_CH // MXU_WIDTH
N_IN_PIECES = IN_XBC + N_AHEAD_PIECES
N_OUT_PIECES = D_MODEL // MXU_WIDTH
FILL_SPLIT = 1
assert GROUP_WIDTH == MXU_WIDTH

ATTN_STAGES = 4
SSD_STAGES = 7
_SSD_NEEDS_PREV_PAST = {2: 4, 4: 6}


def _stage_order(n_chunks):
    prog = {(k, name): 0 for k in range(n_chunks) for name in ("a0", "a1", "s")}
    order = []
    n_stages = {"a0": ATTN_STAGES, "a1": ATTN_STAGES, "s": SSD_STAGES}
    while any(v < n_stages[name] for (_, name), v in prog.items()):
        for k in range(n_chunks):
            for name in ("a0", "a1"):
                if prog[k, name] < ATTN_STAGES:
                    prog[k, name] += 1
                    order.append((k, name))
        for k in range(n_chunks):
            stage = prog[k, "s"]
            need = _SSD_NEEDS_PREV_PAST.get(stage) if k else None
            if stage < SSD_STAGES and (need is None or prog[k - 1, "s"] >= need):
                prog[k, "s"] += 1
                order.append((k, "s"))
    assert all(v == (SSD_STAGES if name == "s" else ATTN_STAGES) for (_, name), v in prog.items())
    return order

_EXPAND_ITEMS = (("acs", 3, LANES), ("ea", 2, SSM_HEAD_DIM), ("w", 2, SSM_HEAD_DIM),
                 ("dt", 2, SSM_HEAD_DIM))


def _expand_layout():
    slot = 0
    col = 0
    layout = {}
    for name, parts, rep in _EXPAND_ITEMS:
        layout[name] = (slot, parts, col, rep * SSM_HEADS)
        slot += parts
        col += rep * SSM_HEADS
    return layout, slot, col


def _expand_matrix():
    layout, nslots, ncols = _expand_layout()
    assert nslots * HEAD_SLOT <= LANES
    e = np.zeros((LANES, ncols), np.float32)
    for name, parts, rep in _EXPAND_ITEMS:
        slot0, _, col0, _ = layout[name]
        for p in range(parts):
            for hh in range(SSM_HEADS):
                e[(slot0 + p) * HEAD_SLOT + hh, col0 + hh * rep:col0 + (hh + 1) * rep] = 1.0
    return jnp.asarray(e, BF16)


def _scan_matrix():
    t = np.arange(CHUNK)
    upper = (t[:, None] <= t[None, :]).astype(np.float32)
    return jnp.asarray(np.concatenate([upper, np.ones((CHUNK, CHUNK), np.float32)], axis=1), BF16)


def _silu(x):
    hx = 0.5 * x
    return hx + hx * jnp.tanh(hx)


def _softplus(x):
    return jnp.maximum(x, 0.0) + jnp.log1p(jnp.exp(-jnp.abs(x)))


def _dot(a, b):
    return jnp.dot(a, b, preferred_element_type=F32)


def _dot_nt(a, b):
    return lax.dot_general(a, b, (((1,), (1,)), ((), ())), preferred_element_type=F32)


def _dot_tn(a, b):
    return lax.dot_general(a, b, (((0,), (0,)), ((), ())), preferred_element_type=F32)


def _rmsnorm_mod(x, norm_w, shift, scale):
    ms = jnp.mean(x * x, axis=-1, keepdims=True)
    return x * lax.rsqrt(ms + EPS) * (norm_w * (1.0 + scale)) + shift


def _piece(j):
    return slice(j * MXU_WIDTH, (j + 1) * MXU_WIDTH)


def _split_bf16(v, parts):
    out = []
    r = v
    for _ in range(parts):
        p = r.astype(BF16).astype(F32)
        out.append(p)
        r = r - p
    return out


def _mod_kernel(c_ref, w_ref, b_ref, o_ref):
    sc = _silu(c_ref[...]).astype(BF16)
    o_ref[...] = _dot(sc, w_ref[...].astype(BF16)) + b_ref[...]


def _adaln_mod(c, w_ada, b_ada):
    b, d = c.shape
    n = w_ada.shape[1]
    return pl.pallas_call(
        _mod_kernel,
        grid=(n // MOD_TILE,),
        in_specs=[
            pl.BlockSpec((b, d), lambda j: (0, 0)),
            pl.BlockSpec((d, MOD_TILE), lambda j: (0, j)),
            pl.BlockSpec((1, MOD_TILE), lambda j: (0, j)),
        ],
        out_specs=pl.BlockSpec((b, MOD_TILE), lambda j: (0, j)),
        out_shape=jax.ShapeDtypeStruct((b, n), F32),
        compiler_params=pltpu.CompilerParams(
            dimension_semantics=("arbitrary",), vmem_limit_bytes=VMEM_LIMIT_BYTES),
        name="adaln_mod",
    )(c, w_ada, b_ada.reshape(1, n))


def _mixer_kernel(tiles_per_row,
                  xn_ref, xp_ref, pos_ref, modn_ref, modp_ref, n1w_ref, invf_ref, win_ref, wdt_ref,
                  convw_ref, convb_ref, dtb_ref, alog_ref, dskip_ref, sinks_ref,
                  snw_ref, wout_ref, scan_ref, expand_ref,
                  o_ref,
                  raw_scr, h_scr, y_scr, q_scr, k_scr, v_scr, xbc_scr, dt_scr, state_scr,
                  mix_scr):
    ts = xn_ref.shape[1]
    n_chunks = ts // CHUNK
    s = pl.program_id(0)
    i = lax.rem(s, tiles_per_row)
    cur = lax.rem(s, 2)
    layout, n_slots, _ = _expand_layout()

    def norm_next():
        modn = modn_ref[0]
        return _rmsnorm_mod(xn_ref[0], n1w_ref[...], modn[0:1], modn[1:2]).astype(BF16)

    @pl.when(s == 0)
    def _():
        modp = modp_ref[0]
        h0 = _rmsnorm_mod(xp_ref[0], n1w_ref[...], modp[0:1], modp[1:2]).astype(BF16)
        h_scr[...] = h0
        for j in range(N_AHEAD_PIECES):
            raw_scr[j] = _dot(h0, win_ref[:, _piece(IN_XBC + j)])
        y_scr[...] = jnp.zeros(y_scr.shape, F32)
        mix_scr[...] = jnp.zeros(mix_scr.shape, BF16)

    @pl.when(i == 0)
    def _():
        k_scr[0:CHUNK, :] = jnp.zeros((CHUNK, KV_WIDTH), BF16)
        v_scr[0:CHUNK, :] = jnp.zeros((CHUNK, KV_WIDTH), BF16)
        xbc_scr[0:SUBLANES, :] = jnp.zeros((SUBLANES, CONV_CH), F32)
        state_scr[...] = jnp.zeros(state_scr.shape, F32)

    for j in range(N_AHEAD_PIECES):
        xbc_scr[SUBLANES:SUBLANES + ts, _piece(j)] = raw_scr[j]
    h = h_scr[...]
    pieces = {j: _dot(h, win_ref[:, _piece(j)]) for j in range(IN_Q, IN_XBC)}

    def proj(j):
        return pieces[j]

    half = HEAD_DIM // 2
    ang_d = pos_ref[0, 0].astype(F32) * invf_ref[...]
    lane_d = lax.broadcasted_iota(jnp.int32, ang_d.shape, 1)

    def spread(tbl):
        blocks = []
        for k in range(ROPE_PACK):
            t = pltpu.roll(tbl, LANES - half * k, axis=1) if k else tbl
            t = jnp.where(lane_d < half, t, pltpu.roll(t, half, axis=1))
            t = jnp.where(lane_d < 2 * half, t, pltpu.roll(t, 2 * half, axis=1))
            blocks.append(t)
        return jnp.concatenate(blocks, axis=0)

    cos = spread(jnp.cos(ang_d))
    sin = spread(jnp.sin(ang_d))
    lane = lax.broadcasted_iota(jnp.int32, (ts, LANES), 1)
    first_half = (lane % HEAD_DIM) < half
    sin_signed = jnp.where(first_half, -sin, sin)

    def rope(t, rows):
        lane_c = lax.broadcasted_iota(jnp.int32, t.shape, 1)
        rot = jnp.where((lane_c % HEAD_DIM) < half, pltpu.roll(t, LANES - half, axis=1),
                        pltpu.roll(t, half, axis=1))
        return t * cos[rows] + rot * sin_signed[rows]

    q_scale = 1.0 / math.sqrt(HEAD_DIM)

    def rope_rows(c):
        rows = slice(c * CHUNK, (c + 1) * CHUNK)
        for j in range(ATTN_WIDTH // LANES):
            piece, off = divmod(j * LANES, MXU_WIDTH)
            q_scr[rows, j * LANES:(j + 1) * LANES] = (
                rope(proj(IN_Q + piece)[rows, off:off + LANES], rows) * q_scale).astype(BF16)
        krows = slice(CHUNK + c * CHUNK, CHUNK + (c + 1) * CHUNK)
        k_scr[krows, :] = rope(proj(IN_KV)[rows, 0:KV_WIDTH], rows).astype(BF16)
        v_scr[krows, :] = proj(IN_KV)[rows, KV_WIDTH:2 * KV_WIDTH].astype(BF16)

    dt_scr[...] = _dot_nt(wdt_ref[...], h)

    h_scr[...] = norm_next()

    a_neg = -jnp.exp(alog_ref[...])
    causal = (lax.broadcasted_iota(jnp.int32, (CHUNK, CHUNK), 0)
              >= lax.broadcasted_iota(jnp.int32, (CHUNK, CHUNK), 1))
    x_cols = slice(0, SSM_WIDTH)
    bc_cols = slice(SSM_WIDTH, CONV_CH)
    group_head = lax.broadcasted_iota(jnp.int32, (CHUNK, GROUP_WIDTH), 1) // SSM_HEAD_DIM

    def conv_silu(row0, cols):
        win = xbc_scr[pl.ds(row0, SUBLANES + CHUNK), cols]
        conv = convb_ref[:, cols]
        for t in range(CONV_WIDTH):
            back = CONV_WIDTH - 1 - t
            tap = pltpu.roll(win, back, axis=0) if back else win
            conv = conv + convw_ref[t:t + 1, cols] * tap[SUBLANES:SUBLANES + CHUNK]
        return _silu(conv)

    n_ql = Q_PER_KV * CHUNK
    key_idx = lax.broadcasted_iota(jnp.int32, (CHUNK, n_ql), 0)
    qry_idx = lax.broadcasted_iota(jnp.int32, (CHUNK, n_ql), 1) % CHUNK
    upper = key_idx > qry_idx
    lane_head = lax.broadcasted_iota(jnp.int32, (1, n_ql), 1) // CHUNK

    def chunk_streams(c, tick):
        r0 = c * CHUNK
        prev_bias = jnp.where((i == 0) & (c == 0), MASK_VALUE, 0.0).astype(F32)

        def attn_stages(g):
            if g == 0:
                rope_rows(c)
            kband = k_scr[pl.ds(r0, 2 * CHUNK), :]
            vband = v_scr[pl.ds(r0, 2 * CHUNK), :]
            qblk = q_scr[pl.ds(r0, CHUNK), :]
            qg = jnp.concatenate(
                [qblk[:, (g * Q_PER_KV + j) * HEAD_DIM:(g * Q_PER_KV + j + 1) * HEAD_DIM]
                 for j in range(Q_PER_KV)], axis=0)
            s_t = _dot_nt(kband[:, g * HEAD_DIM:(g + 1) * HEAD_DIM], qg)
            yield
            sf = jnp.where(upper, s_t[:CHUNK] + prev_bias, s_t[CHUNK:])
            sink = jnp.zeros((1, n_ql), F32)
            for j in range(Q_PER_KV):
                sink = jnp.where(lane_head == j, sinks_ref[g * Q_PER_KV + j], sink)
            m = jnp.maximum(jnp.max(sf, axis=0, keepdims=True), sink)
            pf = jnp.exp(sf - m)
            denom = jnp.sum(pf, axis=0, keepdims=True) + jnp.exp(sink - m)
            p_t = jnp.concatenate([jnp.where(upper, pf, 0.0), jnp.where(upper, 0.0, pf)],
                                  axis=0).astype(BF16)
            yield
            tick()
            o_t = _dot_tn(vband[:, g * HEAD_DIM:(g + 1) * HEAD_DIM], p_t)
            o_t = o_t * (1.0 / denom)
            yield
            tiles = []
            for pp in range(Q_PER_KV // 2):
                blk = jnp.concatenate([o_t[:, (2 * pp) * CHUNK:(2 * pp + 1) * CHUNK],
                                       o_t[:, (2 * pp + 1) * CHUNK:(2 * pp + 2) * CHUNK]], axis=0)
                tiles.append(blk.T)
            mix_scr[cur, pl.ds(r0, CHUNK),
                    g * Q_PER_KV * HEAD_DIM:(g + 1) * Q_PER_KV * HEAD_DIM] = (
                jnp.concatenate(tiles, axis=1).astype(BF16))

        def ssd_stages():
            xs = conv_silu(r0, x_cols)
            bc = conv_silu(r0, bc_cols)
            yield
            dt = _softplus(dt_scr[0:SSM_HEADS, pl.ds(r0, CHUNK)] + dtb_ref[...])
            a = dt * a_neg
            pad = jnp.zeros((SUBLANES, CHUNK), F32)
            cs = _dot(jnp.concatenate(_split_bf16(a, 3) + [pad], axis=0).astype(BF16),
                      scan_ref[...])
            cs = cs[0:8] + cs[8:16] + cs[16:24]
            a_cs = cs[:, :CHUNK]
            a_tot = cs[:, CHUNK:]
            ea = jnp.exp(a_cs)
            w_end = dt * jnp.exp(a_tot - a_cs)
            rows = (_split_bf16(a_cs, 3) + _split_bf16(ea, 2) + _split_bf16(w_end, 2)
                    + _split_bf16(dt, 2))
            assert len(rows) == n_slots
            rows.append(jnp.zeros((LANES - n_slots * HEAD_SLOT, CHUNK), F32))
            packed = jnp.concatenate(rows, axis=0).T.astype(BF16)
            ex = _dot(packed, expand_ref[...])

            def ex_block(name):
                _, _, col0, width = layout[name]
                return ex[:, col0:col0 + width]

            acs_b = ex_block("acs")
            ea_e = ex_block("ea")
            xdt = (xs * ex_block("dt")).astype(BF16)
            xw = (xs * ex_block("w")).astype(BF16)
            yield
            ys = []
            for g in range(SSM_GROUPS):
                gs = slice(g * GROUP_WIDTH, (g + 1) * GROUP_WIDTH)
                bg_t = bc[:, g * D_STATE:(g + 1) * D_STATE].T.astype(BF16)
                cg = bc[:, (SSM_GROUPS + g) * D_STATE:(SSM_GROUPS + g + 1) * D_STATE].astype(BF16)
                cb = _dot(cg, bg_t)
                s_old = state_scr[g]
                y_off = _dot(cg, s_old.astype(BF16)) * ea_e[:, gs]
                ms = []
                xm = []
                for j in range(HEADS_PER_GROUP):
                    hh = g * HEADS_PER_GROUP + j
                    seg = acs_b[:, hh * LANES:(hh + 1) * LANES] - a_cs[hh:hh + 1, :]
                    dec = jnp.exp(jnp.where(causal, seg, MASK_VALUE))
                    ms.append((cb * dec).astype(BF16))
                    xm.append(jnp.where(group_head == j, xdt[:, gs], jnp.zeros((), BF16)))
                tick()
                y_diag = _dot(jnp.concatenate(ms, axis=1), jnp.concatenate(xm, axis=0))
                yield
                y_g = y_diag + y_off + xs[:, gs] * dskip_ref[:, gs]
                state_scr[g] = s_old * ea_e[CHUNK - 1:CHUNK, gs] + _dot(bg_t, xw[:, gs])
                y_g = y_g * _silu(proj(IN_Z + g)[r0:r0 + CHUNK, :])
                y_g = y_g * lax.rsqrt(jnp.mean(y_g * y_g, axis=-1, keepdims=True) + EPS)
                ys.append(y_g * snw_ref[:, gs])
                yield
            mix_scr[cur, pl.ds(r0, CHUNK), ATTN_WIDTH:] = jnp.concatenate(ys, axis=1).astype(BF16)

        return {"a0": attn_stages(0), "a1": attn_stages(1), "s": ssd_stages()}

    def filler_units():
        for j in range(max(N_AHEAD_PIECES, N_OUT_PIECES)):
            for rr in range(FILL_SPLIT):
                rows = slice(rr * ts // FILL_SPLIT, (rr + 1) * ts // FILL_SPLIT)
                if j < N_AHEAD_PIECES:
                    raw_scr[j, rows, :] = _dot(h_scr[rows, :], win_ref[:, _piece(IN_XBC + j)])
                    yield
                if j < N_OUT_PIECES:
                    y_scr[j, rows, :] = _dot(mix_scr[1 - cur, rows, :], wout_ref[:, _piece(j)])
                    yield

    fill = filler_units()
    def tick(units=1):
        for _ in range(units):
            next(fill, None)
    streams = {}
    for c in range(n_chunks):
        for name, gen in chunk_streams(c, tick).items():
            streams[c, name] = gen
    for key in _stage_order(n_chunks):
        next(streams[key], None)
    for stream in streams.values():
        assert next(stream, "done") == "done"
    for _ in fill:
        pass

    k_scr[0:CHUNK, :] = k_scr[ts:ts + CHUNK, :]
    v_scr[0:CHUNK, :] = v_scr[ts:ts + CHUNK, :]
    xbc_scr[0:SUBLANES, :] = xbc_scr[ts:ts + SUBLANES, :]

    gate1 = modp_ref[0][2:3]
    for j in range(N_OUT_PIECES):
        cols = slice(j * MXU_WIDTH, (j + 1) * MXU_WIDTH)
        o_ref[0, :, cols] = xp_ref[0, :, cols] + gate1[:, cols] * y_scr[j]


def _const_spec(shape):
    nd = len(shape)
    return pl.BlockSpec(shape, lambda *_: (0,) * nd, pipeline_mode=pl.Buffered(1))


def _mixer(x, pos4, mod3, n1w, invf, win, wdt, convw, convb, dtb, alog, dskip, sinks, snw, wout, scan,
           expand):
    b, s, d = x.shape
    ts = MIXER_TILE
    assert ts % CHUNK == 0 and s % ts == 0
    tpr = s // ts
    n_tiles = b * tpr

    def nxt(step):
        t = jnp.minimum(step + 1, n_tiles - 1)
        return t // tpr, t % tpr

    def prv(step):
        t = jnp.maximum(step - 1, 0)
        return t // tpr, t % tpr

    def cur(step):
        t = jnp.minimum(step, n_tiles - 1)
        return t // tpr, t % tpr

    in_specs = [
        pl.BlockSpec((1, ts, d), lambda st: (*nxt(st), 0)),
        pl.BlockSpec((1, ts, d), lambda st: (*prv(st), 0)),
        pl.BlockSpec((1, 1, ts // ROPE_PACK, LANES), lambda st: (*cur(st), 0, 0)),
        pl.BlockSpec((1, N_MOD, d), lambda st: (nxt(st)[0], 0, 0)),
        pl.BlockSpec((1, N_MOD, d), lambda st: (prv(st)[0], 0, 0)),
        _const_spec(n1w.shape), _const_spec(invf.shape), _const_spec(win.shape),
        _const_spec(wdt.shape),
        _const_spec(convw.shape), _const_spec(convb.shape), _const_spec(dtb.shape),
        _const_spec(alog.shape), _const_spec(dskip.shape),
        pl.BlockSpec(memory_space=pltpu.SMEM),
        _const_spec(snw.shape), _const_spec(wout.shape), _const_spec(scan.shape),
        _const_spec(expand.shape),
    ]
    scratch = [
        pltpu.VMEM((N_AHEAD_PIECES, ts, MXU_WIDTH), F32),
        pltpu.VMEM((ts, d), BF16),
        pltpu.VMEM((N_OUT_PIECES, ts, MXU_WIDTH), F32),
        pltpu.VMEM((ts, ATTN_WIDTH), BF16),
        pltpu.VMEM((CHUNK + ts, KV_WIDTH), BF16),
        pltpu.VMEM((CHUNK + ts, KV_WIDTH), BF16),
        pltpu.VMEM((SUBLANES + ts, CONV_CH), F32),
        pltpu.VMEM((2 * SUBLANES, ts), F32),
        pltpu.VMEM((SSM_GROUPS, D_STATE, GROUP_WIDTH), F32),
        pltpu.VMEM((2, ts, MIX_WIDTH), BF16),
    ]
    return pl.pallas_call(
        lambda *refs: _mixer_kernel(tpr, *refs),
        grid=(n_tiles + 1,),
        in_specs=in_specs,
        out_specs=pl.BlockSpec((1, ts, d), lambda st: (*prv(st), 0)),
        out_shape=jax.ShapeDtypeStruct((b, s, d), F32),
        scratch_shapes=scratch,
        compiler_params=pltpu.CompilerParams(
            dimension_semantics=("arbitrary",), vmem_limit_bytes=VMEM_LIMIT_BYTES),
        name="mixer",
    )(x, x, pos4, mod3, mod3, n1w, invf, win, wdt, convw, convb, dtb, alog, dskip, sinks, snw, wout,
      scan, expand)


def _ffn_kernel(x_ref, mod_ref, n2w_ref, wgu_ref, wd_ref, nfw_ref, o_ref):
    mod = mod_ref[0]
    shift2, scale2, gate2 = mod[3:4], mod[4:5], mod[5:6]
    x = x_ref[0]
    h = _rmsnorm_mod(x, n2w_ref[...], shift2, scale2).astype(BF16)
    g = _dot(h, wgu_ref[:, :D_FF])
    u = _dot(h, wgu_ref[:, D_FF:])
    act = (_silu(g) * u).astype(BF16)
    x2 = x + gate2 * _dot(act, wd_ref[...])
    ms = jnp.mean(x2 * x2, axis=-1, keepdims=True)
    o_ref[0] = x2 * lax.rsqrt(ms + EPS) * nfw_ref[...]


def _ffn(x1, mod3, n2w, wgu, wd, nfw):
    b, s, d = x1.shape
    tm = FFN_TILE
    return pl.pallas_call(
        _ffn_kernel,
        grid=(b, s // tm),
        in_specs=[
            pl.BlockSpec((1, tm, d), lambda bb, i: (bb, i, 0)),
            pl.BlockSpec((1, N_MOD, d), lambda bb, i: (bb, 0, 0)),
            _const_spec(n2w.shape), _const_spec(wgu.shape), _const_spec(wd.shape),
            _const_spec(nfw.shape),
        ],
        out_specs=pl.BlockSpec((1, tm, d), lambda bb, i: (bb, i, 0)),
        out_shape=jax.ShapeDtypeStruct((b, s, d), F32),
        compiler_params=pltpu.CompilerParams(
            dimension_semantics=("arbitrary", "arbitrary"), vmem_limit_bytes=VMEM_LIMIT_BYTES),
        name="ffn",
    )(x1, mod3, n2w, wgu, wd, nfw)


def _head_rows(v):
    return jnp.broadcast_to(v.astype(F32)[:, None], (SSM_HEADS, LANES))


def kernel(x, c, positions, w_ada, b_ada, norm1_w, w_in, conv_w, conv_b, dt_bias, a_log, d_skip,
           attn_sinks, ssm_norm_w, w_out, norm2_w, w_gate_up, w_down, final_norm_w):
    b, s, d = x.shape
    depth = w_ada.shape[0]
    half = HEAD_DIM // 2
    inv_freq = ROPE_THETA ** (-jnp.arange(half, dtype=F32) / half)
    invf = jnp.tile(inv_freq, LANES // half).reshape(1, LANES)
    scan = _scan_matrix()
    expand = _expand_matrix()
    ts = MIXER_TILE
    pos4 = positions.reshape(b, s // ts, ROPE_PACK, ts // ROPE_PACK).transpose(0, 1, 3, 2)
    pos4 = jnp.repeat(pos4, half, axis=-1)
    assert depth == 1, "the final norm is fused into the (single) layer's ffn call"
    layer = 0
    mod3 = _adaln_mod(c, w_ada[layer], b_ada[layer]).reshape(b, N_MOD, d)
    n_main = N_IN_PIECES * MXU_WIDTH
    assert w_in.shape[2] == n_main + SSM_HEADS
    w_main = w_in[layer][:, :n_main].astype(BF16)
    wdt = jnp.pad(w_in[layer][:, n_main:].T.astype(BF16),
                  ((0, 2 * SUBLANES - SSM_HEADS), (0, 0)))
    x1 = _mixer(
        x, pos4, mod3, norm1_w[layer].reshape(1, d), invf, w_main, wdt,
        conv_w[layer], conv_b[layer].reshape(1, CONV_CH),
        _head_rows(dt_bias[layer]), _head_rows(a_log[layer]),
        jnp.repeat(d_skip[layer].astype(F32), SSM_HEAD_DIM).reshape(1, SSM_WIDTH),
        attn_sinks[layer].astype(F32),
        ssm_norm_w[layer].reshape(1, SSM_WIDTH), w_out[layer].astype(BF16), scan, expand)
    return _ffn(x1, mod3, norm2_w[layer].reshape(1, d), w_gate_up[layer].astype(BF16),
                w_down[layer].astype(BF16), final_norm_w.reshape(1, d))
```

```python
import math

import jax
import jax.numpy as jnp
import numpy as np
from jax import lax
from jax.experimental import pallas as pl
from jax.experimental.pallas import tpu as pltpu

D_MODEL = 1024
HEAD_DIM = 64
N_Q_HEADS = 8
N_KV_HEADS = 2
Q_PER_KV = N_Q_HEADS // N_KV_HEADS
ATTN_WIDTH = N_Q_HEADS * HEAD_DIM
KV_WIDTH = N_KV_HEADS * HEAD_DIM
WINDOW = 128
ROPE_THETA = 10000.0
SSM_HEADS = 8
SSM_HEAD_DIM = 64
SSM_WIDTH = SSM_HEADS * SSM_HEAD_DIM
SSM_GROUPS = 2
HEADS_PER_GROUP = SSM_HEADS // SSM_GROUPS
GROUP_WIDTH = HEADS_PER_GROUP * SSM_HEAD_DIM
D_STATE = 128
CONV_WIDTH = 4
CHUNK = 128
CONV_CH = SSM_WIDTH + 2 * SSM_GROUPS * D_STATE
MIX_WIDTH = ATTN_WIDTH + SSM_WIDTH
D_FF = 2816
N_MOD = 6
EPS = 1e-6

LANES = 128
SUBLANES = 8
MXU_WIDTH = 256
HEAD_SLOT = 8
ROPE_PACK = LANES // (HEAD_DIM // 2)
MASK_VALUE = -1e30
VMEM_LIMIT_BYTES = 56 * 1024 * 1024

MIXER_TILE = 512
FFN_TILE = 1024
MOD_TILE = 2048

F32 = jnp.float32
BF16 = jnp.bfloat16

IN_Q = 0
IN_KV = IN_Q + ATTN_WIDTH // MXU_WIDTH
IN_Z = IN_KV + 2 * KV_WIDTH // MXU_WIDTH
IN_XBC = IN_Z + SSM_WIDTH // MXU_WIDTH
N_AHEAD_PIECES = CONV_CH // MXU_WIDTH
N_IN_PIECES = IN_XBC + N_AHEAD_PIECES
N_OUT_PIECES = D_MODEL // MXU_WIDTH
FILL_SPLIT = 1
assert GROUP_WIDTH == MXU_WIDTH

ATTN_STAGES = 4
SSD_STAGES = 9
_SSD_NEEDS_PREV_PAST = {3: 4, 6: 7}


def _stage_order(n_chunks):
    prog = {(k, name): 0 for k in range(n_chunks) for name in ("a0", "a1", "s")}
    order = []
    n_stages = {"a0": ATTN_STAGES, "a1": ATTN_STAGES, "s": SSD_STAGES}
    while any(v < n_stages[name] for (_, name), v in prog.items()):
        for k in range(n_chunks):
            for name in ("a0", "a1"):
                if prog[k, name] < ATTN_STAGES:
                    prog[k, name] += 1
                    order.append((k, name))
        for k in range(n_chunks):
            stage = prog[k, "s"]
            need = _SSD_NEEDS_PREV_PAST.get(stage) if k else None
            if stage < SSD_STAGES and (need is None or prog[k - 1, "s"] >= need):
                prog[k, "s"] += 1
                order.append((k, "s"))
    assert all(v == (SSD_STAGES if name == "s" else ATTN_STAGES) for (_, name), v in prog.items())
    return order

_EXPAND_ITEMS = (("acs", 3, LANES), ("ea", 2, SSM_HEAD_DIM), ("w", 2, SSM_HEAD_DIM),
                 ("dt", 2, SSM_HEAD_DIM))


def _expand_layout():
    slot = 0
    col = 0
    layout = {}
    for name, parts, rep in _EXPAND_ITEMS:
        layout[name] = (slot, parts, col, rep * SSM_HEADS)
        slot += parts
        col += rep * SSM_HEADS
    return layout, slot, col


def _expand_matrix():
    layout, nslots, ncols = _expand_layout()
    assert nslots * HEAD_SLOT <= LANES
    e = np.zeros((LANES, ncols), np.float32)
    for name, parts, rep in _EXPAND_ITEMS:
        slot0, _, col0, _ = layout[name]
        for p in range(parts):
            for hh in range(SSM_HEADS):
                e[(slot0 + p) * HEAD_SLOT + hh, col0 + hh * rep:col0 + (hh + 1) * rep] = 1.0
    return jnp.asarray(e, BF16)


def _scan_matrix():
    t = np.arange(CHUNK)
    upper = (t[:, None] <= t[None, :]).astype(np.float32)
    return jnp.asarray(np.concatenate([upper, np.ones((CHUNK, CHUNK), np.float32)], axis=1), BF16)


def _silu(x):
    hx = 0.5 * x
    return hx + hx * jnp.tanh(hx)


def _softplus(x):
    return jnp.maximum(x, 0.0) + jnp.log1p(jnp.exp(-jnp.abs(x)))


def _dot(a, b):
    return jnp.dot(a, b, preferred_element_type=F32)


def _dot_nt(a, b):
    return lax.dot_general(a, b, (((1,), (1,)), ((), ())), preferred_element_type=F32)


def _dot_tn(a, b):
    return lax.dot_general(a, b, (((0,), (0,)), ((), ())), preferred_element_type=F32)


def _rmsnorm_mod(x, norm_w, shift, scale):
    ms = jnp.mean(x * x, axis=-1, keepdims=True)
    return x * lax.rsqrt(ms + EPS) * (norm_w * (1.0 + scale)) + shift


def _piece(j):
    return slice(j * MXU_WIDTH, (j + 1) * MXU_WIDTH)


def _split_bf16(v, parts):
    out = []
    r = v
    for _ in range(parts):
        p = r.astype(BF16).astype(F32)
        out.append(p)
        r = r - p
    return out


def _mod_kernel(c_ref, w_ref, b_ref, o_ref):
    sc = _silu(c_ref[...]).astype(BF16)
    o_ref[...] = _dot(sc, w_ref[...].astype(BF16)) + b_ref[...]


def _adaln_mod(c, w_ada, b_ada):
    b, d = c.shape
    n = w_ada.shape[1]
    return pl.pallas_call(
        _mod_kernel,
        grid=(n // MOD_TILE,),
        in_specs=[
            pl.BlockSpec((b, d), lambda j: (0, 0)),
            pl.BlockSpec((d, MOD_TILE), lambda j: (0, j)),
            pl.BlockSpec((1, MOD_TILE), lambda j: (0, j)),
        ],
        out_specs=pl.BlockSpec((b, MOD_TILE), lambda j: (0, j)),
        out_shape=jax.ShapeDtypeStruct((b, n), F32),
        compiler_params=pltpu.CompilerParams(
            dimension_semantics=("arbitrary",), vmem_limit_bytes=VMEM_LIMIT_BYTES),
        name="adaln_mod",
    )(c, w_ada, b_ada.reshape(1, n))


def _mixer_kernel(tiles_per_row,
                  xn_ref, xp_ref, pos_ref, modn_ref, modp_ref, n1w_ref, invf_ref, win_ref, wdt_ref,
                  convw_ref, convb_ref, dtb_ref, alog_ref, dskip_ref, sinks_ref,
                  snw_ref, wout_ref, scan_ref, expand_ref,
                  o_ref,
                  raw_scr, h_scr, y_scr, q_scr, k_scr, v_scr, xbc_scr, dt_scr, state_scr,
                  mix_scr):
    ts = xn_ref.shape[1]
    n_chunks = ts // CHUNK
    s = pl.program_id(0)
    i = lax.rem(s, tiles_per_row)
    cur = lax.rem(s, 2)
    layout, n_slots, _ = _expand_layout()

    def norm_next():
        modn = modn_ref[0]
        return _rmsnorm_mod(xn_ref[0], n1w_ref[...], modn[0:1], modn[1:2]).astype(BF16)

    @pl.when(s == 0)
    def _():
        modp = modp_ref[0]
        h0 = _rmsnorm_mod(xp_ref[0], n1w_ref[...], modp[0:1], modp[1:2]).astype(BF16)
        h_scr[...] = h0
        for j in range(N_AHEAD_PIECES):
            raw_scr[j] = _dot(h0, win_ref[:, _piece(IN_XBC + j)])
        y_scr[...] = jnp.zeros(y_scr.shape, F32)
        mix_scr[...] = jnp.zeros(mix_scr.shape, BF16)

    @pl.when(i == 0)
    def _():
        k_scr[0:CHUNK, :] = jnp.zeros((CHUNK, KV_WIDTH), BF16)
        v_scr[0:CHUNK, :] = jnp.zeros((CHUNK, KV_WIDTH), BF16)
        xbc_scr[0:SUBLANES, :] = jnp.zeros((SUBLANES, CONV_CH), F32)
        state_scr[...] = jnp.zeros(state_scr.shape, F32)

    for j in range(N_AHEAD_PIECES):
        xbc_scr[SUBLANES:SUBLANES + ts, _piece(j)] = raw_scr[j]
    h = h_scr[...]
    pieces = {j: _dot(h, win_ref[:, _piece(j)]) for j in range(IN_Q, IN_XBC)}

    def proj(j):
        return pieces[j]

    half = HEAD_DIM // 2
    ang_d = pos_ref[0, 0].astype(F32) * invf_ref[...]
    lane_d = lax.broadcasted_iota(jnp.int32, ang_d.shape, 1)

    def spread(tbl):
        blocks = []
        for k in range(ROPE_PACK):
            t = pltpu.roll(tbl, LANES - half * k, axis=1) if k else tbl
            t = jnp.where(lane_d < half, t, pltpu.roll(t, half, axis=1))
            t = jnp.where(lane_d < 2 * half, t, pltpu.roll(t, 2 * half, axis=1))
            blocks.append(t)
        return jnp.concatenate(blocks, axis=0)

    cos = spread(jnp.cos(ang_d))
    sin = spread(jnp.sin(ang_d))
    lane = lax.broadcasted_iota(jnp.int32, (ts, LANES), 1)
    first_half = (lane % HEAD_DIM) < half
    sin_signed = jnp.where(first_half, -sin, sin)

    def rope(t, rows):
        lane_c = lax.broadcasted_iota(jnp.int32, t.shape, 1)
        rot = jnp.where((lane_c % HEAD_DIM) < half, pltpu.roll(t, LANES - half, axis=1),
                        pltpu.roll(t, half, axis=1))
        return t * cos[rows] + rot * sin_signed[rows]

    q_scale = 1.0 / math.sqrt(HEAD_DIM)

    def rope_rows(c):
        rows = slice(c * CHUNK, (c + 1) * CHUNK)
        for j in range(ATTN_WIDTH // LANES):
            piece, off = divmod(j * LANES, MXU_WIDTH)
            q_scr[rows, j * LANES:(j + 1) * LANES] = (
                rope(proj(IN_Q + piece)[rows, off:off + LANES], rows) * q_scale).astype(BF16)
        krows = slice(CHUNK + c * CHUNK, CHUNK + (c + 1) * CHUNK)
        k_scr[krows, :] = rope(proj(IN_KV)[rows, 0:KV_WIDTH], rows).astype(BF16)
        v_scr[krows, :] = proj(IN_KV)[rows, KV_WIDTH:2 * KV_WIDTH].astype(BF16)

    dt_scr[...] = _dot_nt(wdt_ref[...], h)

    h_scr[...] = norm_next()

    a_neg = -jnp.exp(alog_ref[...])
    causal = (lax.broadcasted_iota(jnp.int32, (CHUNK, CHUNK), 0)
              >= lax.broadcasted_iota(jnp.int32, (CHUNK, CHUNK), 1))
    x_cols = slice(0, SSM_WIDTH)
    bc_cols = slice(SSM_WIDTH, CONV_CH)
    group_head = lax.broadcasted_iota(jnp.int32, (CHUNK, GROUP_WIDTH), 1) // SSM_HEAD_DIM

    def conv_silu(row0, cols):
        win = xbc_scr[pl.ds(row0, SUBLANES + CHUNK), cols]
        conv = convb_ref[:, cols]
        for t in range(CONV_WIDTH):
            back = CONV_WIDTH - 1 - t
            tap = pltpu.roll(win, back, axis=0) if back else win
            conv = conv + convw_ref[t:t + 1, cols] * tap[SUBLANES:SUBLANES + CHUNK]
        return _silu(conv)

    n_ql = Q_PER_KV * CHUNK
    key_idx = lax.broadcasted_iota(jnp.int32, (CHUNK, n_ql), 0)
    qry_idx = lax.broadcasted_iota(jnp.int32, (CHUNK, n_ql), 1) % CHUNK
    upper = key_idx > qry_idx
    lane_head = lax.broadcasted_iota(jnp.int32, (1, n_ql), 1) // CHUNK

    scalars = {}

    def ssd_scalars(c):
        if not scalars:
            dts, lhs = [], []
            pad = jnp.zeros((SUBLANES, CHUNK), F32)
            for cc in range(n_chunks):
                dt = _softplus(dt_scr[0:SSM_HEADS, cc * CHUNK:(cc + 1) * CHUNK] + dtb_ref[...])
                dts.append(dt)
                lhs += _split_bf16(dt * a_neg, 3) + [pad]
            cs_all = _dot(jnp.concatenate(lhs, axis=0).astype(BF16), scan_ref[...])
            a_css, packed = [], []
            for cc, dt in enumerate(dts):
                cs = cs_all[4 * SUBLANES * cc:4 * SUBLANES * (cc + 1)]
                cs = cs[0:8] + cs[8:16] + cs[16:24]
                a_cs = cs[:, :CHUNK]
                a_tot = cs[:, CHUNK:]
                ea = jnp.exp(a_cs)
                w_end = dt * jnp.exp(a_tot - a_cs)
                rows = (_split_bf16(a_cs, 3) + _split_bf16(ea, 2) + _split_bf16(w_end, 2)
                        + _split_bf16(dt, 2))
                assert len(rows) == n_slots
                rows.append(jnp.zeros((LANES - n_slots * HEAD_SLOT, CHUNK), F32))
                packed.append(jnp.concatenate(rows, axis=0).T.astype(BF16))
                a_css.append(a_cs)
            scalars["a_cs"] = a_css
            scalars["ex"] = _dot(jnp.concatenate(packed, axis=0), expand_ref[...])
        return scalars["a_cs"][c], scalars["ex"][c * CHUNK:(c + 1) * CHUNK]

    def chunk_streams(c, tick):
        r0 = c * CHUNK
        prev_bias = jnp.where((i == 0) & (c == 0), MASK_VALUE, 0.0).astype(F32)

        def attn_stages(g):
            if g == 0:
                rope_rows(c)
            kband = k_scr[pl.ds(r0, 2 * CHUNK), :]
            vband = v_scr[pl.ds(r0, 2 * CHUNK), :]
            qblk = q_scr[pl.ds(r0, CHUNK), :]
            qg = jnp.concatenate(
                [qblk[:, (g * Q_PER_KV + j) * HEAD_DIM:(g * Q_PER_KV + j + 1) * HEAD_DIM]
                 for j in range(Q_PER_KV)], axis=0)
            s_t = _dot_nt(kband[:, g * HEAD_DIM:(g + 1) * HEAD_DIM], qg)
            yield
            sf = jnp.where(upper, s_t[:CHUNK] + prev_bias, s_t[CHUNK:])
            sink = jnp.zeros((1, n_ql), F32)
            for j in range(Q_PER_KV):
                sink = jnp.where(lane_head == j, sinks_ref[g * Q_PER_KV + j], sink)
            m = jnp.maximum(jnp.max(sf, axis=0, keepdims=True), sink)
            pf = jnp.exp(sf - m)
            denom = jnp.sum(pf, axis=0, keepdims=True) + jnp.exp(sink - m)
            p_t = jnp.concatenate([jnp.where(upper, pf, 0.0), jnp.where(upper, 0.0, pf)],
                                  axis=0).astype(BF16)
            yield
            tick()
            o_t = _dot_tn(vband[:, g * HEAD_DIM:(g + 1) * HEAD_DIM], p_t)
            o_t = o_t * (1.0 / denom)
            yield
            tiles = []
            for pp in range(Q_PER_KV // 2):
                blk = jnp.concatenate([o_t[:, (2 * pp) * CHUNK:(2 * pp + 1) * CHUNK],
                                       o_t[:, (2 * pp + 1) * CHUNK:(2 * pp + 2) * CHUNK]], axis=0)
                tiles.append(blk.T)
            mix_scr[cur, pl.ds(r0, CHUNK),
                    g * Q_PER_KV * HEAD_DIM:(g + 1) * Q_PER_KV * HEAD_DIM] = (
                jnp.concatenate(tiles, axis=1).astype(BF16))

        def ssd_stages():
            xs = conv_silu(r0, x_cols)
            bc = conv_silu(r0, bc_cols)
            yield
            a_cs, ex = ssd_scalars(c)

            def ex_block(name):
                _, _, col0, width = layout[name]
                return ex[:, col0:col0 + width]

            acs_b = ex_block("acs")
            ea_e = ex_block("ea")
            xdt = (xs * ex_block("dt")).astype(BF16)
            xw = (xs * ex_block("w")).astype(BF16)
            yield
            ys = []
            for g in range(SSM_GROUPS):
                gs = slice(g * GROUP_WIDTH, (g + 1) * GROUP_WIDTH)
                bg_t = bc[:, g * D_STATE:(g + 1) * D_STATE].T.astype(BF16)
                cg = bc[:, (SSM_GROUPS + g) * D_STATE:(SSM_GROUPS + g + 1) * D_STATE].astype(BF16)
                cb = _dot(cg, bg_t)
                ms = []
                xm = []
                for j in range(HEADS_PER_GROUP):
                    hh = g * HEADS_PER_GROUP + j
                    seg = acs_b[:, hh * LANES:(hh + 1) * LANES] - a_cs[hh:hh + 1, :]
                    dec = jnp.exp(jnp.where(causal, seg, MASK_VALUE))
                    ms.append((cb * dec).astype(BF16))
                    xm.append(jnp.where(group_head == j, xdt[:, gs], jnp.zeros((), BF16)))
                tick()
                y_diag = _dot(jnp.concatenate(ms, axis=1), jnp.concatenate(xm, axis=0))
                yield
                s_old = state_scr[g]
                y_off = _dot(cg, s_old.astype(BF16))
                state_scr[g] = s_old * ea_e[CHUNK - 1:CHUNK, gs] + _dot(bg_t, xw[:, gs])
                yield
                y_g = y_diag + y_off * ea_e[:, gs] + xs[:, gs] * dskip_ref[:, gs]
                y_g = y_g * _silu(proj(IN_Z + g)[r0:r0 + CHUNK, :])
                y_g = y_g * lax.rsqrt(jnp.mean(y_g * y_g, axis=-1, keepdims=True) + EPS)
                ys.append(y_g * snw_ref[:, gs])
                yield
            mix_scr[cur, pl.ds(r0, CHUNK), ATTN_WIDTH:] = jnp.concatenate(ys, axis=1).astype(BF16)

        return {"a0": attn_stages(0), "a1": attn_stages(1), "s": ssd_stages()}

    def filler_units():
        for j in range(max(N_AHEAD_PIECES, N_OUT_PIECES)):
            for rr in range(FILL_SPLIT):
                rows = slice(rr * ts // FILL_SPLIT, (rr + 1) * ts // FILL_SPLIT)
                if j < N_AHEAD_PIECES:
                    raw_scr[j, rows, :] = _dot(h_scr[rows, :], win_ref[:, _piece(IN_XBC + j)])
                    yield
                if j < N_OUT_PIECES:
                    y_scr[j, rows, :] = _dot(mix_scr[1 - cur, rows, :], wout_ref[:, _piece(j)])
                    yield

    fill = filler_units()
    def tick(units=1):
        for _ in range(units):
            next(fill, None)
    streams = {}
    for c in range(n_chunks):
        for name, gen in chunk_streams(c, tick).items():
            streams[c, name] = gen
    for key in _stage_order(n_chunks):
        next(streams[key], None)
    for stream in streams.values():
        assert next(stream, "done") == "done"
    for _ in fill:
        pass

    k_scr[0:CHUNK, :] = k_scr[ts:ts + CHUNK, :]
    v_scr[0:CHUNK, :] = v_scr[ts:ts + CHUNK, :]
    xbc_scr[0:SUBLANES, :] = xbc_scr[ts:ts + SUBLANES, :]

    gate1 = modp_ref[0][2:3]
    for j in range(N_OUT_PIECES):
        cols = slice(j * MXU_WIDTH, (j + 1) * MXU_WIDTH)
        o_ref[0, :, cols] = xp_ref[0, :, cols] + gate1[:, cols] * y_scr[j]


def _const_spec(shape):
    nd = len(shape)
    return pl.BlockSpec(shape, lambda *_: (0,) * nd, pipeline_mode=pl.Buffered(1))


def _mixer(x, pos4, mod3, n1w, invf, win, wdt, convw, convb, dtb, alog, dskip, sinks, snw, wout, scan,
           expand):
    b, s, d = x.shape
    ts = MIXER_TILE
    assert ts % CHUNK == 0 and s % ts == 0
    tpr = s // ts
    n_tiles = b * tpr

    def nxt(step):
        t = jnp.minimum(step + 1, n_tiles - 1)
        return t // tpr, t % tpr

    def prv(step):
        t = jnp.maximum(step - 1, 0)
        return t // tpr, t % tpr

    def cur(step):
        t = jnp.minimum(step, n_tiles - 1)
        return t // tpr, t % tpr

    in_specs = [
        pl.BlockSpec((1, ts, d), lambda st: (*nxt(st), 0)),
        pl.BlockSpec((1, ts, d), lambda st: (*prv(st), 0)),
        pl.BlockSpec((1, 1, ts // ROPE_PACK, LANES), lambda st: (*cur(st), 0, 0)),
        pl.BlockSpec((1, N_MOD, d), lambda st: (nxt(st)[0], 0, 0)),
        pl.BlockSpec((1, N_MOD, d), lambda st: (prv(st)[0], 0, 0)),
        _const_spec(n1w.shape), _const_spec(invf.shape), _const_spec(win.shape),
        _const_spec(wdt.shape),
        _const_spec(convw.shape), _const_spec(convb.shape), _const_spec(dtb.shape),
        _const_spec(alog.shape), _const_spec(dskip.shape),
        pl.BlockSpec(memory_space=pltpu.SMEM),
        _const_spec(snw.shape), _const_spec(wout.shape), _const_spec(scan.shape),
        _const_spec(expand.shape),
    ]
    scratch = [
        pltpu.VMEM((N_AHEAD_PIECES, ts, MXU_WIDTH), F32),
        pltpu.VMEM((ts, d), BF16),
        pltpu.VMEM((N_OUT_PIECES, ts, MXU_WIDTH), F32),
        pltpu.VMEM((ts, ATTN_WIDTH), BF16),
        pltpu.VMEM((CHUNK + ts, KV_WIDTH), BF16),
        pltpu.VMEM((CHUNK + ts, KV_WIDTH), BF16),
        pltpu.VMEM((SUBLANES + ts, CONV_CH), F32),
        pltpu.VMEM((2 * SUBLANES, ts), F32),
        pltpu.VMEM((SSM_GROUPS, D_STATE, GROUP_WIDTH), F32),
        pltpu.VMEM((2, ts, MIX_WIDTH), BF16),
    ]
    return pl.pallas_call(
        lambda *refs: _mixer_kernel(tpr, *refs),
        grid=(n_tiles + 1,),
        in_specs=in_specs,
        out_specs=pl.BlockSpec((1, ts, d), lambda st: (*prv(st), 0)),
        out_shape=jax.ShapeDtypeStruct((b, s, d), F32),
        scratch_shapes=scratch,
        compiler_params=pltpu.CompilerParams(
            dimension_semantics=("arbitrary",), vmem_limit_bytes=VMEM_LIMIT_BYTES),
        name="mixer",
    )(x, x, pos4, mod3, mod3, n1w, invf, win, wdt, convw, convb, dtb, alog, dskip, sinks, snw, wout,
      scan, expand)


def _ffn_kernel(x_ref, mod_ref, n2w_ref, wgu_ref, wd_ref, nfw_ref, o_ref):
    mod = mod_ref[0]
    shift2, scale2, gate2 = mod[3:4], mod[4:5], mod[5:6]
    x = x_ref[0]
    h = _rmsnorm_mod(x, n2w_ref[...], shift2, scale2).astype(BF16)
    g = _dot(h, wgu_ref[:, :D_FF])
    u = _dot(h, wgu_ref[:, D_FF:])
    act = (_silu(g) * u).astype(BF16)
    x2 = x + gate2 * _dot(act, wd_ref[...])
    ms = jnp.mean(x2 * x2, axis=-1, keepdims=True)
    o_ref[0] = x2 * lax.rsqrt(ms + EPS) * nfw_ref[...]


def _ffn(x1, mod3, n2w, wgu, wd, nfw):
    b, s, d = x1.shape
    tm = FFN_TILE
    return pl.pallas_call(
        _ffn_kernel,
        grid=(b, s // tm),
        in_specs=[
            pl.BlockSpec((1, tm, d), lambda bb, i: (bb, i, 0)),
            pl.BlockSpec((1, N_MOD, d), lambda bb, i: (bb, 0, 0)),
            _const_spec(n2w.shape), _const_spec(wgu.shape), _const_spec(wd.shape),
            _const_spec(nfw.shape),
        ],
        out_specs=pl.BlockSpec((1, tm, d), lambda bb, i: (bb, i, 0)),
        out_shape=jax.ShapeDtypeStruct((b, s, d), F32),
        compiler_params=pltpu.CompilerParams(
            dimension_semantics=("arbitrary", "arbitrary"), vmem_limit_bytes=VMEM_LIMIT_BYTES),
        name="ffn",
    )(x1, mod3, n2w, wgu, wd, nfw)


def _head_rows(v):
    return jnp.broadcast_to(v.astype(F32)[:, None], (SSM_HEADS, LANES))


def kernel(x, c, positions, w_ada, b_ada, norm1_w, w_in, conv_w, conv_b, dt_bias, a_log, d_skip,
           attn_sinks, ssm_norm_w, w_out, norm2_w, w_gate_up, w_down, final_norm_w):
    b, s, d = x.shape
    depth = w_ada.shape[0]
    half = HEAD_DIM // 2
    inv_freq = ROPE_THETA ** (-jnp.arange(half, dtype=F32) / half)
    invf = jnp.tile(inv_freq, LANES // half).reshape(1, LANES)
    scan = _scan_matrix()
    expand = _expand_matrix()
    ts = MIXER_TILE
    pos4 = positions.reshape(b, s // ts, ROPE_PACK, ts // ROPE_PACK).transpose(0, 1, 3, 2)
    pos4 = jnp.repeat(pos4, half, axis=-1)
    assert depth == 1, "the final norm is fused into the (single) layer's ffn call"
    layer = 0
    mod3 = _adaln_mod(c, w_ada[layer], b_ada[layer]).reshape(b, N_MOD, d)
    n_main = N_IN_PIECES * MXU_WIDTH
    assert w_in.shape[2] == n_main + SSM_HEADS
    w_main = w_in[layer][:, :n_main].astype(BF16)
    wdt = jnp.pad(w_in[layer][:, n_main:].T.astype(BF16),
                  ((0, 2 * SUBLANES - SSM_HEADS), (0, 0)))
    x1 = _mixer(
        x, pos4, mod3, norm1_w[layer].reshape(1, d), invf, w_main, wdt,
        conv_w[layer], conv_b[layer].reshape(1, CONV_CH),
        _head_rows(dt_bias[layer]), _head_rows(a_log[layer]),
        jnp.repeat(d_skip[layer].astype(F32), SSM_HEAD_DIM).reshape(1, SSM_WIDTH),
        attn_sinks[layer].astype(F32),
        ssm_norm_w[layer].reshape(1, SSM_WIDTH), w_out[layer].astype(BF16), scan, expand)
    return _ffn(x1, mod3, norm2_w[layer].reshape(1, d), w_gate_up[layer].astype(BF16),
                w_down[layer].astype(BF16), final_norm_w.reshape(1, d))
```

```python
import math

import jax
import jax.numpy as jnp
import numpy as np
from jax import lax
from jax.experimental import pallas as pl
from jax.experimental.pallas import tpu as pltpu

D_MODEL = 1024
HEAD_DIM = 64
N_Q_HEADS = 8
N_KV_HEADS = 2
Q_PER_KV = N_Q_HEADS // N_KV_HEADS
ATTN_WIDTH = N_Q_HEADS * HEAD_DIM
KV_WIDTH = N_KV_HEADS * HEAD_DIM
WINDOW = 128
ROPE_THETA = 10000.0
SSM_HEADS = 8
SSM_HEAD_DIM = 64
SSM_WIDTH = SSM_HEADS * SSM_HEAD_DIM
SSM_GROUPS = 2
HEADS_PER_GROUP = SSM_HEADS // SSM_GROUPS
GROUP_WIDTH = HEADS_PER_GROUP * SSM_HEAD_DIM
D_STATE = 128
CONV_WIDTH = 4
CHUNK = 128
CONV_CH = SSM_WIDTH + 2 * SSM_GROUPS * D_STATE
MIX_WIDTH = ATTN_WIDTH + SSM_WIDTH
D_FF = 2816
N_MOD = 6
EPS = 1e-6

LANES = 128
SUBLANES = 8
MXU_WIDTH = 256
HEAD_SLOT = 8
ROPE_PACK = LANES // (HEAD_DIM // 2)
MASK_VALUE = -1e30
VMEM_LIMIT_BYTES = 56 * 1024 * 1024

MIXER_TILE = 512
FFN_TILE = 1024
MOD_TILE = 2048

F32 = jnp.float32
BF16 = jnp.bfloat16

IN_Q = 0
IN_KV = IN_Q + ATTN_WIDTH // MXU_WIDTH
IN_Z = IN_KV + 2 * KV_WIDTH // MXU_WIDTH
IN_XBC = IN_Z + SSM_WIDTH // MXU_WIDTH
N_AHEAD_PIECES = CONV_CH // MXU_WIDTH
N_IN_PIECES = IN_XBC + N_AHEAD_PIECES
N_OUT_PIECES = D_MODEL // MXU_WIDTH
assert GROUP_WIDTH == MXU_WIDTH
assert WINDOW == CHUNK

ATTN_STAGES = 4
SSD_STAGES = 9
_SSD_NEEDS_PREV_PAST = {3: 4, 6: 7}


def _stage_order(n_chunks):
    prog = {(k, name): 0 for k in range(n_chunks) for name in ("a0", "a1", "s")}
    order = []
    n_stages = {"a0": ATTN_STAGES, "a1": ATTN_STAGES, "s": SSD_STAGES}
    while any(v < n_stages[name] for (_, name), v in prog.items()):
        for k in range(n_chunks):
            for name in ("a0", "a1"):
                if prog[k, name] < ATTN_STAGES:
                    prog[k, name] += 1
                    order.append((k, name))
        for k in range(n_chunks):
            stage = prog[k, "s"]
            need = _SSD_NEEDS_PREV_PAST.get(stage) if k else None
            if stage < SSD_STAGES and (need is None or prog[k - 1, "s"] >= need):
                prog[k, "s"] += 1
                order.append((k, "s"))
    assert all(v == (SSD_STAGES if name == "s" else ATTN_STAGES) for (_, name), v in prog.items())
    return order

_EXPAND_ITEMS = (("acs", 3, LANES), ("ea", 2, SSM_HEAD_DIM), ("w", 2, SSM_HEAD_DIM),
                 ("dt", 2, SSM_HEAD_DIM))


def _expand_layout():
    slot = 0
    col = 0
    layout = {}
    for name, parts, rep in _EXPAND_ITEMS:
        layout[name] = (slot, parts, col, rep * SSM_HEADS)
        slot += parts
        col += rep * SSM_HEADS
    return layout, slot, col


def _expand_matrix():
    layout, nslots, ncols = _expand_layout()
    assert nslots * HEAD_SLOT <= LANES
    e = np.zeros((LANES, ncols), np.float32)
    for name, parts, rep in _EXPAND_ITEMS:
        slot0, _, col0, _ = layout[name]
        for p in range(parts):
            for hh in range(SSM_HEADS):
                e[(slot0 + p) * HEAD_SLOT + hh, col0 + hh * rep:col0 + (hh + 1) * rep] = 1.0
    return jnp.asarray(e, BF16)


def _scan_matrix():
    t = np.arange(CHUNK)
    upper = (t[:, None] <= t[None, :]).astype(np.float32)
    return jnp.asarray(np.concatenate([upper, np.ones((CHUNK, CHUNK), np.float32)], axis=1), BF16)


def _silu(x):
    hx = 0.5 * x
    return hx + hx * jnp.tanh(hx)


def _softplus(x):
    return jnp.maximum(x, 0.0) + jnp.log1p(jnp.exp(-jnp.abs(x)))


def _dot(a, b):
    return jnp.dot(a, b, preferred_element_type=F32)


def _dot_nt(a, b):
    return lax.dot_general(a, b, (((1,), (1,)), ((), ())), preferred_element_type=F32)


def _dot_tn(a, b):
    return lax.dot_general(a, b, (((0,), (0,)), ((), ())), preferred_element_type=F32)


def _rmsnorm_mod(x, norm_w, shift, scale):
    ms = jnp.mean(x * x, axis=-1, keepdims=True)
    return x * lax.rsqrt(ms + EPS) * (norm_w * (1.0 + scale)) + shift


def _piece(j):
    return slice(j * MXU_WIDTH, (j + 1) * MXU_WIDTH)


def _split_bf16(v, parts):
    out = []
    r = v
    for _ in range(parts):
        p = r.astype(BF16).astype(F32)
        out.append(p)
        r = r - p
    return out


def _mod_kernel(c_ref, w_ref, b_ref, o_ref):
    sc = _silu(c_ref[...]).astype(BF16)
    o_ref[...] = _dot(sc, w_ref[...].astype(BF16)) + b_ref[...]


def _adaln_mod(c, w_ada, b_ada):
    b, d = c.shape
    n = w_ada.shape[1]
    return pl.pallas_call(
        _mod_kernel,
        grid=(n // MOD_TILE,),
        in_specs=[
            pl.BlockSpec((b, d), lambda j: (0, 0)),
            pl.BlockSpec((d, MOD_TILE), lambda j: (0, j)),
            pl.BlockSpec((1, MOD_TILE), lambda j: (0, j)),
        ],
        out_specs=pl.BlockSpec((b, MOD_TILE), lambda j: (0, j)),
        out_shape=jax.ShapeDtypeStruct((b, n), F32),
        compiler_params=pltpu.CompilerParams(
            dimension_semantics=("arbitrary",), vmem_limit_bytes=VMEM_LIMIT_BYTES),
        name="adaln_mod",
    )(c, w_ada, b_ada.reshape(1, n))


def _mixer_kernel(tiles_per_row,
                  xn_ref, xp_ref, pos_ref, modn_ref, modp_ref, n1w_ref, invf_ref, win_ref, wdt_ref,
                  convw_ref, convb_ref, dtb_ref, alog_ref, dskip_ref, sinks_ref,
                  snw_ref, wout_ref, scan_ref, expand_ref,
                  o_ref,
                  raw_scr, h_scr, y_scr, q_scr, k_scr, v_scr, xbc_scr, dt_scr, state_scr,
                  mix_scr):
    ts = xn_ref.shape[1]
    n_chunks = ts // CHUNK
    s = pl.program_id(0)
    i = lax.rem(s, tiles_per_row)
    cur = lax.rem(s, 2)
    layout, n_slots, _ = _expand_layout()

    def norm_next():
        modn = modn_ref[0]
        return _rmsnorm_mod(xn_ref[0], n1w_ref[...], modn[0:1], modn[1:2]).astype(BF16)

    @pl.when(s == 0)
    def _():
        modp = modp_ref[0]
        h0 = _rmsnorm_mod(xp_ref[0], n1w_ref[...], modp[0:1], modp[1:2]).astype(BF16)
        h_scr[...] = h0
        for j in range(N_AHEAD_PIECES):
            raw_scr[j] = _dot(h0, win_ref[:, _piece(IN_XBC + j)])
        y_scr[...] = jnp.zeros(y_scr.shape, F32)
        mix_scr[...] = jnp.zeros(mix_scr.shape, BF16)

    @pl.when(i == 0)
    def _():
        k_scr[0:CHUNK, :] = jnp.zeros((CHUNK, KV_WIDTH), BF16)
        v_scr[0:CHUNK, :] = jnp.zeros((CHUNK, KV_WIDTH), BF16)
        xbc_scr[0:SUBLANES, :] = jnp.zeros((SUBLANES, CONV_CH), F32)
        state_scr[...] = jnp.zeros(state_scr.shape, F32)

    for j in range(N_AHEAD_PIECES):
        xbc_scr[SUBLANES:SUBLANES + ts, _piece(j)] = raw_scr[j]
    h = h_scr[...]
    pieces = {j: _dot(h, win_ref[:, _piece(j)]) for j in range(IN_Q, IN_XBC)}

    def proj(j):
        return pieces[j]

    half = HEAD_DIM // 2
    ang_d = pos_ref[0, 0].astype(F32) * invf_ref[...]
    lane_d = lax.broadcasted_iota(jnp.int32, ang_d.shape, 1)

    def spread(tbl):
        blocks = []
        for k in range(ROPE_PACK):
            t = pltpu.roll(tbl, LANES - half * k, axis=1) if k else tbl
            t = jnp.where(lane_d < half, t, pltpu.roll(t, half, axis=1))
            t = jnp.where(lane_d < 2 * half, t, pltpu.roll(t, 2 * half, axis=1))
            blocks.append(t)
        return jnp.concatenate(blocks, axis=0)

    cos = spread(jnp.cos(ang_d))
    sin = spread(jnp.sin(ang_d))
    lane = lax.broadcasted_iota(jnp.int32, (ts, LANES), 1)
    first_half = (lane % HEAD_DIM) < half
    sin_signed = jnp.where(first_half, -sin, sin)

    def rope(t, rows):
        lane_c = lax.broadcasted_iota(jnp.int32, t.shape, 1)
        rot = jnp.where((lane_c % HEAD_DIM) < half, pltpu.roll(t, LANES - half, axis=1),
                        pltpu.roll(t, half, axis=1))
        return t * cos[rows] + rot * sin_signed[rows]

    q_scale = 1.0 / math.sqrt(HEAD_DIM)

    def rope_rows(c):
        rows = slice(c * CHUNK, (c + 1) * CHUNK)
        for j in range(ATTN_WIDTH // LANES):
            piece, off = divmod(j * LANES, MXU_WIDTH)
            q_scr[rows, j * LANES:(j + 1) * LANES] = (
                rope(proj(IN_Q + piece)[rows, off:off + LANES], rows) * q_scale).astype(BF16)
        krows = slice(CHUNK + c * CHUNK, CHUNK + (c + 1) * CHUNK)
        k_scr[krows, :] = rope(proj(IN_KV)[rows, 0:KV_WIDTH], rows).astype(BF16)
        v_scr[krows, :] = proj(IN_KV)[rows, KV_WIDTH:2 * KV_WIDTH].astype(BF16)

    dt_scr[...] = _dot_nt(wdt_ref[...], h)

    h_scr[...] = norm_next()

    a_neg = -jnp.exp(alog_ref[...])
    causal = (lax.broadcasted_iota(jnp.int32, (CHUNK, CHUNK), 0)
              >= lax.broadcasted_iota(jnp.int32, (CHUNK, CHUNK), 1))
    x_cols = slice(0, SSM_WIDTH)
    bc_cols = slice(SSM_WIDTH, CONV_CH)
    group_head = lax.broadcasted_iota(jnp.int32, (CHUNK, GROUP_WIDTH), 1) // SSM_HEAD_DIM

    def conv_silu(row0, cols):
        win = xbc_scr[pl.ds(row0, SUBLANES + CHUNK), cols]
        assert CONV_WIDTH == 4
        w0, w1, w2, w3 = (convw_ref[t:t + 1, cols] for t in range(CONV_WIDTH))
        back2 = pltpu.roll(win, 2, axis=0)
        even = w3 * win + w1 * back2
        odd = pltpu.roll(w2 * win + w0 * back2, 1, axis=0)
        conv = (even + odd)[SUBLANES:SUBLANES + CHUNK] + convb_ref[:, cols]
        return _silu(conv)

    n_ql = Q_PER_KV * CHUNK
    key_idx = lax.broadcasted_iota(jnp.int32, (CHUNK, n_ql), 0)
    qry_idx = lax.broadcasted_iota(jnp.int32, (CHUNK, n_ql), 1) % CHUNK
    upper = key_idx > qry_idx
    lane_head = lax.broadcasted_iota(jnp.int32, (1, n_ql), 1) // CHUNK

    scalars = {}

    def ssd_scalars(c):
        if not scalars:
            dts, lhs = [], []
            pad = jnp.zeros((SUBLANES, CHUNK), F32)
            for cc in range(n_chunks):
                dt = _softplus(dt_scr[0:SSM_HEADS, cc * CHUNK:(cc + 1) * CHUNK] + dtb_ref[...])
                dts.append(dt)
                lhs += _split_bf16(dt * a_neg, 3) + [pad]
            cs_all = _dot(jnp.concatenate(lhs, axis=0).astype(BF16), scan_ref[...])
            a_css, packed = [], []
            for cc, dt in enumerate(dts):
                cs = cs_all[4 * SUBLANES * cc:4 * SUBLANES * (cc + 1)]
                cs = cs[0:8] + cs[8:16] + cs[16:24]
                a_cs = cs[:, :CHUNK]
                a_tot = cs[:, CHUNK:]
                ea = jnp.exp(a_cs)
                w_end = dt * jnp.exp(a_tot - a_cs)
                rows = (_split_bf16(a_cs, 3) + _split_bf16(ea, 2) + _split_bf16(w_end, 2)
                        + _split_bf16(dt, 2))
                assert len(rows) == n_slots
                rows.append(jnp.zeros((LANES - n_slots * HEAD_SLOT, CHUNK), F32))
                packed.append(jnp.concatenate(rows, axis=0).T.astype(BF16))
                a_css.append(a_cs)
            scalars["a_cs"] = a_css
            scalars["ex"] = _dot(jnp.concatenate(packed, axis=0), expand_ref[...])
        return scalars["a_cs"][c], scalars["ex"][c * CHUNK:(c + 1) * CHUNK]

    def chunk_streams(c, tick):
        r0 = c * CHUNK
        prev_bias = jnp.where((i == 0) & (c == 0), MASK_VALUE, 0.0).astype(F32)

        def attn_stages(g):
            if g == 0:
                rope_rows(c)
            kband = k_scr[pl.ds(r0, 2 * CHUNK), :]
            vband = v_scr[pl.ds(r0, 2 * CHUNK), :]
            qblk = q_scr[pl.ds(r0, CHUNK), :]
            qg = jnp.concatenate(
                [qblk[:, (g * Q_PER_KV + j) * HEAD_DIM:(g * Q_PER_KV + j + 1) * HEAD_DIM]
                 for j in range(Q_PER_KV)], axis=0)
            s_t = _dot_nt(kband[:, g * HEAD_DIM:(g + 1) * HEAD_DIM], qg)
            yield
            sf = jnp.where(upper, s_t[:CHUNK] + prev_bias, s_t[CHUNK:])
            sink = jnp.zeros((1, n_ql), F32)
            for j in range(Q_PER_KV):
                sink = jnp.where(lane_head == j, sinks_ref[g * Q_PER_KV + j], sink)
            m = jnp.maximum(jnp.max(sf, axis=0, keepdims=True), sink)
            pf = jnp.exp(sf - m)
            denom = jnp.sum(pf, axis=0, keepdims=True) + jnp.exp(sink - m)
            p_t = jnp.concatenate([jnp.where(upper, pf, 0.0), jnp.where(upper, 0.0, pf)],
                                  axis=0).astype(BF16)
            yield
            tick()
            o_t = _dot_tn(vband[:, g * HEAD_DIM:(g + 1) * HEAD_DIM], p_t)
            o_t = o_t * (1.0 / denom)
            yield
            tiles = []
            for pp in range(Q_PER_KV // 2):
                blk = jnp.concatenate([o_t[:, (2 * pp) * CHUNK:(2 * pp + 1) * CHUNK],
                                       o_t[:, (2 * pp + 1) * CHUNK:(2 * pp + 2) * CHUNK]], axis=0)
                tiles.append(blk.T)
            mix_scr[cur, pl.ds(r0, CHUNK),
                    g * Q_PER_KV * HEAD_DIM:(g + 1) * Q_PER_KV * HEAD_DIM] = (
                jnp.concatenate(tiles, axis=1).astype(BF16))

        def ssd_stages():
            xs = conv_silu(r0, x_cols)
            bc = conv_silu(r0, bc_cols)
            yield
            a_cs, ex = ssd_scalars(c)

            def ex_block(name):
                _, _, col0, width = layout[name]
                return ex[:, col0:col0 + width]

            acs_b = ex_block("acs")
            ea_e = ex_block("ea")
            xdt = (xs * ex_block("dt")).astype(BF16)
            xw = (xs * ex_block("w")).astype(BF16)
            yield
            ys = []
            for g in range(SSM_GROUPS):
                gs = slice(g * GROUP_WIDTH, (g + 1) * GROUP_WIDTH)
                bg_t = bc[:, g * D_STATE:(g + 1) * D_STATE].T.astype(BF16)
                cg = bc[:, (SSM_GROUPS + g) * D_STATE:(SSM_GROUPS + g + 1) * D_STATE].astype(BF16)
                cb = _dot(cg, bg_t)
                ms = []
                xm = []
                for j in range(HEADS_PER_GROUP):
                    hh = g * HEADS_PER_GROUP + j
                    seg = acs_b[:, hh * LANES:(hh + 1) * LANES] - a_cs[hh:hh + 1, :]
                    dec = jnp.exp(jnp.where(causal, seg, MASK_VALUE))
                    ms.append((cb * dec).astype(BF16))
                    xm.append(jnp.where(group_head == j, xdt[:, gs], jnp.zeros((), BF16)))
                tick()
                y_diag = _dot(jnp.concatenate(ms, axis=1), jnp.concatenate(xm, axis=0))
                yield
                s_old = state_scr[g]
                y_off = _dot(cg, s_old.astype(BF16))
                state_scr[g] = s_old * ea_e[CHUNK - 1:CHUNK, gs] + _dot(bg_t, xw[:, gs])
                yield
                y_g = y_diag + y_off * ea_e[:, gs] + xs[:, gs] * dskip_ref[:, gs]
                y_g = y_g * _silu(proj(IN_Z + g)[r0:r0 + CHUNK, :])
                y_g = y_g * lax.rsqrt(jnp.mean(y_g * y_g, axis=-1, keepdims=True) + EPS)
                ys.append(y_g * snw_ref[:, gs])
                yield
            mix_scr[cur, pl.ds(r0, CHUNK), ATTN_WIDTH:] = jnp.concatenate(ys, axis=1).astype(BF16)

        return {"a0": attn_stages(0), "a1": attn_stages(1), "s": ssd_stages()}

    def filler_units():
        for j in range(max(N_AHEAD_PIECES, N_OUT_PIECES)):
            if j < N_AHEAD_PIECES:
                raw_scr[j] = _dot(h_scr[...], win_ref[:, _piece(IN_XBC + j)])
                yield
            if j < N_OUT_PIECES:
                y_scr[j] = _dot(mix_scr[1 - cur], wout_ref[:, _piece(j)])
                yield

    fill = filler_units()
    def tick(units=1):
        for _ in range(units):
            next(fill, None)
    streams = {}
    for c in range(n_chunks):
        for name, gen in chunk_streams(c, tick).items():
            streams[c, name] = gen
    for key in _stage_order(n_chunks):
        next(streams[key], None)
    for stream in streams.values():
        assert next(stream, "done") == "done"
    for _ in fill:
        pass

    k_scr[0:CHUNK, :] = k_scr[ts:ts + CHUNK, :]
    v_scr[0:CHUNK, :] = v_scr[ts:ts + CHUNK, :]
    xbc_scr[0:SUBLANES, :] = xbc_scr[ts:ts + SUBLANES, :]

    gate1 = modp_ref[0][2:3]
    for j in range(N_OUT_PIECES):
        cols = slice(j * MXU_WIDTH, (j + 1) * MXU_WIDTH)
        o_ref[0, :, cols] = xp_ref[0, :, cols] + gate1[:, cols] * y_scr[j]


def _const_spec(shape):
    nd = len(shape)
    return pl.BlockSpec(shape, lambda *_: (0,) * nd, pipeline_mode=pl.Buffered(1))


def _mixer(x, pos4, mod3, n1w, invf, win, wdt, convw, convb, dtb, alog, dskip, sinks, snw, wout, scan,
           expand):
    b, s, d = x.shape
    ts = MIXER_TILE
    assert ts % CHUNK == 0 and s % ts == 0
    tpr = s // ts
    n_tiles = b * tpr

    def nxt(step):
        t = jnp.minimum(step + 1, n_tiles - 1)
        return t // tpr, t % tpr

    def prv(step):
        t = jnp.maximum(step - 1, 0)
        return t // tpr, t % tpr

    def cur(step):
        t = jnp.minimum(step, n_tiles - 1)
        return t // tpr, t % tpr

    in_specs = [
        pl.BlockSpec((1, ts, d), lambda st: (*nxt(st), 0)),
        pl.BlockSpec((1, ts, d), lambda st: (*prv(st), 0)),
        pl.BlockSpec((1, 1, ts // ROPE_PACK, LANES), lambda st: (*cur(st), 0, 0)),
        pl.BlockSpec((1, N_MOD, d), lambda st: (nxt(st)[0], 0, 0)),
        pl.BlockSpec((1, N_MOD, d), lambda st: (prv(st)[0], 0, 0)),
        _const_spec(n1w.shape), _const_spec(invf.shape), _const_spec(win.shape),
        _const_spec(wdt.shape),
        _const_spec(convw.shape), _const_spec(convb.shape), _const_spec(dtb.shape),
        _const_spec(alog.shape), _const_spec(dskip.shape),
        pl.BlockSpec(memory_space=pltpu.SMEM),
        _const_spec(snw.shape), _const_spec(wout.shape), _const_spec(scan.shape),
        _const_spec(expand.shape),
    ]
    scratch = [
        pltpu.VMEM((N_AHEAD_PIECES, ts, MXU_WIDTH), F32),
        pltpu.VMEM((ts, d), BF16),
        pltpu.VMEM((N_OUT_PIECES, ts, MXU_WIDTH), F32),
        pltpu.VMEM((ts, ATTN_WIDTH), BF16),
        pltpu.VMEM((CHUNK + ts, KV_WIDTH), BF16),
        pltpu.VMEM((CHUNK + ts, KV_WIDTH), BF16),
        pltpu.VMEM((SUBLANES + ts, CONV_CH), F32),
        pltpu.VMEM((2 * SUBLANES, ts), F32),
        pltpu.VMEM((SSM_GROUPS, D_STATE, GROUP_WIDTH), F32),
        pltpu.VMEM((2, ts, MIX_WIDTH), BF16),
    ]
    return pl.pallas_call(
        lambda *refs: _mixer_kernel(tpr, *refs),
        grid=(n_tiles + 1,),
        in_specs=in_specs,
        out_specs=pl.BlockSpec((1, ts, d), lambda st: (*prv(st), 0)),
        out_shape=jax.ShapeDtypeStruct((b, s, d), F32),
        scratch_shapes=scratch,
        compiler_params=pltpu.CompilerParams(
            dimension_semantics=("arbitrary",), vmem_limit_bytes=VMEM_LIMIT_BYTES),
        name="mixer",
    )(x, x, pos4, mod3, mod3, n1w, invf, win, wdt, convw, convb, dtb, alog, dskip, sinks, snw, wout,
      scan, expand)


def _ffn_kernel(x_ref, mod_ref, n2w_ref, wgu_ref, wd_ref, nfw_ref, o_ref):
    mod = mod_ref[0]
    shift2, scale2, gate2 = mod[3:4], mod[4:5], mod[5:6]
    x = x_ref[0]
    h = _rmsnorm_mod(x, n2w_ref[...], shift2, scale2).astype(BF16)
    g = _dot(h, wgu_ref[:, :D_FF])
    u = _dot(h, wgu_ref[:, D_FF:])
    act = (_silu(g) * u).astype(BF16)
    x2 = x + gate2 * _dot(act, wd_ref[...])
    ms = jnp.mean(x2 * x2, axis=-1, keepdims=True)
    o_ref[0] = x2 * lax.rsqrt(ms + EPS) * nfw_ref[...]


def _ffn(x1, mod3, n2w, wgu, wd, nfw):
    b, s, d = x1.shape
    tm = FFN_TILE
    return pl.pallas_call(
        _ffn_kernel,
        grid=(b, s // tm),
        in_specs=[
            pl.BlockSpec((1, tm, d), lambda bb, i: (bb, i, 0)),
            pl.BlockSpec((1, N_MOD, d), lambda bb, i: (bb, 0, 0)),
            _const_spec(n2w.shape), _const_spec(wgu.shape), _const_spec(wd.shape),
            _const_spec(nfw.shape),
        ],
        out_specs=pl.BlockSpec((1, tm, d), lambda bb, i: (bb, i, 0)),
        out_shape=jax.ShapeDtypeStruct((b, s, d), F32),
        compiler_params=pltpu.CompilerParams(
            dimension_semantics=("arbitrary", "arbitrary"), vmem_limit_bytes=VMEM_LIMIT_BYTES),
        name="ffn",
    )(x1, mod3, n2w, wgu, wd, nfw)


def _head_rows(v):
    return jnp.broadcast_to(v.astype(F32)[:, None], (SSM_HEADS, LANES))


def kernel(x, c, positions, w_ada, b_ada, norm1_w, w_in, conv_w, conv_b, dt_bias, a_log, d_skip,
           attn_sinks, ssm_norm_w, w_out, norm2_w, w_gate_up, w_down, final_norm_w):
    b, s, d = x.shape
    depth = w_ada.shape[0]
    half = HEAD_DIM // 2
    inv_freq = ROPE_THETA ** (-jnp.arange(half, dtype=F32) / half)
    invf = jnp.tile(inv_freq, LANES // half).reshape(1, LANES)
    scan = _scan_matrix()
    expand = _expand_matrix()
    ts = MIXER_TILE
    pos4 = positions.reshape(b, s // ts, ROPE_PACK, ts // ROPE_PACK).transpose(0, 1, 3, 2)
    pos4 = jnp.repeat(pos4, half, axis=-1)
    assert depth == 1, "the final norm is fused into the (single) layer's ffn call"
    layer = 0
    mod3 = _adaln_mod(c, w_ada[layer], b_ada[layer]).reshape(b, N_MOD, d)
    n_main = N_IN_PIECES * MXU_WIDTH
    assert w_in.shape[2] == n_main + SSM_HEADS
    w_main = w_in[layer][:, :n_main].astype(BF16)
    wdt = jnp.pad(w_in[layer][:, n_main:].T.astype(BF16),
                  ((0, 2 * SUBLANES - SSM_HEADS), (0, 0)))
    x1 = _mixer(
        x, pos4, mod3, norm1_w[layer].reshape(1, d), invf, w_main, wdt,
        conv_w[layer], conv_b[layer].reshape(1, CONV_CH),
        _head_rows(dt_bias[layer]), _head_rows(a_log[layer]),
        jnp.repeat(d_skip[layer].astype(F32), SSM_HEAD_DIM).reshape(1, SSM_WIDTH),
        attn_sinks[layer].astype(F32),
        ssm_norm_w[layer].reshape(1, SSM_WIDTH), w_out[layer].astype(BF16), scan, expand)
    return _ffn(x1, mod3, norm2_w[layer].reshape(1, d), w_gate_up[layer].astype(BF16),
                w_down[layer].astype(BF16), final_norm_w.reshape(1, d))
```

```python
import math

import jax
import jax.numpy as jnp
import numpy as np
from jax import lax
from jax.experimental import pallas as pl
from jax.experimental.pallas import tpu as pltpu

D_MODEL = 1024
HEAD_DIM = 64
N_Q_HEADS = 8
N_KV_HEADS = 2
Q_PER_KV = N_Q_HEADS // N_KV_HEADS
ATTN_WIDTH = N_Q_HEADS * HEAD_DIM
KV_WIDTH = N_KV_HEADS * HEAD_DIM
WINDOW = 128
ROPE_THETA = 10000.0
SSM_HEADS = 8
SSM_HEAD_DIM = 64
SSM_WIDTH = SSM_HEADS * SSM_HEAD_DIM
SSM_GROUPS = 2
HEADS_PER_GROUP = SSM_HEADS // SSM_GROUPS
GROUP_WIDTH = HEADS_PER_GROUP * SSM_HEAD_DIM
D_STATE = 128
CONV_WIDTH = 4
CHUNK = 128
CONV_CH = SSM_WIDTH + 2 * SSM_GROUPS * D_STATE
MIX_WIDTH = ATTN_WIDTH + SSM_WIDTH
D_FF = 2816
N_MOD = 6
EPS = 1e-6

LANES = 128
SUBLANES = 8
MXU_WIDTH = 256
HEAD_SLOT = 8
ROPE_PACK = LANES // (HEAD_DIM // 2)
MASK_VALUE = -1e30
VMEM_LIMIT_BYTES = 56 * 1024 * 1024

MIXER_TILE = 512
FFN_TILE = 1024
MOD_TILE = 2048

F32 = jnp.float32
BF16 = jnp.bfloat16

IN_Q = 0
IN_KV = IN_Q + ATTN_WIDTH // MXU_WIDTH
IN_Z = IN_KV + 2 * KV_WIDTH // MXU_WIDTH
IN_XBC = IN_Z + SSM_WIDTH // MXU_WIDTH
N_AHEAD_PIECES = CONV_CH // MXU_WIDTH
N_IN_PIECES = IN_XBC + N_AHEAD_PIECES
N_OUT_PIECES = D_MODEL // MXU_WIDTH
assert GROUP_WIDTH == MXU_WIDTH
assert WINDOW == CHUNK

ATTN_STAGES = 4
SSD_STAGES = 9
_SSD_NEEDS_PREV_PAST = {3: 4, 6: 7}


def _stage_order(n_chunks):
    prog = {(k, name): 0 for k in range(n_chunks) for name in ("a0", "a1", "s")}
    order = []
    n_stages = {"a0": ATTN_STAGES, "a1": ATTN_STAGES, "s": SSD_STAGES}
    while any(v < n_stages[name] for (_, name), v in prog.items()):
        for k in range(n_chunks):
            for name in ("a0", "a1"):
                if prog[k, name] < ATTN_STAGES:
                    prog[k, name] += 1
                    order.append((k, name))
        for k in range(n_chunks):
            stage = prog[k, "s"]
            need = _SSD_NEEDS_PREV_PAST.get(stage) if k else None
            if stage < SSD_STAGES and (need is None or prog[k - 1, "s"] >= need):
                prog[k, "s"] += 1
                order.append((k, "s"))
    assert all(v == (SSD_STAGES if name == "s" else ATTN_STAGES) for (_, name), v in prog.items())
    return order

_EXPAND_ITEMS = (("acs", 3, LANES), ("ea", 2, SSM_HEAD_DIM), ("w", 2, SSM_HEAD_DIM),
                 ("dt", 2, SSM_HEAD_DIM))


def _expand_layout():
    slot = 0
    col = 0
    layout = {}
    for name, parts, rep in _EXPAND_ITEMS:
        layout[name] = (slot, parts, col, rep * SSM_HEADS)
        slot += parts
        col += rep * SSM_HEADS
    return layout, slot, col


def _expand_matrix():
    layout, nslots, ncols = _expand_layout()
    assert nslots * HEAD_SLOT <= LANES
    e = np.zeros((LANES, ncols), np.float32)
    for name, parts, rep in _EXPAND_ITEMS:
        slot0, _, col0, _ = layout[name]
        for p in range(parts):
            for hh in range(SSM_HEADS):
                e[(slot0 + p) * HEAD_SLOT + hh, col0 + hh * rep:col0 + (hh + 1) * rep] = 1.0
    return jnp.asarray(e, BF16)


def _scan_matrix():
    t = np.arange(CHUNK)
    upper = (t[:, None] <= t[None, :]).astype(np.float32)
    return jnp.asarray(np.concatenate([upper, np.ones((CHUNK, CHUNK), np.float32)], axis=1), BF16)


def _silu(x):
    hx = 0.5 * x
    return hx + hx * jnp.tanh(hx)


def _softplus(x):
    return jnp.maximum(x, 0.0) + jnp.log1p(jnp.exp(-jnp.abs(x)))


def _dot(a, b):
    return jnp.dot(a, b, preferred_element_type=F32)


def _dot_nt(a, b):
    return lax.dot_general(a, b, (((1,), (1,)), ((), ())), preferred_element_type=F32)


def _dot_tn(a, b):
    return lax.dot_general(a, b, (((0,), (0,)), ((), ())), preferred_element_type=F32)


def _rmsnorm_mod(x, norm_w, shift, scale):
    ms = jnp.mean(x * x, axis=-1, keepdims=True)
    return x * lax.rsqrt(ms + EPS) * (norm_w * (1.0 + scale)) + shift


def _piece(j):
    return slice(j * MXU_WIDTH, (j + 1) * MXU_WIDTH)


def _split_bf16(v, parts):
    out = []
    r = v
    for _ in range(parts):
        p = r.astype(BF16).astype(F32)
        out.append(p)
        r = r - p
    return out


def _mod_kernel(c_ref, w_ref, b_ref, o_ref):
    sc = _silu(c_ref[...]).astype(BF16)
    o_ref[...] = _dot(sc, w_ref[...].astype(BF16)) + b_ref[...]


def _adaln_mod(c, w_ada, b_ada):
    b, d = c.shape
    n = w_ada.shape[1]
    return pl.pallas_call(
        _mod_kernel,
        grid=(n // MOD_TILE,),
        in_specs=[
            pl.BlockSpec((b, d), lambda j: (0, 0)),
            pl.BlockSpec((d, MOD_TILE), lambda j: (0, j)),
            pl.BlockSpec((1, MOD_TILE), lambda j: (0, j)),
        ],
        out_specs=pl.BlockSpec((b, MOD_TILE), lambda j: (0, j)),
        out_shape=jax.ShapeDtypeStruct((b, n), F32),
        compiler_params=pltpu.CompilerParams(
            dimension_semantics=("arbitrary",), vmem_limit_bytes=VMEM_LIMIT_BYTES),
        name="adaln_mod",
    )(c, w_ada, b_ada.reshape(1, n))


def _mixer_kernel(tiles_per_row,
                  xn_ref, xp_ref, pos_ref, modn_ref, modp_ref, n1w_ref, invf_ref, win_ref, wdt_ref,
                  convw_ref, convb_ref, dtb_ref, alog_ref, dskip_ref, sinks_ref,
                  snw_ref, wout_ref, scan_ref, expand_ref,
                  o_ref,
                  raw_scr, h_scr, y_scr, q_scr, k_scr, v_scr, xbc_scr, dt_scr, state_scr,
                  mix_scr):
    ts = xn_ref.shape[1]
    n_chunks = ts // CHUNK
    s = pl.program_id(0)
    i = lax.rem(s, tiles_per_row)
    cur = lax.rem(s, 2)
    layout, n_slots, _ = _expand_layout()

    def norm_next():
        modn = modn_ref[0]
        return _rmsnorm_mod(xn_ref[0], n1w_ref[...], modn[0:1], modn[1:2]).astype(BF16)

    @pl.when(s == 0)
    def _():
        modp = modp_ref[0]
        h0 = _rmsnorm_mod(xp_ref[0], n1w_ref[...], modp[0:1], modp[1:2]).astype(BF16)
        h_scr[...] = h0
        for j in range(N_AHEAD_PIECES):
            raw_scr[j] = _dot(h0, win_ref[:, _piece(IN_XBC + j)])
        y_scr[...] = jnp.zeros(y_scr.shape, F32)
        mix_scr[...] = jnp.zeros(mix_scr.shape, BF16)

    @pl.when(i == 0)
    def _():
        k_scr[0:CHUNK, :] = jnp.zeros((CHUNK, KV_WIDTH), BF16)
        v_scr[0:CHUNK, :] = jnp.zeros((CHUNK, KV_WIDTH), BF16)
        xbc_scr[0:SUBLANES, :] = jnp.zeros((SUBLANES, CONV_CH), F32)
        state_scr[...] = jnp.zeros(state_scr.shape, F32)

    for j in range(N_AHEAD_PIECES):
        xbc_scr[SUBLANES:SUBLANES + ts, _piece(j)] = raw_scr[j]
    h = h_scr[...]
    pieces = {j: _dot(h, win_ref[:, _piece(j)]) for j in range(IN_Q, IN_XBC)}

    def proj(j):
        return pieces[j]

    half = HEAD_DIM // 2
    ang_d = pos_ref[0, 0].astype(F32) * invf_ref[...]
    lane_d = lax.broadcasted_iota(jnp.int32, ang_d.shape, 1)

    def spread(tbl):
        blocks = []
        for k in range(ROPE_PACK):
            t = pltpu.roll(tbl, LANES - half * k, axis=1) if k else tbl
            t = jnp.where(lane_d < half, t, pltpu.roll(t, half, axis=1))
            t = jnp.where(lane_d < 2 * half, t, pltpu.roll(t, 2 * half, axis=1))
            blocks.append(t)
        return jnp.concatenate(blocks, axis=0)

    cos = spread(jnp.cos(ang_d))
    sin = spread(jnp.sin(ang_d))
    lane = lax.broadcasted_iota(jnp.int32, (ts, LANES), 1)
    first_half = (lane % HEAD_DIM) < half
    sin_signed = jnp.where(first_half, -sin, sin)

    def rope(t, rows):
        lane_c = lax.broadcasted_iota(jnp.int32, t.shape, 1)
        rot = jnp.where((lane_c % HEAD_DIM) < half, pltpu.roll(t, LANES - half, axis=1),
                        pltpu.roll(t, half, axis=1))
        return t * cos[rows] + rot * sin_signed[rows]

    q_scale = 1.0 / math.sqrt(HEAD_DIM)

    def rope_rows(c):
        rows = slice(c * CHUNK, (c + 1) * CHUNK)
        for j in range(ATTN_WIDTH // LANES):
            piece, off = divmod(j * LANES, MXU_WIDTH)
            q_scr[rows, j * LANES:(j + 1) * LANES] = (
                rope(proj(IN_Q + piece)[rows, off:off + LANES], rows) * q_scale).astype(BF16)
        krows = slice(CHUNK + c * CHUNK, CHUNK + (c + 1) * CHUNK)
        k_scr[krows, :] = rope(proj(IN_KV)[rows, 0:KV_WIDTH], rows).astype(BF16)
        v_scr[krows, :] = proj(IN_KV)[rows, KV_WIDTH:2 * KV_WIDTH].astype(BF16)

    dt_scr[...] = _dot_nt(wdt_ref[...], h)

    h_scr[...] = norm_next()

    a_neg = -jnp.exp(alog_ref[...])
    causal = (lax.broadcasted_iota(jnp.int32, (CHUNK, CHUNK), 0)
              >= lax.broadcasted_iota(jnp.int32, (CHUNK, CHUNK), 1))
    x_cols = slice(0, SSM_WIDTH)
    bc_cols = slice(SSM_WIDTH, CONV_CH)
    group_head = lax.broadcasted_iota(jnp.int32, (CHUNK, GROUP_WIDTH), 1) // SSM_HEAD_DIM

    def conv_silu(row0, cols):
        win = xbc_scr[pl.ds(row0, SUBLANES + CHUNK), cols]
        assert CONV_WIDTH == 4
        w0, w1, w2, w3 = (convw_ref[t:t + 1, cols] for t in range(CONV_WIDTH))
        back2 = pltpu.roll(win, 2, axis=0)
        even = w3 * win + w1 * back2
        odd = pltpu.roll(w2 * win + w0 * back2, 1, axis=0)
        conv = (even + odd)[SUBLANES:SUBLANES + CHUNK] + convb_ref[:, cols]
        return _silu(conv)

    n_ql = Q_PER_KV * CHUNK
    key_idx = lax.broadcasted_iota(jnp.int32, (CHUNK, n_ql), 0)
    qry_idx = lax.broadcasted_iota(jnp.int32, (CHUNK, n_ql), 1) % CHUNK
    upper = key_idx > qry_idx
    lane_head = lax.broadcasted_iota(jnp.int32, (1, n_ql), 1) // CHUNK

    scalars = {}

    def ssd_scalars(c):
        if not scalars:
            dts, lhs = [], []
            pad = jnp.zeros((SUBLANES, CHUNK), F32)
            for cc in range(n_chunks):
                dt = _softplus(dt_scr[0:SSM_HEADS, cc * CHUNK:(cc + 1) * CHUNK] + dtb_ref[...])
                dts.append(dt)
                lhs += _split_bf16(dt * a_neg, 3) + [pad]
            cs_all = _dot(jnp.concatenate(lhs, axis=0).astype(BF16), scan_ref[...])
            a_css, packed = [], []
            for cc, dt in enumerate(dts):
                cs = cs_all[4 * SUBLANES * cc:4 * SUBLANES * (cc + 1)]
                cs = cs[0:8] + cs[8:16] + cs[16:24]
                a_cs = cs[:, :CHUNK]
                a_tot = cs[:, CHUNK:]
                ea = jnp.exp(a_cs)
                w_end = dt * jnp.exp(a_tot - a_cs)
                rows = (_split_bf16(a_cs, 3) + _split_bf16(ea, 2) + _split_bf16(w_end, 2)
                        + _split_bf16(dt, 2))
                assert len(rows) == n_slots
                rows.append(jnp.zeros((LANES - n_slots * HEAD_SLOT, CHUNK), F32))
                packed.append(jnp.concatenate(rows, axis=0).T.astype(BF16))
                a_css.append(a_cs)
            scalars["a_cs"] = a_css
            scalars["ex"] = _dot(jnp.concatenate(packed, axis=0), expand_ref[...])
        return scalars["a_cs"][c], scalars["ex"][c * CHUNK:(c + 1) * CHUNK]

    def chunk_streams(c, tick):
        r0 = c * CHUNK
        prev_bias = jnp.where((i == 0) & (c == 0), MASK_VALUE, 0.0).astype(F32)

        def attn_stages(g):
            if g == 0:
                rope_rows(c)
            kband = k_scr[pl.ds(r0, 2 * CHUNK), :]
            vband = v_scr[pl.ds(r0, 2 * CHUNK), :]
            qblk = q_scr[pl.ds(r0, CHUNK), :]
            qg = jnp.concatenate(
                [qblk[:, (g * Q_PER_KV + j) * HEAD_DIM:(g * Q_PER_KV + j + 1) * HEAD_DIM]
                 for j in range(Q_PER_KV)], axis=0)
            s_t = _dot_nt(kband[:, g * HEAD_DIM:(g + 1) * HEAD_DIM], qg)
            yield
            sf = jnp.where(upper, s_t[:CHUNK] + prev_bias, s_t[CHUNK:])
            sink = jnp.zeros((1, n_ql), F32)
            for j in range(Q_PER_KV):
                sink = jnp.where(lane_head == j, sinks_ref[g * Q_PER_KV + j], sink)
            m = jnp.maximum(jnp.max(sf, axis=0, keepdims=True), sink)
            pf = jnp.exp(sf - m)
            denom = jnp.sum(pf, axis=0, keepdims=True) + jnp.exp(sink - m)
            pfb = pf.astype(BF16)
            zero = jnp.zeros((), BF16)
            p_t = jnp.concatenate([jnp.where(upper, pfb, zero), jnp.where(upper, zero, pfb)],
                                  axis=0)
            yield
            tick()
            o_t = _dot_tn(vband[:, g * HEAD_DIM:(g + 1) * HEAD_DIM], p_t)
            o_t = o_t * (1.0 / denom)
            yield
            tiles = []
            for pp in range(Q_PER_KV // 2):
                blk = jnp.concatenate([o_t[:, (2 * pp) * CHUNK:(2 * pp + 1) * CHUNK],
                                       o_t[:, (2 * pp + 1) * CHUNK:(2 * pp + 2) * CHUNK]], axis=0)
                tiles.append(blk.T)
            mix_scr[cur, pl.ds(r0, CHUNK),
                    g * Q_PER_KV * HEAD_DIM:(g + 1) * Q_PER_KV * HEAD_DIM] = (
                jnp.concatenate(tiles, axis=1).astype(BF16))

        def ssd_stages():
            xs = conv_silu(r0, x_cols)
            bc = conv_silu(r0, bc_cols)
            yield
            a_cs, ex = ssd_scalars(c)

            def ex_block(name):
                _, _, col0, width = layout[name]
                return ex[:, col0:col0 + width]

            acs_b = ex_block("acs")
            ea_e = ex_block("ea")
            xdt = (xs * ex_block("dt")).astype(BF16)
            xw = (xs * ex_block("w")).astype(BF16)
            yield
            ys = []
            for g in range(SSM_GROUPS):
                gs = slice(g * GROUP_WIDTH, (g + 1) * GROUP_WIDTH)
                bg_t = bc[:, g * D_STATE:(g + 1) * D_STATE].T.astype(BF16)
                cg = bc[:, (SSM_GROUPS + g) * D_STATE:(SSM_GROUPS + g + 1) * D_STATE].astype(BF16)
                cb = _dot(cg, bg_t)
                ms = []
                xm = []
                for j in range(HEADS_PER_GROUP):
                    hh = g * HEADS_PER_GROUP + j
                    seg = acs_b[:, hh * LANES:(hh + 1) * LANES] - a_cs[hh:hh + 1, :]
                    dec = jnp.exp(jnp.where(causal, seg, MASK_VALUE))
                    ms.append((cb * dec).astype(BF16))
                    xm.append(jnp.where(group_head == j, xdt[:, gs], jnp.zeros((), BF16)))
                tick()
                y_diag = _dot(jnp.concatenate(ms, axis=1), jnp.concatenate(xm, axis=0))
                yield
                s_old = state_scr[g]
                y_off = _dot(cg, s_old.astype(BF16))
                state_scr[g] = s_old * ea_e[CHUNK - 1:CHUNK, gs] + _dot(bg_t, xw[:, gs])
                yield
                y_g = y_diag + y_off * ea_e[:, gs] + xs[:, gs] * dskip_ref[:, gs]
                y_g = y_g * _silu(proj(IN_Z + g)[r0:r0 + CHUNK, :])
                y_g = y_g * lax.rsqrt(jnp.mean(y_g * y_g, axis=-1, keepdims=True) + EPS)
                ys.append(y_g * snw_ref[:, gs])
                yield
            mix_scr[cur, pl.ds(r0, CHUNK), ATTN_WIDTH:] = jnp.concatenate(ys, axis=1).astype(BF16)

        return {"a0": attn_stages(0), "a1": attn_stages(1), "s": ssd_stages()}

    def filler_units():
        for j in range(max(N_AHEAD_PIECES, N_OUT_PIECES)):
            if j < N_AHEAD_PIECES:
                raw_scr[j] = _dot(h_scr[...], win_ref[:, _piece(IN_XBC + j)])
                yield
            if j < N_OUT_PIECES:
                y_scr[j] = _dot(mix_scr[1 - cur], wout_ref[:, _piece(j)])
                yield

    fill = filler_units()
    def tick(units=1):
        for _ in range(units):
            next(fill, None)
    streams = {}
    for c in range(n_chunks):
        for name, gen in chunk_streams(c, tick).items():
            streams[c, name] = gen
    for key in _stage_order(n_chunks):
        next(streams[key], None)
    for stream in streams.values():
        assert next(stream, "done") == "done"
    for _ in fill:
        pass

    k_scr[0:CHUNK, :] = k_scr[ts:ts + CHUNK, :]
    v_scr[0:CHUNK, :] = v_scr[ts:ts + CHUNK, :]
    xbc_scr[0:SUBLANES, :] = xbc_scr[ts:ts + SUBLANES, :]

    gate1 = modp_ref[0][2:3]
    for j in range(N_OUT_PIECES):
        cols = slice(j * MXU_WIDTH, (j + 1) * MXU_WIDTH)
        o_ref[0, :, cols] = xp_ref[0, :, cols] + gate1[:, cols] * y_scr[j]


def _const_spec(shape):
    nd = len(shape)
    return pl.BlockSpec(shape, lambda *_: (0,) * nd, pipeline_mode=pl.Buffered(1))


def _mixer(x, pos4, mod3, n1w, invf, win, wdt, convw, convb, dtb, alog, dskip, sinks, snw, wout, scan,
           expand):
    b, s, d = x.shape
    ts = MIXER_TILE
    assert ts % CHUNK == 0 and s % ts == 0
    tpr = s // ts
    n_tiles = b * tpr

    def nxt(step):
        t = jnp.minimum(step + 1, n_tiles - 1)
        return t // tpr, t % tpr

    def prv(step):
        t = jnp.maximum(step - 1, 0)
        return t // tpr, t % tpr

    def cur(step):
        t = jnp.minimum(step, n_tiles - 1)
        return t // tpr, t % tpr

    in_specs = [
        pl.BlockSpec((1, ts, d), lambda st: (*nxt(st), 0)),
        pl.BlockSpec((1, ts, d), lambda st: (*prv(st), 0)),
        pl.BlockSpec((1, 1, ts // ROPE_PACK, LANES), lambda st: (*cur(st), 0, 0)),
        pl.BlockSpec((1, N_MOD, d), lambda st: (nxt(st)[0], 0, 0)),
        pl.BlockSpec((1, N_MOD, d), lambda st: (prv(st)[0], 0, 0)),
        _const_spec(n1w.shape), _const_spec(invf.shape), _const_spec(win.shape),
        _const_spec(wdt.shape),
        _const_spec(convw.shape), _const_spec(convb.shape), _const_spec(dtb.shape),
        _const_spec(alog.shape), _const_spec(dskip.shape),
        pl.BlockSpec(memory_space=pltpu.SMEM),
        _const_spec(snw.shape), _const_spec(wout.shape), _const_spec(scan.shape),
        _const_spec(expand.shape),
    ]
    scratch = [
        pltpu.VMEM((N_AHEAD_PIECES, ts, MXU_WIDTH), F32),
        pltpu.VMEM((ts, d), BF16),
        pltpu.VMEM((N_OUT_PIECES, ts, MXU_WIDTH), F32),
        pltpu.VMEM((ts, ATTN_WIDTH), BF16),
        pltpu.VMEM((CHUNK + ts, KV_WIDTH), BF16),
        pltpu.VMEM((CHUNK + ts, KV_WIDTH), BF16),
        pltpu.VMEM((SUBLANES + ts, CONV_CH), F32),
        pltpu.VMEM((2 * SUBLANES, ts), F32),
        pltpu.VMEM((SSM_GROUPS, D_STATE, GROUP_WIDTH), F32),
        pltpu.VMEM((2, ts, MIX_WIDTH), BF16),
    ]
    return pl.pallas_call(
        lambda *refs: _mixer_kernel(tpr, *refs),
        grid=(n_tiles + 1,),
        in_specs=in_specs,
        out_specs=pl.BlockSpec((1, ts, d), lambda st: (*prv(st), 0)),
        out_shape=jax.ShapeDtypeStruct((b, s, d), F32),
        scratch_shapes=scratch,
        compiler_params=pltpu.CompilerParams(
            dimension_semantics=("arbitrary",), vmem_limit_bytes=VMEM_LIMIT_BYTES),
        name="mixer",
    )(x, x, pos4, mod3, mod3, n1w, invf, win, wdt, convw, convb, dtb, alog, dskip, sinks, snw, wout,
      scan, expand)


def _ffn_kernel(x_ref, mod_ref, n2w_ref, wgu_ref, wd_ref, nfw_ref, o_ref):
    mod = mod_ref[0]
    shift2, scale2, gate2 = mod[3:4], mod[4:5], mod[5:6]
    x = x_ref[0]
    h = _rmsnorm_mod(x, n2w_ref[...], shift2, scale2).astype(BF16)
    g = _dot(h, wgu_ref[:, :D_FF])
    u = _dot(h, wgu_ref[:, D_FF:])
    act = (_silu(g) * u).astype(BF16)
    x2 = x + gate2 * _dot(act, wd_ref[...])
    ms = jnp.mean(x2 * x2, axis=-1, keepdims=True)
    o_ref[0] = x2 * lax.rsqrt(ms + EPS) * nfw_ref[...]


def _ffn(x1, mod3, n2w, wgu, wd, nfw):
    b, s, d = x1.shape
    tm = FFN_TILE
    return pl.pallas_call(
        _ffn_kernel,
        grid=(b, s // tm),
        in_specs=[
            pl.BlockSpec((1, tm, d), lambda bb, i: (bb, i, 0)),
            pl.BlockSpec((1, N_MOD, d), lambda bb, i: (bb, 0, 0)),
            _const_spec(n2w.shape), _const_spec(wgu.shape), _const_spec(wd.shape),
            _const_spec(nfw.shape),
        ],
        out_specs=pl.BlockSpec((1, tm, d), lambda bb, i: (bb, i, 0)),
        out_shape=jax.ShapeDtypeStruct((b, s, d), F32),
        compiler_params=pltpu.CompilerParams(
            dimension_semantics=("arbitrary", "arbitrary"), vmem_limit_bytes=VMEM_LIMIT_BYTES),
        name="ffn",
    )(x1, mod3, n2w, wgu, wd, nfw)


def _head_rows(v):
    return jnp.broadcast_to(v.astype(F32)[:, None], (SSM_HEADS, LANES))


def kernel(x, c, positions, w_ada, b_ada, norm1_w, w_in, conv_w, conv_b, dt_bias, a_log, d_skip,
           attn_sinks, ssm_norm_w, w_out, norm2_w, w_gate_up, w_down, final_norm_w):
    b, s, d = x.shape
    depth = w_ada.shape[0]
    half = HEAD_DIM // 2
    inv_freq = ROPE_THETA ** (-jnp.arange(half, dtype=F32) / half)
    invf = jnp.tile(inv_freq, LANES // half).reshape(1, LANES)
    scan = _scan_matrix()
    expand = _expand_matrix()
    ts = MIXER_TILE
    pos4 = positions.reshape(b, s // ts, ROPE_PACK, ts // ROPE_PACK).transpose(0, 1, 3, 2)
    pos4 = jnp.repeat(pos4, half, axis=-1)
    assert depth == 1, "the final norm is fused into the (single) layer's ffn call"
    layer = 0
    mod3 = _adaln_mod(c, w_ada[layer], b_ada[layer]).reshape(b, N_MOD, d)
    n_main = N_IN_PIECES * MXU_WIDTH
    assert w_in.shape[2] == n_main + SSM_HEADS
    w_main = w_in[layer][:, :n_main].astype(BF16)
    wdt = jnp.pad(w_in[layer][:, n_main:].T.astype(BF16),
                  ((0, 2 * SUBLANES - SSM_HEADS), (0, 0)))
    x1 = _mixer(
        x, pos4, mod3, norm1_w[layer].reshape(1, d), invf, w_main, wdt,
        conv_w[layer], conv_b[layer].reshape(1, CONV_CH),
        _head_rows(dt_bias[layer]), _head_rows(a_log[layer]),
        jnp.repeat(d_skip[layer].astype(F32), SSM_HEAD_DIM).reshape(1, SSM_WIDTH),
        attn_sinks[layer].astype(F32),
        ssm_norm_w[layer].reshape(1, SSM_WIDTH), w_out[layer].astype(BF16), scan, expand)
    return _ffn(x1, mod3, norm2_w[layer].reshape(1, d), w_gate_up[layer].astype(BF16),
                w_down[layer].astype(BF16), final_norm_w.reshape(1, d))
```

```python
import math

import jax
import jax.numpy as jnp
import numpy as np
from jax import lax
from jax.experimental import pallas as pl
from jax.experimental.pallas import tpu as pltpu

D_MODEL = 1024
HEAD_DIM = 64
N_Q_HEADS = 8
N_KV_HEADS = 2
Q_PER_KV = N_Q_HEADS // N_KV_HEADS
ATTN_WIDTH = N_Q_HEADS * HEAD_DIM
KV_WIDTH = N_KV_HEADS * HEAD_DIM
WINDOW = 128
ROPE_THETA = 10000.0
SSM_HEADS = 8
SSM_HEAD_DIM = 64
SSM_WIDTH = SSM_HEADS * SSM_HEAD_DIM
SSM_GROUPS = 2
HEADS_PER_GROUP = SSM_HEADS // SSM_GROUPS
GROUP_WIDTH = HEADS_PER_GROUP * SSM_HEAD_DIM
D_STATE = 128
CONV_WIDTH = 4
CHUNK = 128
CONV_CH = SSM_WIDTH + 2 * SSM_GROUPS * D_STATE
MIX_WIDTH = ATTN_WIDTH + SSM_WIDTH
D_FF = 2816
N_MOD = 6
EPS = 1e-6

LANES = 128
SUBLANES = 8
MXU_WIDTH = 256
HEAD_SLOT = 8
ROPE_PACK = LANES // (HEAD_DIM // 2)
MASK_VALUE = -1e30
VMEM_LIMIT_BYTES = 56 * 1024 * 1024

MIXER_TILE = 512
FFN_TILE = 1024
MOD_TILE = 2048

F32 = jnp.float32
BF16 = jnp.bfloat16

IN_Q = 0
IN_KV = IN_Q + ATTN_WIDTH // MXU_WIDTH
IN_Z = IN_KV + 2 * KV_WIDTH // MXU_WIDTH
IN_XBC = IN_Z + SSM_WIDTH // MXU_WIDTH
N_AHEAD_PIECES = CONV_CH // MXU_WIDTH
N_IN_PIECES = IN_XBC + N_AHEAD_PIECES
N_OUT_PIECES = D_MODEL // MXU_WIDTH
assert GROUP_WIDTH == MXU_WIDTH
assert WINDOW == CHUNK

ATTN_STAGES = 4
SSD_STAGES = 9
_SSD_NEEDS_PREV_PAST = {3: 4, 6: 7}


def _stage_order(n_chunks):
    prog = {(k, name): 0 for k in range(n_chunks) for name in ("a0", "a1", "s")}
    order = []
    n_stages = {"a0": ATTN_STAGES, "a1": ATTN_STAGES, "s": SSD_STAGES}
    while any(v < n_stages[name] for (_, name), v in prog.items()):
        for k in range(n_chunks):
            for name in ("a0", "a1"):
                if prog[k, name] < ATTN_STAGES:
                    prog[k, name] += 1
                    order.append((k, name))
        for k in range(n_chunks):
            stage = prog[k, "s"]
            need = _SSD_NEEDS_PREV_PAST.get(stage) if k else None
            if stage < SSD_STAGES and (need is None or prog[k - 1, "s"] >= need):
                prog[k, "s"] += 1
                order.append((k, "s"))
    assert all(v == (SSD_STAGES if name == "s" else ATTN_STAGES) for (_, name), v in prog.items())
    return order

_EXPAND_ITEMS = (("acs", 3, LANES), ("ea", 2, SSM_HEAD_DIM), ("w", 2, SSM_HEAD_DIM),
                 ("dt", 2, SSM_HEAD_DIM))


def _expand_layout():
    slot = 0
    col = 0
    layout = {}
    for name, parts, rep in _EXPAND_ITEMS:
        layout[name] = (slot, parts, col, rep * SSM_HEADS)
        slot += parts
        col += rep * SSM_HEADS
    return layout, slot, col


def _expand_matrix():
    layout, nslots, ncols = _expand_layout()
    assert nslots * HEAD_SLOT <= LANES
    e = np.zeros((LANES, ncols), np.float32)
    for name, parts, rep in _EXPAND_ITEMS:
        slot0, _, col0, _ = layout[name]
        for p in range(parts):
            for hh in range(SSM_HEADS):
                e[(slot0 + p) * HEAD_SLOT + hh, col0 + hh * rep:col0 + (hh + 1) * rep] = 1.0
    return jnp.asarray(e, BF16)


def _scan_matrix():
    t = np.arange(CHUNK)
    upper = (t[:, None] <= t[None, :]).astype(np.float32)
    return jnp.asarray(np.concatenate([upper, np.ones((CHUNK, CHUNK), np.float32)], axis=1), BF16)


def _silu(x):
    hx = 0.5 * x
    return hx + hx * jnp.tanh(hx)


def _softplus(x):
    return jnp.maximum(x, 0.0) + jnp.log1p(jnp.exp(-jnp.abs(x)))


def _dot(a, b):
    return jnp.dot(a, b, preferred_element_type=F32)


def _dot_nt(a, b):
    return lax.dot_general(a, b, (((1,), (1,)), ((), ())), preferred_element_type=F32)


def _dot_tn(a, b):
    return lax.dot_general(a, b, (((0,), (0,)), ((), ())), preferred_element_type=F32)


def _rmsnorm_mod(x, norm_w, shift, scale):
    ms = jnp.mean(x * x, axis=-1, keepdims=True)
    return x * lax.rsqrt(ms + EPS) * (norm_w * (1.0 + scale)) + shift


def _piece(j):
    return slice(j * MXU_WIDTH, (j + 1) * MXU_WIDTH)


def _split_bf16(v, parts):
    out = []
    r = v
    for _ in range(parts):
        p = r.astype(BF16).astype(F32)
        out.append(p)
        r = r - p
    return out


def _mod_kernel(c_ref, w_ref, b_ref, o_ref):
    sc = _silu(c_ref[...]).astype(BF16)
    o_ref[...] = _dot(sc, w_ref[...].astype(BF16)) + b_ref[...]


def _adaln_mod(c, w_ada, b_ada):
    b, d = c.shape
    n = w_ada.shape[1]
    return pl.pallas_call(
        _mod_kernel,
        grid=(n // MOD_TILE,),
        in_specs=[
            pl.BlockSpec((b, d), lambda j: (0, 0)),
            pl.BlockSpec((d, MOD_TILE), lambda j: (0, j)),
            pl.BlockSpec((1, MOD_TILE), lambda j: (0, j)),
        ],
        out_specs=pl.BlockSpec((b, MOD_TILE), lambda j: (0, j)),
        out_shape=jax.ShapeDtypeStruct((b, n), F32),
        compiler_params=pltpu.CompilerParams(
            dimension_semantics=("arbitrary",), vmem_limit_bytes=VMEM_LIMIT_BYTES),
        name="adaln_mod",
    )(c, w_ada, b_ada.reshape(1, n))


def _mixer_kernel(tiles_per_row,
                  xn_ref, xp_ref, pos_ref, modn_ref, modp_ref, n1w_ref, invf_ref, win_ref, wdt_ref,
                  convw_ref, convb_ref, dtb_ref, alog_ref, dskip_ref, sinks_ref,
                  snw_ref, wout_ref, scan_ref, expand_ref,
                  o_ref,
                  raw_scr, h_scr, y_scr, q_scr, k_scr, v_scr, xbc_scr, dt_scr, state_scr,
                  mix_scr):
    ts = xn_ref.shape[1]
    n_chunks = ts // CHUNK
    s = pl.program_id(0)
    i = lax.rem(s, tiles_per_row)
    cur = lax.rem(s, 2)
    layout, n_slots, _ = _expand_layout()

    def norm_next():
        modn = modn_ref[0]
        return _rmsnorm_mod(xn_ref[0], n1w_ref[...], modn[0:1], modn[1:2]).astype(BF16)

    @pl.when(s == 0)
    def _():
        modp = modp_ref[0]
        h0 = _rmsnorm_mod(xp_ref[0], n1w_ref[...], modp[0:1], modp[1:2]).astype(BF16)
        h_scr[...] = h0
        for j in range(N_AHEAD_PIECES):
            raw_scr[j] = _dot(h0, win_ref[:, _piece(IN_XBC + j)])
        y_scr[...] = jnp.zeros(y_scr.shape, F32)
        mix_scr[...] = jnp.zeros(mix_scr.shape, BF16)

    @pl.when(i == 0)
    def _():
        k_scr[0:CHUNK, :] = jnp.zeros((CHUNK, KV_WIDTH), BF16)
        v_scr[0:CHUNK, :] = jnp.zeros((CHUNK, KV_WIDTH), BF16)
        xbc_scr[0:SUBLANES, :] = jnp.zeros((SUBLANES, CONV_CH), F32)
        state_scr[...] = jnp.zeros(state_scr.shape, F32)

    for j in range(N_AHEAD_PIECES):
        xbc_scr[SUBLANES:SUBLANES + ts, _piece(j)] = raw_scr[j]
    h = h_scr[...]
    pieces = {j: _dot(h, win_ref[:, _piece(j)]) for j in range(IN_Q, IN_XBC)}

    def proj(j):
        return pieces[j]

    half = HEAD_DIM // 2
    ang_d = pos_ref[0, 0].astype(F32) * invf_ref[...]
    lane_d = lax.broadcasted_iota(jnp.int32, ang_d.shape, 1)

    def spread(tbl):
        blocks = []
        for k in range(ROPE_PACK):
            t = pltpu.roll(tbl, LANES - half * k, axis=1) if k else tbl
            t = jnp.where(lane_d < half, t, pltpu.roll(t, half, axis=1))
            t = jnp.where(lane_d < 2 * half, t, pltpu.roll(t, 2 * half, axis=1))
            blocks.append(t)
        return jnp.concatenate(blocks, axis=0)

    cos = spread(jnp.cos(ang_d))
    sin = spread(jnp.sin(ang_d))
    lane = lax.broadcasted_iota(jnp.int32, (ts, LANES), 1)
    first_half = (lane % HEAD_DIM) < half
    sin_signed = jnp.where(first_half, -sin, sin)

    def rope(t, rows):
        lane_c = lax.broadcasted_iota(jnp.int32, t.shape, 1)
        rot = jnp.where((lane_c % HEAD_DIM) < half, pltpu.roll(t, LANES - half, axis=1),
                        pltpu.roll(t, half, axis=1))
        return t * cos[rows] + rot * sin_signed[rows]

    q_scale = 1.0 / math.sqrt(HEAD_DIM)

    def rope_rows(c):
        rows = slice(c * CHUNK, (c + 1) * CHUNK)
        for j in range(ATTN_WIDTH // LANES):
            piece, off = divmod(j * LANES, MXU_WIDTH)
            q_scr[rows, j * LANES:(j + 1) * LANES] = (
                rope(proj(IN_Q + piece)[rows, off:off + LANES], rows) * q_scale).astype(BF16)
        krows = slice(CHUNK + c * CHUNK, CHUNK + (c + 1) * CHUNK)
        k_scr[krows, :] = rope(proj(IN_KV)[rows, 0:KV_WIDTH], rows).astype(BF16)
        v_scr[krows, :] = proj(IN_KV)[rows, KV_WIDTH:2 * KV_WIDTH].astype(BF16)

    dt_scr[...] = _dot_nt(wdt_ref[...], h)

    h_scr[...] = norm_next()

    a_neg = -jnp.exp(alog_ref[...])
    causal = (lax.broadcasted_iota(jnp.int32, (CHUNK, CHUNK), 0)
              >= lax.broadcasted_iota(jnp.int32, (CHUNK, CHUNK), 1))
    x_cols = slice(0, SSM_WIDTH)
    bc_cols = slice(SSM_WIDTH, CONV_CH)
    group_head = lax.broadcasted_iota(jnp.int32, (CHUNK, GROUP_WIDTH), 1) // SSM_HEAD_DIM

    def conv_silu(row0, cols):
        win = xbc_scr[pl.ds(row0, SUBLANES + CHUNK), cols]
        assert CONV_WIDTH == 4
        w0, w1, w2, w3 = (convw_ref[t:t + 1, cols] for t in range(CONV_WIDTH))
        back2 = pltpu.roll(win, 2, axis=0)
        even = w3 * win + w1 * back2
        odd = pltpu.roll(w2 * win + w0 * back2, 1, axis=0)
        conv = (even + odd)[SUBLANES:SUBLANES + CHUNK] + convb_ref[:, cols]
        return _silu(conv)

    n_ql = Q_PER_KV * CHUNK
    key_idx = lax.broadcasted_iota(jnp.int32, (CHUNK, n_ql), 0)
    qry_idx = lax.broadcasted_iota(jnp.int32, (CHUNK, n_ql), 1) % CHUNK
    upper = key_idx > qry_idx
    lane_head = lax.broadcasted_iota(jnp.int32, (1, n_ql), 1) // CHUNK

    scalars = {}

    def ssd_scalars(c):
        if not scalars:
            dts, lhs = [], []
            pad = jnp.zeros((SUBLANES, CHUNK), F32)
            for cc in range(n_chunks):
                dt = _softplus(dt_scr[0:SSM_HEADS, cc * CHUNK:(cc + 1) * CHUNK] + dtb_ref[...])
                dts.append(dt)
                lhs += _split_bf16(dt * a_neg, 3) + [pad]
            cs_all = _dot(jnp.concatenate(lhs, axis=0).astype(BF16), scan_ref[...])
            a_css, packed = [], []
            for cc, dt in enumerate(dts):
                cs = cs_all[4 * SUBLANES * cc:4 * SUBLANES * (cc + 1)]
                cs = cs[0:8] + cs[8:16] + cs[16:24]
                a_cs = cs[:, :CHUNK]
                a_tot = cs[:, CHUNK:]
                ea = jnp.exp(a_cs)
                w_end = dt * jnp.exp(a_tot - a_cs)
                rows = (_split_bf16(a_cs, 3) + _split_bf16(ea, 2) + _split_bf16(w_end, 2)
                        + _split_bf16(dt, 2))
                assert len(rows) == n_slots
                rows.append(jnp.zeros((LANES - n_slots * HEAD_SLOT, CHUNK), F32))
                packed.append(jnp.concatenate(rows, axis=0).T.astype(BF16))
                a_css.append(a_cs)
            scalars["a_cs"] = a_css
            scalars["ex"] = _dot(jnp.concatenate(packed, axis=0), expand_ref[...])
        return scalars["a_cs"][c], scalars["ex"][c * CHUNK:(c + 1) * CHUNK]

    def chunk_streams(c, tick):
        r0 = c * CHUNK
        prev_bias = jnp.where((i == 0) & (c == 0), MASK_VALUE, 0.0).astype(F32)

        def attn_stages(g):
            if g == 0:
                rope_rows(c)
            kband = k_scr[pl.ds(r0, 2 * CHUNK), :]
            vband = v_scr[pl.ds(r0, 2 * CHUNK), :]
            qblk = q_scr[pl.ds(r0, CHUNK), :]
            qg = jnp.concatenate(
                [qblk[:, (g * Q_PER_KV + j) * HEAD_DIM:(g * Q_PER_KV + j + 1) * HEAD_DIM]
                 for j in range(Q_PER_KV)], axis=0)
            if g == 1:
                tick()
            s_t = _dot_nt(kband[:, g * HEAD_DIM:(g + 1) * HEAD_DIM], qg)
            yield
            sf = jnp.where(upper, s_t[:CHUNK] + prev_bias, s_t[CHUNK:])
            sink = jnp.zeros((1, n_ql), F32)
            for j in range(Q_PER_KV):
                sink = jnp.where(lane_head == j, sinks_ref[g * Q_PER_KV + j], sink)
            m = jnp.maximum(jnp.max(sf, axis=0, keepdims=True), sink)
            pf = jnp.exp(sf - m)
            denom = jnp.sum(pf, axis=0, keepdims=True) + jnp.exp(sink - m)
            pfb = pf.astype(BF16)
            zero = jnp.zeros((), BF16)
            p_t = jnp.concatenate([jnp.where(upper, pfb, zero), jnp.where(upper, zero, pfb)],
                                  axis=0)
            yield
            tick()
            o_t = _dot_tn(vband[:, g * HEAD_DIM:(g + 1) * HEAD_DIM], p_t)
            o_t = o_t * (1.0 / denom)
            yield
            tiles = []
            for pp in range(Q_PER_KV // 2):
                blk = jnp.concatenate([o_t[:, (2 * pp) * CHUNK:(2 * pp + 1) * CHUNK],
                                       o_t[:, (2 * pp + 1) * CHUNK:(2 * pp + 2) * CHUNK]], axis=0)
                tiles.append(blk.T)
            mix_scr[cur, pl.ds(r0, CHUNK),
                    g * Q_PER_KV * HEAD_DIM:(g + 1) * Q_PER_KV * HEAD_DIM] = (
                jnp.concatenate(tiles, axis=1).astype(BF16))

        def ssd_stages():
            xs = conv_silu(r0, x_cols)
            bc = conv_silu(r0, bc_cols)
            yield
            a_cs, ex = ssd_scalars(c)

            def ex_block(name):
                _, _, col0, width = layout[name]
                return ex[:, col0:col0 + width]

            acs_b = ex_block("acs")
            ea_e = ex_block("ea")
            xdt = (xs * ex_block("dt")).astype(BF16)
            xw = (xs * ex_block("w")).astype(BF16)
            yield
            ys = []
            for g in range(SSM_GROUPS):
                gs = slice(g * GROUP_WIDTH, (g + 1) * GROUP_WIDTH)
                bg_t = bc[:, g * D_STATE:(g + 1) * D_STATE].T.astype(BF16)
                cg = bc[:, (SSM_GROUPS + g) * D_STATE:(SSM_GROUPS + g + 1) * D_STATE].astype(BF16)
                cb = _dot(cg, bg_t)
                ms = []
                xm = []
                for j in range(HEADS_PER_GROUP):
                    hh = g * HEADS_PER_GROUP + j
                    seg = acs_b[:, hh * LANES:(hh + 1) * LANES] - a_cs[hh:hh + 1, :]
                    dec = jnp.exp(jnp.where(causal, seg, MASK_VALUE))
                    ms.append((cb * dec).astype(BF16))
                    xm.append(jnp.where(group_head == j, xdt[:, gs], jnp.zeros((), BF16)))
                tick()
                y_diag = _dot(jnp.concatenate(ms, axis=1), jnp.concatenate(xm, axis=0))
                yield
                s_old = state_scr[g]
                y_off = _dot(cg, s_old.astype(BF16))
                state_scr[g] = s_old * ea_e[CHUNK - 1:CHUNK, gs] + _dot(bg_t, xw[:, gs])
                yield
                y_g = y_diag + y_off * ea_e[:, gs] + xs[:, gs] * dskip_ref[:, gs]
                y_g = y_g * _silu(proj(IN_Z + g)[r0:r0 + CHUNK, :])
                y_g = y_g * lax.rsqrt(jnp.mean(y_g * y_g, axis=-1, keepdims=True) + EPS)
                ys.append(y_g * snw_ref[:, gs])
                yield
            mix_scr[cur, pl.ds(r0, CHUNK), ATTN_WIDTH:] = jnp.concatenate(ys, axis=1).astype(BF16)

        return {"a0": attn_stages(0), "a1": attn_stages(1), "s": ssd_stages()}

    def filler_units():
        for j in range(max(N_AHEAD_PIECES, N_OUT_PIECES)):
            if j < N_AHEAD_PIECES:
                raw_scr[j] = _dot(h_scr[...], win_ref[:, _piece(IN_XBC + j)])
                yield
            if j < N_OUT_PIECES:
                y_scr[j] = _dot(mix_scr[1 - cur], wout_ref[:, _piece(j)])
                yield

    fill = filler_units()
    def tick(units=1):
        for _ in range(units):
            next(fill, None)
    streams = {}
    for c in range(n_chunks):
        for name, gen in chunk_streams(c, tick).items():
            streams[c, name] = gen
    for key in _stage_order(n_chunks):
        next(streams[key], None)
    for stream in streams.values():
        assert next(stream, "done") == "done"
    for _ in fill:
        pass

    k_scr[0:CHUNK, :] = k_scr[ts:ts + CHUNK, :]
    v_scr[0:CHUNK, :] = v_scr[ts:ts + CHUNK, :]
    xbc_scr[0:SUBLANES, :] = xbc_scr[ts:ts + SUBLANES, :]

    gate1 = modp_ref[0][2:3]
    for j in range(N_OUT_PIECES):
        cols = slice(j * MXU_WIDTH, (j + 1) * MXU_WIDTH)
        o_ref[0, :, cols] = xp_ref[0, :, cols] + gate1[:, cols] * y_scr[j]


def _const_spec(shape):
    nd = len(shape)
    return pl.BlockSpec(shape, lambda *_: (0,) * nd, pipeline_mode=pl.Buffered(1))


def _mixer(x, pos4, mod3, n1w, invf, win, wdt, convw, convb, dtb, alog, dskip, sinks, snw, wout, scan,
           expand):
    b, s, d = x.shape
    ts = MIXER_TILE
    assert ts % CHUNK == 0 and s % ts == 0
    tpr = s // ts
    n_tiles = b * tpr

    def nxt(step):
        t = jnp.minimum(step + 1, n_tiles - 1)
        return t // tpr, t % tpr

    def prv(step):
        t = jnp.maximum(step - 1, 0)
        return t // tpr, t % tpr

    def cur(step):
        t = jnp.minimum(step, n_tiles - 1)
        return t // tpr, t % tpr

    in_specs = [
        pl.BlockSpec((1, ts, d), lambda st: (*nxt(st), 0)),
        pl.BlockSpec((1, ts, d), lambda st: (*prv(st), 0)),
        pl.BlockSpec((1, 1, ts // ROPE_PACK, LANES), lambda st: (*cur(st), 0, 0)),
        pl.BlockSpec((1, N_MOD, d), lambda st: (nxt(st)[0], 0, 0)),
        pl.BlockSpec((1, N_MOD, d), lambda st: (prv(st)[0], 0, 0)),
        _const_spec(n1w.shape), _const_spec(invf.shape), _const_spec(win.shape),
        _const_spec(wdt.shape),
        _const_spec(convw.shape), _const_spec(convb.shape), _const_spec(dtb.shape),
        _const_spec(alog.shape), _const_spec(dskip.shape),
        pl.BlockSpec(memory_space=pltpu.SMEM),
        _const_spec(snw.shape), _const_spec(wout.shape), _const_spec(scan.shape),
        _const_spec(expand.shape),
    ]
    scratch = [
        pltpu.VMEM((N_AHEAD_PIECES, ts, MXU_WIDTH), F32),
        pltpu.VMEM((ts, d), BF16),
        pltpu.VMEM((N_OUT_PIECES, ts, MXU_WIDTH), F32),
        pltpu.VMEM((ts, ATTN_WIDTH), BF16),
        pltpu.VMEM((CHUNK + ts, KV_WIDTH), BF16),
        pltpu.VMEM((CHUNK + ts, KV_WIDTH), BF16),
        pltpu.VMEM((SUBLANES + ts, CONV_CH), F32),
        pltpu.VMEM((2 * SUBLANES, ts), F32),
        pltpu.VMEM((SSM_GROUPS, D_STATE, GROUP_WIDTH), F32),
        pltpu.VMEM((2, ts, MIX_WIDTH), BF16),
    ]
    return pl.pallas_call(
        lambda *refs: _mixer_kernel(tpr, *refs),
        grid=(n_tiles + 1,),
        in_specs=in_specs,
        out_specs=pl.BlockSpec((1, ts, d), lambda st: (*prv(st), 0)),
        out_shape=jax.ShapeDtypeStruct((b, s, d), F32),
        scratch_shapes=scratch,
        compiler_params=pltpu.CompilerParams(
            dimension_semantics=("arbitrary",), vmem_limit_bytes=VMEM_LIMIT_BYTES),
        name="mixer",
    )(x, x, pos4, mod3, mod3, n1w, invf, win, wdt, convw, convb, dtb, alog, dskip, sinks, snw, wout,
      scan, expand)


def _ffn_kernel(x_ref, mod_ref, n2w_ref, wgu_ref, wd_ref, nfw_ref, o_ref):
    mod = mod_ref[0]
    shift2, scale2, gate2 = mod[3:4], mod[4:5], mod[5:6]
    x = x_ref[0]
    h = _rmsnorm_mod(x, n2w_ref[...], shift2, scale2).astype(BF16)
    g = _dot(h, wgu_ref[:, :D_FF])
    u = _dot(h, wgu_ref[:, D_FF:])
    act = (_silu(g) * u).astype(BF16)
    x2 = x + gate2 * _dot(act, wd_ref[...])
    ms = jnp.mean(x2 * x2, axis=-1, keepdims=True)
    o_ref[0] = x2 * lax.rsqrt(ms + EPS) * nfw_ref[...]


def _ffn(x1, mod3, n2w, wgu, wd, nfw):
    b, s, d = x1.shape
    tm = FFN_TILE
    return pl.pallas_call(
        _ffn_kernel,
        grid=(b, s // tm),
        in_specs=[
            pl.BlockSpec((1, tm, d), lambda bb, i: (bb, i, 0)),
            pl.BlockSpec((1, N_MOD, d), lambda bb, i: (bb, 0, 0)),
            _const_spec(n2w.shape), _const_spec(wgu.shape), _const_spec(wd.shape),
            _const_spec(nfw.shape),
        ],
        out_specs=pl.BlockSpec((1, tm, d), lambda bb, i: (bb, i, 0)),
        out_shape=jax.ShapeDtypeStruct((b, s, d), F32),
        compiler_params=pltpu.CompilerParams(
            dimension_semantics=("arbitrary", "arbitrary"), vmem_limit_bytes=VMEM_LIMIT_BYTES),
        name="ffn",
    )(x1, mod3, n2w, wgu, wd, nfw)


def _head_rows(v):
    return jnp.broadcast_to(v.astype(F32)[:, None], (SSM_HEADS, LANES))


def kernel(x, c, positions, w_ada, b_ada, norm1_w, w_in, conv_w, conv_b, dt_bias, a_log, d_skip,
           attn_sinks, ssm_norm_w, w_out, norm2_w, w_gate_up, w_down, final_norm_w):
    b, s, d = x.shape
    depth = w_ada.shape[0]
    half = HEAD_DIM // 2
    inv_freq = ROPE_THETA ** (-jnp.arange(half, dtype=F32) / half)
    invf = jnp.tile(inv_freq, LANES // half).reshape(1, LANES)
    scan = _scan_matrix()
    expand = _expand_matrix()
    ts = MIXER_TILE
    pos4 = positions.reshape(b, s // ts, ROPE_PACK, ts // ROPE_PACK).transpose(0, 1, 3, 2)
    pos4 = jnp.repeat(pos4, half, axis=-1)
    assert depth == 1, "the final norm is fused into the (single) layer's ffn call"
    layer = 0
    mod3 = _adaln_mod(c, w_ada[layer], b_ada[layer]).reshape(b, N_MOD, d)
    n_main = N_IN_PIECES * MXU_WIDTH
    assert w_in.shape[2] == n_main + SSM_HEADS
    w_main = w_in[layer][:, :n_main].astype(BF16)
    wdt = jnp.pad(w_in[layer][:, n_main:].T.astype(BF16),
                  ((0, 2 * SUBLANES - SSM_HEADS), (0, 0)))
    x1 = _mixer(
        x, pos4, mod3, norm1_w[layer].reshape(1, d), invf, w_main, wdt,
        conv_w[layer], conv_b[layer].reshape(1, CONV_CH),
        _head_rows(dt_bias[layer]), _head_rows(a_log[layer]),
        jnp.repeat(d_skip[layer].astype(F32), SSM_HEAD_DIM).reshape(1, SSM_WIDTH),
        attn_sinks[layer].astype(F32),
        ssm_norm_w[layer].reshape(1, SSM_WIDTH), w_out[layer].astype(BF16), scan, expand)
    return _ffn(x1, mod3, norm2_w[layer].reshape(1, d), w_gate_up[layer].astype(BF16),
                w_down[layer].astype(BF16), final_norm_w.reshape(1, d))
```

```python
import math

import jax
import jax.numpy as jnp
import numpy as np
from jax import lax
from jax.experimental import pallas as pl
from jax.experimental.pallas import tpu as pltpu

D_MODEL = 1024
HEAD_DIM = 64
N_Q_HEADS = 8
N_KV_HEADS = 2
Q_PER_KV = N_Q_HEADS // N_KV_HEADS
ATTN_WIDTH = N_Q_HEADS * HEAD_DIM
KV_WIDTH = N_KV_HEADS * HEAD_DIM
WINDOW = 128
ROPE_THETA = 10000.0
SSM_HEADS = 8
SSM_HEAD_DIM = 64
SSM_WIDTH = SSM_HEADS * SSM_HEAD_DIM
SSM_GROUPS = 2
HEADS_PER_GROUP = SSM_HEADS // SSM_GROUPS
GROUP_WIDTH = HEADS_PER_GROUP * SSM_HEAD_DIM
D_STATE = 128
CONV_WIDTH = 4
CHUNK = 128
CONV_CH = SSM_WIDTH + 2 * SSM_GROUPS * D_STATE
MIX_WIDTH = ATTN_WIDTH + SSM_WIDTH
D_FF = 2816
N_MOD = 6
EPS = 1e-6

LANES = 128
SUBLANES = 8
MXU_WIDTH = 256
HEAD_SLOT = 8
ROPE_PACK = LANES // (HEAD_DIM // 2)
MASK_VALUE = -1e30
VMEM_LIMIT_BYTES = 56 * 1024 * 1024

MIXER_TILE = 512
FFN_TILE = 1024
MOD_TILE = 2048

F32 = jnp.float32
BF16 = jnp.bfloat16

IN_Q = 0
IN_KV = IN_Q + ATTN_WIDTH // MXU_WIDTH
IN_Z = IN_KV + 2 * KV_WIDTH // MXU_WIDTH
IN_XBC = IN_Z + SSM_WIDTH // MXU_WIDTH
N_AHEAD_PIECES = CONV_CH // MXU_WIDTH
N_IN_PIECES = IN_XBC + N_AHEAD_PIECES
N_OUT_PIECES = D_MODEL // MXU_WIDTH
assert GROUP_WIDTH == MXU_WIDTH
assert WINDOW == CHUNK

ATTN_STAGES = 4
SSD_STAGES = 9
_SSD_NEEDS_PREV_PAST = {3: 4, 6: 7}


def _stage_order(n_chunks):
    prog = {(k, name): 0 for k in range(n_chunks) for name in ("a0", "a1", "s")}
    order = []
    n_stages = {"a0": ATTN_STAGES, "a1": ATTN_STAGES, "s": SSD_STAGES}
    while any(v < n_stages[name] for (_, name), v in prog.items()):
        for k in range(n_chunks):
            for name in ("a0", "a1"):
                if prog[k, name] < ATTN_STAGES:
                    prog[k, name] += 1
                    order.append((k, name))
        for k in range(n_chunks):
            stage = prog[k, "s"]
            need = _SSD_NEEDS_PREV_PAST.get(stage) if k else None
            if stage < SSD_STAGES and (need is None or prog[k - 1, "s"] >= need):
                prog[k, "s"] += 1
                order.append((k, "s"))
    assert all(v == (SSD_STAGES if name == "s" else ATTN_STAGES) for (_, name), v in prog.items())
    return order

_EXPAND_ITEMS = (("acs", 3, LANES), ("ea", 2, SSM_HEAD_DIM), ("w", 2, SSM_HEAD_DIM),
                 ("dt", 2, SSM_HEAD_DIM))


def _expand_layout():
    slot = 0
    col = 0
    layout = {}
    for name, parts, rep in _EXPAND_ITEMS:
        layout[name] = (slot, parts, col, rep * SSM_HEADS)
        slot += parts
        col += rep * SSM_HEADS
    return layout, slot, col


def _expand_matrix():
    layout, nslots, ncols = _expand_layout()
    assert nslots * HEAD_SLOT <= LANES
    e = np.zeros((LANES, ncols), np.float32)
    for name, parts, rep in _EXPAND_ITEMS:
        slot0, _, col0, _ = layout[name]
        for p in range(parts):
            for hh in range(SSM_HEADS):
                e[(slot0 + p) * HEAD_SLOT + hh, col0 + hh * rep:col0 + (hh + 1) * rep] = 1.0
    return jnp.asarray(e, BF16)


def _scan_matrix():
    t = np.arange(CHUNK)
    upper = (t[:, None] <= t[None, :]).astype(np.float32)
    return jnp.asarray(np.concatenate([upper, np.ones((CHUNK, CHUNK), np.float32)], axis=1), BF16)


def _silu(x):
    hx = 0.5 * x
    return hx + hx * jnp.tanh(hx)


def _softplus(x):
    return jnp.maximum(x, 0.0) + jnp.log1p(jnp.exp(-jnp.abs(x)))


def _dot(a, b):
    return jnp.dot(a, b, preferred_element_type=F32)


def _dot_nt(a, b):
    return lax.dot_general(a, b, (((1,), (1,)), ((), ())), preferred_element_type=F32)


def _dot_tn(a, b):
    return lax.dot_general(a, b, (((0,), (0,)), ((), ())), preferred_element_type=F32)


def _rmsnorm_mod(x, norm_w, shift, scale):
    ms = jnp.mean(x * x, axis=-1, keepdims=True)
    return x * lax.rsqrt(ms + EPS) * (norm_w * (1.0 + scale)) + shift


def _piece(j):
    return slice(j * MXU_WIDTH, (j + 1) * MXU_WIDTH)


def _split_bf16(v, parts):
    out = []
    r = v
    for _ in range(parts):
        p = r.astype(BF16).astype(F32)
        out.append(p)
        r = r - p
    return out


def _mod_kernel(c_ref, w_ref, b_ref, o_ref):
    sc = _silu(c_ref[...]).astype(BF16)
    o_ref[...] = _dot(sc, w_ref[...].astype(BF16)) + b_ref[...]


def _adaln_mod(c, w_ada, b_ada):
    b, d = c.shape
    n = w_ada.shape[1]
    return pl.pallas_call(
        _mod_kernel,
        grid=(n // MOD_TILE,),
        in_specs=[
            pl.BlockSpec((b, d), lambda j: (0, 0)),
            pl.BlockSpec((d, MOD_TILE), lambda j: (0, j)),
            pl.BlockSpec((1, MOD_TILE), lambda j: (0, j)),
        ],
        out_specs=pl.BlockSpec((b, MOD_TILE), lambda j: (0, j)),
        out_shape=jax.ShapeDtypeStruct((b, n), F32),
        compiler_params=pltpu.CompilerParams(
            dimension_semantics=("arbitrary",), vmem_limit_bytes=VMEM_LIMIT_BYTES),
        name="adaln_mod",
    )(c, w_ada, b_ada.reshape(1, n))


def _mixer_kernel(tiles_per_row,
                  xn_ref, xp_ref, pos_ref, modn_ref, modp_ref, n1w_ref, invf_ref, win_ref, wdt_ref,
                  convw_ref, convb_ref, dtb_ref, alog_ref, dskip_ref, sinks_ref,
                  snw_ref, wout_ref, scan_ref, expand_ref,
                  o_ref,
                  raw_scr, h_scr, y_scr, q_scr, k_scr, v_scr, xbc_scr, dt_scr, state_scr,
                  mix_scr):
    ts = xn_ref.shape[1]
    n_chunks = ts // CHUNK
    s = pl.program_id(0)
    i = lax.rem(s, tiles_per_row)
    cur = lax.rem(s, 2)
    layout, n_slots, _ = _expand_layout()

    def norm_next():
        modn = modn_ref[0]
        return _rmsnorm_mod(xn_ref[0], n1w_ref[...], modn[0:1], modn[1:2]).astype(BF16)

    @pl.when(s == 0)
    def _():
        modp = modp_ref[0]
        h0 = _rmsnorm_mod(xp_ref[0], n1w_ref[...], modp[0:1], modp[1:2]).astype(BF16)
        h_scr[...] = h0
        for j in range(N_AHEAD_PIECES):
            raw_scr[j] = _dot(h0, win_ref[:, _piece(IN_XBC + j)])
        y_scr[...] = jnp.zeros(y_scr.shape, F32)
        mix_scr[...] = jnp.zeros(mix_scr.shape, BF16)

    @pl.when(i == 0)
    def _():
        k_scr[0:CHUNK, :] = jnp.zeros((CHUNK, KV_WIDTH), BF16)
        v_scr[0:CHUNK, :] = jnp.zeros((CHUNK, KV_WIDTH), BF16)
        xbc_scr[0:SUBLANES, :] = jnp.zeros((SUBLANES, CONV_CH), F32)
        state_scr[...] = jnp.zeros(state_scr.shape, F32)

    for j in range(N_AHEAD_PIECES):
        xbc_scr[SUBLANES:SUBLANES + ts, _piece(j)] = raw_scr[j]
    h = h_scr[...]
    pieces = {j: _dot(h, win_ref[:, _piece(j)]) for j in range(IN_Q, IN_XBC)}

    def proj(j):
        return pieces[j]

    half = HEAD_DIM // 2
    ang_d = pos_ref[0, 0].astype(F32) * invf_ref[...]
    lane_d = lax.broadcasted_iota(jnp.int32, ang_d.shape, 1)

    def spread(tbl):
        blocks = []
        for k in range(ROPE_PACK):
            t = pltpu.roll(tbl, LANES - half * k, axis=1) if k else tbl
            t = jnp.where(lane_d < half, t, pltpu.roll(t, half, axis=1))
            t = jnp.where(lane_d < 2 * half, t, pltpu.roll(t, 2 * half, axis=1))
            blocks.append(t)
        return jnp.concatenate(blocks, axis=0)

    cos = spread(jnp.cos(ang_d))
    sin = spread(jnp.sin(ang_d))
    lane = lax.broadcasted_iota(jnp.int32, (ts, LANES), 1)
    first_half = (lane % HEAD_DIM) < half
    sin_signed = jnp.where(first_half, -sin, sin)

    def rope(t, rows):
        lane_c = lax.broadcasted_iota(jnp.int32, t.shape, 1)
        rot = jnp.where((lane_c % HEAD_DIM) < half, pltpu.roll(t, LANES - half, axis=1),
                        pltpu.roll(t, half, axis=1))
        return t * cos[rows] + rot * sin_signed[rows]

    q_scale = 1.0 / math.sqrt(HEAD_DIM)

    def rope_rows(c):
        rows = slice(c * CHUNK, (c + 1) * CHUNK)
        for j in range(ATTN_WIDTH // LANES):
            piece, off = divmod(j * LANES, MXU_WIDTH)
            q_scr[rows, j * LANES:(j + 1) * LANES] = (
                rope(proj(IN_Q + piece)[rows, off:off + LANES], rows) * q_scale).astype(BF16)
        krows = slice(CHUNK + c * CHUNK, CHUNK + (c + 1) * CHUNK)
        k_scr[krows, :] = rope(proj(IN_KV)[rows, 0:KV_WIDTH], rows).astype(BF16)
        v_scr[krows, :] = proj(IN_KV)[rows, KV_WIDTH:2 * KV_WIDTH].astype(BF16)

    dt_scr[...] = _dot_nt(wdt_ref[...], h)

    h_scr[...] = norm_next()

    a_neg = -jnp.exp(alog_ref[...])
    causal = (lax.broadcasted_iota(jnp.int32, (CHUNK, CHUNK), 0)
              >= lax.broadcasted_iota(jnp.int32, (CHUNK, CHUNK), 1))
    x_cols = slice(0, SSM_WIDTH)
    bc_cols = slice(SSM_WIDTH, CONV_CH)
    group_head = lax.broadcasted_iota(jnp.int32, (CHUNK, GROUP_WIDTH), 1) // SSM_HEAD_DIM

    def conv_silu(row0, cols):
        win = xbc_scr[pl.ds(row0, SUBLANES + CHUNK), cols]
        assert CONV_WIDTH == 4
        w0, w1, w2, w3 = (convw_ref[t:t + 1, cols] for t in range(CONV_WIDTH))
        back2 = pltpu.roll(win, 2, axis=0)
        even = w3 * win + w1 * back2
        odd = pltpu.roll(w2 * win + w0 * back2, 1, axis=0)
        conv = (even + odd)[SUBLANES:SUBLANES + CHUNK] + convb_ref[:, cols]
        return _silu(conv)

    n_ql = Q_PER_KV * CHUNK
    key_idx = lax.broadcasted_iota(jnp.int32, (CHUNK, n_ql), 0)
    qry_idx = lax.broadcasted_iota(jnp.int32, (CHUNK, n_ql), 1) % CHUNK
    upper = key_idx > qry_idx
    lane_head = lax.broadcasted_iota(jnp.int32, (1, n_ql), 1) // CHUNK

    scalars = {}

    def ssd_scalars(c):
        if not scalars:
            dts, lhs = [], []
            pad = jnp.zeros((SUBLANES, CHUNK), F32)
            for cc in range(n_chunks):
                dt = _softplus(dt_scr[0:SSM_HEADS, cc * CHUNK:(cc + 1) * CHUNK] + dtb_ref[...])
                dts.append(dt)
                lhs += _split_bf16(dt * a_neg, 3) + [pad]
            cs_all = _dot(jnp.concatenate(lhs, axis=0).astype(BF16), scan_ref[...])
            a_css, packed = [], []
            for cc, dt in enumerate(dts):
                cs = cs_all[4 * SUBLANES * cc:4 * SUBLANES * (cc + 1)]
                cs = cs[0:8] + cs[8:16] + cs[16:24]
                a_cs = cs[:, :CHUNK]
                a_tot = cs[:, CHUNK:]
                ea = jnp.exp(a_cs)
                w_end = dt * jnp.exp(a_tot - a_cs)
                rows = (_split_bf16(a_cs, 3) + _split_bf16(ea, 2) + _split_bf16(w_end, 2)
                        + _split_bf16(dt, 2))
                assert len(rows) == n_slots
                rows.append(jnp.zeros((LANES - n_slots * HEAD_SLOT, CHUNK), F32))
                packed.append(jnp.concatenate(rows, axis=0).T.astype(BF16))
                a_css.append(a_cs)
            scalars["a_cs"] = a_css
            scalars["ex"] = _dot(jnp.concatenate(packed, axis=0), expand_ref[...])
        return scalars["a_cs"][c], scalars["ex"][c * CHUNK:(c + 1) * CHUNK]

    def chunk_streams(c, tick):
        r0 = c * CHUNK
        prev_bias = jnp.where((i == 0) & (c == 0), MASK_VALUE, 0.0).astype(F32)

        def attn_stages(g):
            if g == 0:
                rope_rows(c)
            kband = k_scr[pl.ds(r0, 2 * CHUNK), :]
            vband = v_scr[pl.ds(r0, 2 * CHUNK), :]
            qblk = q_scr[pl.ds(r0, CHUNK), :]
            qg = jnp.concatenate(
                [qblk[:, (g * Q_PER_KV + j) * HEAD_DIM:(g * Q_PER_KV + j + 1) * HEAD_DIM]
                 for j in range(Q_PER_KV)], axis=0)
            if g == 1:
                tick()
            s_t = _dot_nt(kband[:, g * HEAD_DIM:(g + 1) * HEAD_DIM], qg)
            yield
            sf = jnp.where(upper, s_t[:CHUNK] + prev_bias, s_t[CHUNK:])
            sink = jnp.zeros((1, n_ql), F32)
            for j in range(Q_PER_KV):
                sink = jnp.where(lane_head == j, sinks_ref[g * Q_PER_KV + j], sink)
            m = jnp.maximum(jnp.max(sf, axis=0, keepdims=True), sink)
            pf = jnp.exp(sf - m)
            denom = jnp.sum(pf, axis=0, keepdims=True) + jnp.exp(sink - m)
            pfb = pf.astype(BF16)
            zero = jnp.zeros((), BF16)
            p_t = jnp.concatenate([jnp.where(upper, pfb, zero), jnp.where(upper, zero, pfb)],
                                  axis=0)
            yield
            tick()
            o_t = _dot_tn(vband[:, g * HEAD_DIM:(g + 1) * HEAD_DIM], p_t)
            o_t = o_t * (1.0 / denom)
            yield
            tiles = []
            for pp in range(Q_PER_KV // 2):
                blk = jnp.concatenate([o_t[:, (2 * pp) * CHUNK:(2 * pp + 1) * CHUNK],
                                       o_t[:, (2 * pp + 1) * CHUNK:(2 * pp + 2) * CHUNK]], axis=0)
                tiles.append(blk.T)
            mix_scr[cur, pl.ds(r0, CHUNK),
                    g * Q_PER_KV * HEAD_DIM:(g + 1) * Q_PER_KV * HEAD_DIM] = (
                jnp.concatenate(tiles, axis=1).astype(BF16))

        def ssd_stages():
            xs = conv_silu(r0, x_cols)
            bc = conv_silu(r0, bc_cols)
            yield
            a_cs, ex = ssd_scalars(c)

            def ex_block(name):
                _, _, col0, width = layout[name]
                return ex[:, col0:col0 + width]

            acs_b = ex_block("acs")
            ea_e = ex_block("ea")
            xdt = (xs * ex_block("dt")).astype(BF16)
            xw = (xs * ex_block("w")).astype(BF16)
            yield
            ys = []
            for g in range(SSM_GROUPS):
                gs = slice(g * GROUP_WIDTH, (g + 1) * GROUP_WIDTH)
                bg_t = bc[:, g * D_STATE:(g + 1) * D_STATE].T.astype(BF16)
                cg = bc[:, (SSM_GROUPS + g) * D_STATE:(SSM_GROUPS + g + 1) * D_STATE].astype(BF16)
                cb = _dot(cg, bg_t)
                ms = []
                xm = []
                for j in range(HEADS_PER_GROUP):
                    hh = g * HEADS_PER_GROUP + j
                    seg = acs_b[:, hh * LANES:(hh + 1) * LANES] - a_cs[hh:hh + 1, :]
                    dec = jnp.exp(jnp.where(causal, seg, MASK_VALUE))
                    ms.append((cb * dec).astype(BF16))
                    xm.append(jnp.where(group_head == j, xdt[:, gs], jnp.zeros((), BF16)))
                tick()
                y_diag = _dot(jnp.concatenate(ms, axis=1), jnp.concatenate(xm, axis=0))
                yield
                s_old = state_scr[g]
                y_off = _dot(cg, s_old.astype(BF16))
                state_scr[g] = s_old * ea_e[CHUNK - 1:CHUNK, gs] + _dot(bg_t, xw[:, gs])
                yield
                y_g = y_diag + y_off * ea_e[:, gs] + xs[:, gs] * dskip_ref[:, gs]
                y_g = y_g * _silu(proj(IN_Z + g)[r0:r0 + CHUNK, :])
                y_g = y_g * lax.rsqrt(jnp.mean(y_g * y_g, axis=-1, keepdims=True) + EPS)
                ys.append(y_g * snw_ref[:, gs])
                yield
            mix_scr[cur, pl.ds(r0, CHUNK), ATTN_WIDTH:] = jnp.concatenate(ys, axis=1).astype(BF16)

        return {"a0": attn_stages(0), "a1": attn_stages(1), "s": ssd_stages()}

    def filler_units():
        for j in range(max(N_AHEAD_PIECES, N_OUT_PIECES)):
            if j < N_AHEAD_PIECES:
                raw_scr[j] = _dot(h_scr[...], win_ref[:, _piece(IN_XBC + j)])
                yield
            if j < N_OUT_PIECES:
                y_scr[j] = _dot(mix_scr[1 - cur], wout_ref[:, _piece(j)])
                yield

    fill = filler_units()

    def tick():
        next(fill, None)

    streams = {}
    for c in range(n_chunks):
        for name, gen in chunk_streams(c, tick).items():
            streams[c, name] = gen
    for key in _stage_order(n_chunks):
        next(streams[key], None)
    for stream in streams.values():
        assert next(stream, "done") == "done"
    for _ in fill:
        pass

    k_scr[0:CHUNK, :] = k_scr[ts:ts + CHUNK, :]
    v_scr[0:CHUNK, :] = v_scr[ts:ts + CHUNK, :]
    xbc_scr[0:SUBLANES, :] = xbc_scr[ts:ts + SUBLANES, :]

    gate1 = modp_ref[0][2:3]
    for j in range(N_OUT_PIECES):
        cols = slice(j * MXU_WIDTH, (j + 1) * MXU_WIDTH)
        o_ref[0, :, cols] = xp_ref[0, :, cols] + gate1[:, cols] * y_scr[j]


def _const_spec(shape):
    nd = len(shape)
    return pl.BlockSpec(shape, lambda *_: (0,) * nd, pipeline_mode=pl.Buffered(1))


def _mixer(x, pos4, mod3, n1w, invf, win, wdt, convw, convb, dtb, alog, dskip, sinks, snw, wout, scan,
           expand):
    b, s, d = x.shape
    ts = MIXER_TILE
    assert ts % CHUNK == 0 and s % ts == 0
    tpr = s // ts
    n_tiles = b * tpr

    def nxt(step):
        t = jnp.minimum(step + 1, n_tiles - 1)
        return t // tpr, t % tpr

    def prv(step):
        t = jnp.maximum(step - 1, 0)
        return t // tpr, t % tpr

    def cur(step):
        t = jnp.minimum(step, n_tiles - 1)
        return t // tpr, t % tpr

    in_specs = [
        pl.BlockSpec((1, ts, d), lambda st: (*nxt(st), 0)),
        pl.BlockSpec((1, ts, d), lambda st: (*prv(st), 0)),
        pl.BlockSpec((1, 1, ts // ROPE_PACK, LANES), lambda st: (*cur(st), 0, 0)),
        pl.BlockSpec((1, N_MOD, d), lambda st: (nxt(st)[0], 0, 0)),
        pl.BlockSpec((1, N_MOD, d), lambda st: (prv(st)[0], 0, 0)),
        _const_spec(n1w.shape), _const_spec(invf.shape), _const_spec(win.shape),
        _const_spec(wdt.shape),
        _const_spec(convw.shape), _const_spec(convb.shape), _const_spec(dtb.shape),
        _const_spec(alog.shape), _const_spec(dskip.shape),
        pl.BlockSpec(memory_space=pltpu.SMEM),
        _const_spec(snw.shape), _const_spec(wout.shape), _const_spec(scan.shape),
        _const_spec(expand.shape),
    ]
    scratch = [
        pltpu.VMEM((N_AHEAD_PIECES, ts, MXU_WIDTH), F32),
        pltpu.VMEM((ts, d), BF16),
        pltpu.VMEM((N_OUT_PIECES, ts, MXU_WIDTH), F32),
        pltpu.VMEM((ts, ATTN_WIDTH), BF16),
        pltpu.VMEM((CHUNK + ts, KV_WIDTH), BF16),
        pltpu.VMEM((CHUNK + ts, KV_WIDTH), BF16),
        pltpu.VMEM((SUBLANES + ts, CONV_CH), F32),
        pltpu.VMEM((2 * SUBLANES, ts), F32),
        pltpu.VMEM((SSM_GROUPS, D_STATE, GROUP_WIDTH), F32),
        pltpu.VMEM((2, ts, MIX_WIDTH), BF16),
    ]
    return pl.pallas_call(
        lambda *refs: _mixer_kernel(tpr, *refs),
        grid=(n_tiles + 1,),
        in_specs=in_specs,
        out_specs=pl.BlockSpec((1, ts, d), lambda st: (*prv(st), 0)),
        out_shape=jax.ShapeDtypeStruct((b, s, d), F32),
        scratch_shapes=scratch,
        compiler_params=pltpu.CompilerParams(
            dimension_semantics=("arbitrary",), vmem_limit_bytes=VMEM_LIMIT_BYTES),
        name="mixer",
    )(x, x, pos4, mod3, mod3, n1w, invf, win, wdt, convw, convb, dtb, alog, dskip, sinks, snw, wout,
      scan, expand)


def _ffn_kernel(x_ref, mod_ref, n2w_ref, wgu_ref, wd_ref, nfw_ref, o_ref):
    mod = mod_ref[0]
    shift2, scale2, gate2 = mod[3:4], mod[4:5], mod[5:6]
    x = x_ref[0]
    h = _rmsnorm_mod(x, n2w_ref[...], shift2, scale2).astype(BF16)
    g = _dot(h, wgu_ref[:, :D_FF])
    u = _dot(h, wgu_ref[:, D_FF:])
    act = (_silu(g) * u).astype(BF16)
    x2 = x + gate2 * _dot(act, wd_ref[...])
    ms = jnp.mean(x2 * x2, axis=-1, keepdims=True)
    o_ref[0] = x2 * lax.rsqrt(ms + EPS) * nfw_ref[...]


def _ffn(x1, mod3, n2w, wgu, wd, nfw):
    b, s, d = x1.shape
    tm = FFN_TILE
    return pl.pallas_call(
        _ffn_kernel,
        grid=(b, s // tm),
        in_specs=[
            pl.BlockSpec((1, tm, d), lambda bb, i: (bb, i, 0)),
            pl.BlockSpec((1, N_MOD, d), lambda bb, i: (bb, 0, 0)),
            _const_spec(n2w.shape), _const_spec(wgu.shape), _const_spec(wd.shape),
            _const_spec(nfw.shape),
        ],
        out_specs=pl.BlockSpec((1, tm, d), lambda bb, i: (bb, i, 0)),
        out_shape=jax.ShapeDtypeStruct((b, s, d), F32),
        compiler_params=pltpu.CompilerParams(
            dimension_semantics=("arbitrary", "arbitrary"), vmem_limit_bytes=VMEM_LIMIT_BYTES),
        name="ffn",
    )(x1, mod3, n2w, wgu, wd, nfw)


def _head_rows(v):
    return jnp.broadcast_to(v.astype(F32)[:, None], (SSM_HEADS, LANES))


def kernel(x, c, positions, w_ada, b_ada, norm1_w, w_in, conv_w, conv_b, dt_bias, a_log, d_skip,
           attn_sinks, ssm_norm_w, w_out, norm2_w, w_gate_up, w_down, final_norm_w):
    b, s, d = x.shape
    depth = w_ada.shape[0]
    half = HEAD_DIM // 2
    inv_freq = ROPE_THETA ** (-jnp.arange(half, dtype=F32) / half)
    invf = jnp.tile(inv_freq, LANES // half).reshape(1, LANES)
    scan = _scan_matrix()
    expand = _expand_matrix()
    ts = MIXER_TILE
    pos4 = positions.reshape(b, s // ts, ROPE_PACK, ts // ROPE_PACK).transpose(0, 1, 3, 2)
    pos4 = jnp.repeat(pos4, half, axis=-1)
    assert depth == 1, "the final norm is fused into the (single) layer's ffn call"
    layer = 0
    mod3 = _adaln_mod(c, w_ada[layer], b_ada[layer]).reshape(b, N_MOD, d)
    n_main = N_IN_PIECES * MXU_WIDTH
    assert w_in.shape[2] == n_main + SSM_HEADS
    w_main = w_in[layer][:, :n_main].astype(BF16)
    wdt = jnp.pad(w_in[layer][:, n_main:].T.astype(BF16),
                  ((0, 2 * SUBLANES - SSM_HEADS), (0, 0)))
    x1 = _mixer(
        x, pos4, mod3, norm1_w[layer].reshape(1, d), invf, w_main, wdt,
        conv_w[layer], conv_b[layer].reshape(1, CONV_CH),
        _head_rows(dt_bias[layer]), _head_rows(a_log[layer]),
        jnp.repeat(d_skip[layer].astype(F32), SSM_HEAD_DIM).reshape(1, SSM_WIDTH),
        attn_sinks[layer].astype(F32),
        ssm_norm_w[layer].reshape(1, SSM_WIDTH), w_out[layer].astype(BF16), scan, expand)
    return _ffn(x1, mod3, norm2_w[layer].reshape(1, d), w_gate_up[layer].astype(BF16),
                w_down[layer].astype(BF16), final_norm_w.reshape(1, d))
```

```python
import math

import jax
import jax.numpy as jnp
import numpy as np
from jax import lax
from jax.experimental import pallas as pl
from jax.experimental.pallas import tpu as pltpu

D_MODEL = 1024
HEAD_DIM = 64
N_Q_HEADS = 8
N_KV_HEADS = 2
Q_PER_KV = N_Q_HEADS // N_KV_HEADS
ATTN_WIDTH = N_Q_HEADS * HEAD_DIM
KV_WIDTH = N_KV_HEADS * HEAD_DIM
WINDOW = 128
ROPE_THETA = 10000.0
SSM_HEADS = 8
SSM_HEAD_DIM = 64
SSM_WIDTH = SSM_HEADS * SSM_HEAD_DIM
SSM_GROUPS = 2
HEADS_PER_GROUP = SSM_HEADS // SSM_GROUPS
GROUP_WIDTH = HEADS_PER_GROUP * SSM_HEAD_DIM
D_STATE = 128
CONV_WIDTH = 4
CHUNK = 128
CONV_CH = SSM_WIDTH + 2 * SSM_GROUPS * D_STATE
MIX_WIDTH = ATTN_WIDTH + SSM_WIDTH
D_FF = 2816
N_MOD = 6
EPS = 1e-6

LANES = 128
SUBLANES = 8
MXU_WIDTH = 256
HEAD_SLOT = 8
ROPE_PACK = LANES // (HEAD_DIM // 2)
MASK_VALUE = -1e30
VMEM_LIMIT_BYTES = 56 * 1024 * 1024

MIXER_TILE = 512
FFN_TILE = 1024
MOD_TILE = 2048

F32 = jnp.float32
BF16 = jnp.bfloat16

IN_Q = 0
IN_KV = IN_Q + ATTN_WIDTH // MXU_WIDTH
IN_Z = IN_KV + 2 * KV_WIDTH // MXU_WIDTH
IN_XBC = IN_Z + SSM_WIDTH // MXU_WIDTH
N_AHEAD_PIECES = CONV_CH // MXU_WIDTH
N_IN_PIECES = IN_XBC + N_AHEAD_PIECES
N_OUT_PIECES = D_MODEL // MXU_WIDTH
assert GROUP_WIDTH == MXU_WIDTH
assert WINDOW == CHUNK

ATTN_STAGES = 4
SSD_STAGES = 9
_SSD_NEEDS_PREV_PAST = {3: 4, 6: 7}


def _stage_order(n_chunks):
    prog = {(k, name): 0 for k in range(n_chunks) for name in ("a0", "a1", "s")}
    order = []
    n_stages = {"a0": ATTN_STAGES, "a1": ATTN_STAGES, "s": SSD_STAGES}
    while any(v < n_stages[name] for (_, name), v in prog.items()):
        for k in range(n_chunks):
            for name in ("a0", "a1"):
                if prog[k, name] < ATTN_STAGES:
                    prog[k, name] += 1
                    order.append((k, name))
        for k in range(n_chunks):
            stage = prog[k, "s"]
            need = _SSD_NEEDS_PREV_PAST.get(stage) if k else None
            if stage < SSD_STAGES and (need is None or prog[k - 1, "s"] >= need):
                prog[k, "s"] += 1
                order.append((k, "s"))
    assert all(v == (SSD_STAGES if name == "s" else ATTN_STAGES) for (_, name), v in prog.items())
    return order

_EXPAND_ITEMS = (("acs", 3, LANES), ("ea", 2, SSM_HEAD_DIM), ("w", 2, SSM_HEAD_DIM),
                 ("dt", 2, SSM_HEAD_DIM))


def _expand_layout():
    slot = 0
    col = 0
    layout = {}
    for name, parts, rep in _EXPAND_ITEMS:
        layout[name] = (slot, parts, col, rep * SSM_HEADS)
        slot += parts
        col += rep * SSM_HEADS
    return layout, slot, col


def _expand_matrix():
    layout, nslots, ncols = _expand_layout()
    assert nslots * HEAD_SLOT <= LANES
    e = np.zeros((LANES, ncols), np.float32)
    for name, parts, rep in _EXPAND_ITEMS:
        slot0, _, col0, _ = layout[name]
        for p in range(parts):
            for hh in range(SSM_HEADS):
                e[(slot0 + p) * HEAD_SLOT + hh, col0 + hh * rep:col0 + (hh + 1) * rep] = 1.0
    return jnp.asarray(e, BF16)


def _scan_matrix():
    t = np.arange(CHUNK)
    upper = (t[:, None] <= t[None, :]).astype(np.float32)
    return jnp.asarray(np.concatenate([upper, np.ones((CHUNK, CHUNK), np.float32)], axis=1), BF16)


def _silu(x):
    hx = 0.5 * x
    return hx + hx * jnp.tanh(hx)


def _softplus(x):
    return jnp.maximum(x, 0.0) + jnp.log1p(jnp.exp(-jnp.abs(x)))


def _dot(a, b):
    return jnp.dot(a, b, preferred_element_type=F32)


def _dot_nt(a, b):
    return lax.dot_general(a, b, (((1,), (1,)), ((), ())), preferred_element_type=F32)


def _dot_tn(a, b):
    return lax.dot_general(a, b, (((0,), (0,)), ((), ())), preferred_element_type=F32)


def _rmsnorm_mod(x, norm_w, shift, scale):
    ms = jnp.mean(x * x, axis=-1, keepdims=True)
    return x * lax.rsqrt(ms + EPS) * (norm_w * (1.0 + scale)) + shift


def _piece(j):
    return slice(j * MXU_WIDTH, (j + 1) * MXU_WIDTH)


def _split_bf16(v, parts):
    out = []
    r = v
    for _ in range(parts):
        p = r.astype(BF16).astype(F32)
        out.append(p)
        r = r - p
    return out


def _mod_kernel(c_ref, w_ref, b_ref, o_ref):
    sc = _silu(c_ref[...]).astype(BF16)
    o_ref[...] = _dot(sc, w_ref[...].astype(BF16)) + b_ref[...]


def _adaln_mod(c, w_ada, b_ada):
    b, d = c.shape
    n = w_ada.shape[1]
    return pl.pallas_call(
        _mod_kernel,
        grid=(n // MOD_TILE,),
        in_specs=[
            pl.BlockSpec((b, d), lambda j: (0, 0)),
            pl.BlockSpec((d, MOD_TILE), lambda j: (0, j)),
            pl.BlockSpec((1, MOD_TILE), lambda j: (0, j)),
        ],
        out_specs=pl.BlockSpec((b, MOD_TILE), lambda j: (0, j)),
        out_shape=jax.ShapeDtypeStruct((b, n), F32),
        compiler_params=pltpu.CompilerParams(
            dimension_semantics=("arbitrary",), vmem_limit_bytes=VMEM_LIMIT_BYTES),
        name="adaln_mod",
    )(c, w_ada, b_ada.reshape(1, n))


def _mixer_kernel(tiles_per_row,
                  xn_ref, xp_ref, pos_ref, modn_ref, modp_ref, n1w_ref, invf_ref, win_ref, wdt_ref,
                  convw_ref, convb_ref, dtb_ref, alog_ref, dskip_ref, sinks_ref,
                  snw_ref, wout_ref, scan_ref, expand_ref,
                  o_ref,
                  raw_scr, h_scr, y_scr, q_scr, k_scr, v_scr, xbc_scr, dt_scr, state_scr,
                  mix_scr):
    ts = xn_ref.shape[1]
    n_chunks = ts // CHUNK
    s = pl.program_id(0)
    i = lax.rem(s, tiles_per_row)
    cur = lax.rem(s, 2)
    layout, n_slots, _ = _expand_layout()

    def norm_next():
        modn = modn_ref[0]
        return _rmsnorm_mod(xn_ref[0], n1w_ref[...], modn[0:1], modn[1:2]).astype(BF16)

    @pl.when(s == 0)
    def _():
        modp = modp_ref[0]
        h0 = _rmsnorm_mod(xp_ref[0], n1w_ref[...], modp[0:1], modp[1:2]).astype(BF16)
        h_scr[...] = h0
        for j in range(N_AHEAD_PIECES):
            raw_scr[j] = _dot(h0, win_ref[:, _piece(IN_XBC + j)])
        y_scr[...] = jnp.zeros(y_scr.shape, F32)
        mix_scr[...] = jnp.zeros(mix_scr.shape, BF16)

    @pl.when(i == 0)
    def _():
        k_scr[0:CHUNK, :] = jnp.zeros((CHUNK, KV_WIDTH), BF16)
        v_scr[0:CHUNK, :] = jnp.zeros((CHUNK, KV_WIDTH), BF16)
        xbc_scr[0:SUBLANES, :] = jnp.zeros((SUBLANES, CONV_CH), F32)
        state_scr[...] = jnp.zeros(state_scr.shape, F32)

    for j in range(N_AHEAD_PIECES):
        xbc_scr[SUBLANES:SUBLANES + ts, _piece(j)] = raw_scr[j]
    h = h_scr[...]
    pieces = {j: _dot(h, win_ref[:, _piece(j)]) for j in range(IN_Q, IN_XBC)}

    def proj(j):
        return pieces[j]

    half = HEAD_DIM // 2
    ang_d = pos_ref[0, 0].astype(F32) * invf_ref[...]
    lane_d = lax.broadcasted_iota(jnp.int32, ang_d.shape, 1)

    def spread(tbl):
        blocks = []
        for k in range(ROPE_PACK):
            t = pltpu.roll(tbl, LANES - half * k, axis=1) if k else tbl
            t = jnp.where(lane_d < half, t, pltpu.roll(t, half, axis=1))
            t = jnp.where(lane_d < 2 * half, t, pltpu.roll(t, 2 * half, axis=1))
            blocks.append(t)
        return jnp.concatenate(blocks, axis=0)

    cos = spread(jnp.cos(ang_d))
    sin = spread(jnp.sin(ang_d))
    lane = lax.broadcasted_iota(jnp.int32, (ts, LANES), 1)
    first_half = (lane % HEAD_DIM) < half
    sin_signed = jnp.where(first_half, -sin, sin)

    def rope(t, rows):
        lane_c = lax.broadcasted_iota(jnp.int32, t.shape, 1)
        rot = jnp.where((lane_c % HEAD_DIM) < half, pltpu.roll(t, LANES - half, axis=1),
                        pltpu.roll(t, half, axis=1))
        return t * cos[rows] + rot * sin_signed[rows]

    q_scale = 1.0 / math.sqrt(HEAD_DIM)

    def rope_rows(c):
        rows = slice(c * CHUNK, (c + 1) * CHUNK)
        for j in range(ATTN_WIDTH // LANES):
            piece, off = divmod(j * LANES, MXU_WIDTH)
            q_scr[rows, j * LANES:(j + 1) * LANES] = (
                rope(proj(IN_Q + piece)[rows, off:off + LANES], rows) * q_scale).astype(BF16)
        krows = slice(CHUNK + c * CHUNK, CHUNK + (c + 1) * CHUNK)
        k_scr[krows, :] = rope(proj(IN_KV)[rows, 0:KV_WIDTH], rows).astype(BF16)
        v_scr[krows, :] = proj(IN_KV)[rows, KV_WIDTH:2 * KV_WIDTH].astype(BF16)

    dt_scr[...] = _dot_nt(wdt_ref[...], h)

    h_scr[...] = norm_next()

    a_neg = -jnp.exp(alog_ref[...])
    causal = (lax.broadcasted_iota(jnp.int32, (CHUNK, CHUNK), 0)
              >= lax.broadcasted_iota(jnp.int32, (CHUNK, CHUNK), 1))
    x_cols = slice(0, SSM_WIDTH)
    bc_cols = slice(SSM_WIDTH, CONV_CH)
    group_head = lax.broadcasted_iota(jnp.int32, (CHUNK, GROUP_WIDTH), 1) // SSM_HEAD_DIM

    def conv_silu(row0, cols):
        win = xbc_scr[pl.ds(row0, SUBLANES + CHUNK), cols]
        assert CONV_WIDTH == 4
        w0, w1, w2, w3 = (convw_ref[t:t + 1, cols] for t in range(CONV_WIDTH))
        back2 = pltpu.roll(win, 2, axis=0)
        even = w3 * win + w1 * back2
        odd = pltpu.roll(w2 * win + w0 * back2, 1, axis=0)
        conv = (even + odd)[SUBLANES:SUBLANES + CHUNK] + convb_ref[:, cols]
        return _silu(conv)

    n_ql = Q_PER_KV * CHUNK
    key_idx = lax.broadcasted_iota(jnp.int32, (CHUNK, n_ql), 0)
    qry_idx = lax.broadcasted_iota(jnp.int32, (CHUNK, n_ql), 1) % CHUNK
    upper = key_idx > qry_idx
    lane_head = lax.broadcasted_iota(jnp.int32, (1, n_ql), 1) // CHUNK

    scalars = {}

    def ssd_scalars(c):
        if not scalars:
            dts, lhs = [], []
            pad = jnp.zeros((SUBLANES, CHUNK), F32)
            for cc in range(n_chunks):
                dt = _softplus(dt_scr[0:SSM_HEADS, cc * CHUNK:(cc + 1) * CHUNK] + dtb_ref[...])
                dts.append(dt)
                lhs += _split_bf16(dt * a_neg, 3) + [pad]
            cs_all = _dot(jnp.concatenate(lhs, axis=0).astype(BF16), scan_ref[...])
            a_css, packed = [], []
            for cc, dt in enumerate(dts):
                cs = cs_all[4 * SUBLANES * cc:4 * SUBLANES * (cc + 1)]
                cs = cs[0:8] + cs[8:16] + cs[16:24]
                a_cs = cs[:, :CHUNK]
                a_tot = cs[:, CHUNK:]
                ea = jnp.exp(a_cs)
                w_end = dt * jnp.exp(a_tot - a_cs)
                rows = (_split_bf16(a_cs, 3) + _split_bf16(ea, 2) + _split_bf16(w_end, 2)
                        + _split_bf16(dt, 2))
                assert len(rows) == n_slots
                rows.append(jnp.zeros((LANES - n_slots * HEAD_SLOT, CHUNK), F32))
                packed.append(jnp.concatenate(rows, axis=0).T.astype(BF16))
                a_css.append(a_cs)
            scalars["a_cs"] = a_css
            scalars["ex"] = _dot(jnp.concatenate(packed, axis=0), expand_ref[...])
        return scalars["a_cs"][c], scalars["ex"][c * CHUNK:(c + 1) * CHUNK]

    def chunk_streams(c, tick):
        r0 = c * CHUNK
        prev_bias = jnp.where((i == 0) & (c == 0), MASK_VALUE, 0.0).astype(F32)

        def attn_stages(g):
            if g == 0:
                rope_rows(c)
            kband = k_scr[pl.ds(r0, 2 * CHUNK), :]
            vband = v_scr[pl.ds(r0, 2 * CHUNK), :]
            qblk = q_scr[pl.ds(r0, CHUNK), :]
            qg = jnp.concatenate(
                [qblk[:, (g * Q_PER_KV + j) * HEAD_DIM:(g * Q_PER_KV + j + 1) * HEAD_DIM]
                 for j in range(Q_PER_KV)], axis=0)
            if g == 1:
                tick()
            s_t = _dot_nt(kband[:, g * HEAD_DIM:(g + 1) * HEAD_DIM], qg)
            yield
            sf = jnp.where(upper, s_t[:CHUNK] + prev_bias, s_t[CHUNK:])
            sink = jnp.zeros((1, n_ql), F32)
            for j in range(Q_PER_KV):
                sink = jnp.where(lane_head == j, sinks_ref[g * Q_PER_KV + j], sink)
            m = jnp.maximum(jnp.max(sf, axis=0, keepdims=True), sink)
            pf = jnp.exp(sf - m)
            denom = jnp.sum(pf, axis=0, keepdims=True) + jnp.exp(sink - m)
            pfb = pf.astype(BF16)
            zero = jnp.zeros((), BF16)
            p_t = jnp.concatenate([jnp.where(upper, pfb, zero), jnp.where(upper, zero, pfb)],
                                  axis=0)
            yield
            tick()
            o_t = _dot_tn(vband[:, g * HEAD_DIM:(g + 1) * HEAD_DIM], p_t)
            o_t = o_t * (1.0 / denom)
            yield
            tiles = []
            for pp in range(Q_PER_KV // 2):
                blk = jnp.concatenate([o_t[:, (2 * pp) * CHUNK:(2 * pp + 1) * CHUNK],
                                       o_t[:, (2 * pp + 1) * CHUNK:(2 * pp + 2) * CHUNK]], axis=0)
                tiles.append(blk.T)
            mix_scr[cur, pl.ds(r0, CHUNK),
                    g * Q_PER_KV * HEAD_DIM:(g + 1) * Q_PER_KV * HEAD_DIM] = (
                jnp.concatenate(tiles, axis=1).astype(BF16))

        def ssd_stages():
            xs = conv_silu(r0, x_cols)
            bc = conv_silu(r0, bc_cols)
            yield
            a_cs, ex = ssd_scalars(c)

            def ex_block(name):
                _, _, col0, width = layout[name]
                return ex[:, col0:col0 + width]

            acs_b = ex_block("acs")
            ea_e = ex_block("ea")
            xdt = (xs * ex_block("dt")).astype(BF16)
            xw = (xs * ex_block("w")).astype(BF16)
            yield
            ys = []
            for g in range(SSM_GROUPS):
                gs = slice(g * GROUP_WIDTH, (g + 1) * GROUP_WIDTH)
                bg_t = bc[:, g * D_STATE:(g + 1) * D_STATE].T.astype(BF16)
                cg = bc[:, (SSM_GROUPS + g) * D_STATE:(SSM_GROUPS + g + 1) * D_STATE].astype(BF16)
                cb = _dot(cg, bg_t)
                ms = []
                xm = []
                for j in range(HEADS_PER_GROUP):
                    hh = g * HEADS_PER_GROUP + j
                    seg = acs_b[:, hh * LANES:(hh + 1) * LANES] - a_cs[hh:hh + 1, :]
                    dec = jnp.exp(jnp.where(causal, seg, MASK_VALUE))
                    ms.append((cb * dec).astype(BF16))
                    xm.append(jnp.where(group_head == j, xdt[:, gs], jnp.zeros((), BF16)))
                tick()
                y_diag = _dot(jnp.concatenate(ms, axis=1), jnp.concatenate(xm, axis=0))
                yield
                s_old = state_scr[g]
                y_off = _dot(cg, s_old.astype(BF16))
                state_scr[g] = s_old * ea_e[CHUNK - 1:CHUNK, gs] + _dot(bg_t, xw[:, gs])
                yield
                y_g = y_diag + y_off * ea_e[:, gs] + xs[:, gs] * dskip_ref[:, gs]
                y_g = y_g * _silu(proj(IN_Z + g)[r0:r0 + CHUNK, :])
                y_g = y_g * lax.rsqrt(jnp.mean(y_g * y_g, axis=-1, keepdims=True) + EPS)
                ys.append(y_g * snw_ref[:, gs])
                yield
            mix_scr[cur, pl.ds(r0, CHUNK), ATTN_WIDTH:] = jnp.concatenate(ys, axis=1).astype(BF16)

        return {"a0": attn_stages(0), "a1": attn_stages(1), "s": ssd_stages()}

    def filler_units():
        for j in range(max(N_AHEAD_PIECES, N_OUT_PIECES)):
            if j < N_AHEAD_PIECES:
                raw_scr[j] = _dot(h_scr[...], win_ref[:, _piece(IN_XBC + j)])
                yield
            if j < N_OUT_PIECES:
                y_scr[j] = _dot(mix_scr[1 - cur], wout_ref[:, _piece(j)])
                yield

    fill = filler_units()

    def tick():
        next(fill, None)

    streams = {}
    for c in range(n_chunks):
        for name, gen in chunk_streams(c, tick).items():
            streams[c, name] = gen
    for key in _stage_order(n_chunks):
        next(streams[key], None)
    for stream in streams.values():
        assert next(stream, "done") == "done"
    for _ in fill:
        pass

    k_scr[0:CHUNK, :] = k_scr[ts:ts + CHUNK, :]
    v_scr[0:CHUNK, :] = v_scr[ts:ts + CHUNK, :]
    xbc_scr[0:SUBLANES, :] = xbc_scr[ts:ts + SUBLANES, :]

    gate1 = modp_ref[0][2:3]
    for j in range(N_OUT_PIECES):
        cols = slice(j * MXU_WIDTH, (j + 1) * MXU_WIDTH)
        o_ref[0, :, cols] = xp_ref[0, :, cols] + gate1[:, cols] * y_scr[j]


def _const_spec(shape):
    nd = len(shape)
    return pl.BlockSpec(shape, lambda *_: (0,) * nd, pipeline_mode=pl.Buffered(1))


def _mixer(x, pos4, mod3, n1w, invf, win, wdt, convw, convb, dtb, alog, dskip, sinks, snw, wout, scan,
           expand):
    b, s, d = x.shape
    ts = MIXER_TILE
    assert ts % CHUNK == 0 and s % ts == 0
    tpr = s // ts
    n_tiles = b * tpr

    def nxt(step):
        t = jnp.minimum(step + 1, n_tiles - 1)
        return t // tpr, t % tpr

    def prv(step):
        t = jnp.maximum(step - 1, 0)
        return t // tpr, t % tpr

    def cur(step):
        t = jnp.minimum(step, n_tiles - 1)
        return t // tpr, t % tpr

    in_specs = [
        pl.BlockSpec((1, ts, d), lambda st: (*nxt(st), 0)),
        pl.BlockSpec((1, ts, d), lambda st: (*prv(st), 0)),
        pl.BlockSpec((1, 1, ts // ROPE_PACK, LANES), lambda st: (*cur(st), 0, 0)),
        pl.BlockSpec((1, N_MOD, d), lambda st: (nxt(st)[0], 0, 0)),
        pl.BlockSpec((1, N_MOD, d), lambda st: (prv(st)[0], 0, 0)),
        _const_spec(n1w.shape), _const_spec(invf.shape), _const_spec(win.shape),
        _const_spec(wdt.shape),
        _const_spec(convw.shape), _const_spec(convb.shape), _const_spec(dtb.shape),
        _const_spec(alog.shape), _const_spec(dskip.shape),
        pl.BlockSpec(memory_space=pltpu.SMEM),
        _const_spec(snw.shape), _const_spec(wout.shape), _const_spec(scan.shape),
        _const_spec(expand.shape),
    ]
    scratch = [
        pltpu.VMEM((N_AHEAD_PIECES, ts, MXU_WIDTH), F32),
        pltpu.VMEM((ts, d), BF16),
        pltpu.VMEM((N_OUT_PIECES, ts, MXU_WIDTH), F32),
        pltpu.VMEM((ts, ATTN_WIDTH), BF16),
        pltpu.VMEM((CHUNK + ts, KV_WIDTH), BF16),
        pltpu.VMEM((CHUNK + ts, KV_WIDTH), BF16),
        pltpu.VMEM((SUBLANES + ts, CONV_CH), F32),
        pltpu.VMEM((2 * SUBLANES, ts), F32),
        pltpu.VMEM((SSM_GROUPS, D_STATE, GROUP_WIDTH), F32),
        pltpu.VMEM((2, ts, MIX_WIDTH), BF16),
    ]
    return pl.pallas_call(
        lambda *refs: _mixer_kernel(tpr, *refs),
        grid=(n_tiles + 1,),
        in_specs=in_specs,
        out_specs=pl.BlockSpec((1, ts, d), lambda st: (*prv(st), 0)),
        out_shape=jax.ShapeDtypeStruct((b, s, d), F32),
        scratch_shapes=scratch,
        compiler_params=pltpu.CompilerParams(
            dimension_semantics=("arbitrary",), vmem_limit_bytes=VMEM_LIMIT_BYTES),
        name="mixer",
    )(x, x, pos4, mod3, mod3, n1w, invf, win, wdt, convw, convb, dtb, alog, dskip, sinks, snw, wout,
      scan, expand)


def _ffn_kernel(x_ref, mod_ref, n2w_ref, wgu_ref, wd_ref, nfw_ref, o_ref):
    mod = mod_ref[0]
    shift2, scale2, gate2 = mod[3:4], mod[4:5], mod[5:6]
    half_rows = x_ref.shape[1] // 2
    edge = None
    for part in range(2):
        rows = slice(part * half_rows, (part + 1) * half_rows)
        x = x_ref[0, rows, :]
        h = _rmsnorm_mod(x, n2w_ref[...], shift2, scale2).astype(BF16)
        g = _dot(h, wgu_ref[:, :D_FF])
        u = _dot(h, wgu_ref[:, D_FF:])
        act = (_silu(g) * u).astype(BF16)
        if edge is not None:
            act = jnp.concatenate([act[:, :LANES] + edge.astype(BF16), act[:, LANES:]], axis=1)
        x2 = x + gate2 * _dot(act, wd_ref[...])
        ms = jnp.mean(x2 * x2, axis=-1, keepdims=True)
        out = x2 * lax.rsqrt(ms + EPS) * nfw_ref[...]
        o_ref[0, rows, :] = out
        edge = jnp.minimum(jnp.abs(jnp.sum(out, axis=-1, keepdims=True)), 0.0)


def _ffn(x1, mod3, n2w, wgu, wd, nfw):
    b, s, d = x1.shape
    tm = FFN_TILE
    return pl.pallas_call(
        _ffn_kernel,
        grid=(b, s // tm),
        in_specs=[
            pl.BlockSpec((1, tm, d), lambda bb, i: (bb, i, 0)),
            pl.BlockSpec((1, N_MOD, d), lambda bb, i: (bb, 0, 0)),
            _const_spec(n2w.shape), _const_spec(wgu.shape), _const_spec(wd.shape),
            _const_spec(nfw.shape),
        ],
        out_specs=pl.BlockSpec((1, tm, d), lambda bb, i: (bb, i, 0)),
        out_shape=jax.ShapeDtypeStruct((b, s, d), F32),
        compiler_params=pltpu.CompilerParams(
            dimension_semantics=("arbitrary", "arbitrary"), vmem_limit_bytes=VMEM_LIMIT_BYTES),
        name="ffn",
    )(x1, mod3, n2w, wgu, wd, nfw)


def _head_rows(v):
    return jnp.broadcast_to(v.astype(F32)[:, None], (SSM_HEADS, LANES))


def kernel(x, c, positions, w_ada, b_ada, norm1_w, w_in, conv_w, conv_b, dt_bias, a_log, d_skip,
           attn_sinks, ssm_norm_w, w_out, norm2_w, w_gate_up, w_down, final_norm_w):
    b, s, d = x.shape
    depth = w_ada.shape[0]
    half = HEAD_DIM // 2
    inv_freq = ROPE_THETA ** (-jnp.arange(half, dtype=F32) / half)
    invf = jnp.tile(inv_freq, LANES // half).reshape(1, LANES)
    scan = _scan_matrix()
    expand = _expand_matrix()
    ts = MIXER_TILE
    pos4 = positions.reshape(b, s // ts, ROPE_PACK, ts // ROPE_PACK).transpose(0, 1, 3, 2)
    pos4 = jnp.repeat(pos4, half, axis=-1)
    assert depth == 1, "the final norm is fused into the (single) layer's ffn call"
    layer = 0
    mod3 = _adaln_mod(c, w_ada[layer], b_ada[layer]).reshape(b, N_MOD, d)
    n_main = N_IN_PIECES * MXU_WIDTH
    assert w_in.shape[2] == n_main + SSM_HEADS
    w_main = w_in[layer][:, :n_main].astype(BF16)
    wdt = jnp.pad(w_in[layer][:, n_main:].T.astype(BF16),
                  ((0, 2 * SUBLANES - SSM_HEADS), (0, 0)))
    x1 = _mixer(
        x, pos4, mod3, norm1_w[layer].reshape(1, d), invf, w_main, wdt,
        conv_w[layer], conv_b[layer].reshape(1, CONV_CH),
        _head_rows(dt_bias[layer]), _head_rows(a_log[layer]),
        jnp.repeat(d_skip[layer].astype(F32), SSM_HEAD_DIM).reshape(1, SSM_WIDTH),
        attn_sinks[layer].astype(F32),
        ssm_norm_w[layer].reshape(1, SSM_WIDTH), w_out[layer].astype(BF16), scan, expand)
    return _ffn(x1, mod3, norm2_w[layer].reshape(1, d), w_gate_up[layer].astype(BF16),
                w_down[layer].astype(BF16), final_norm_w.reshape(1, d))
```
